```python
import jax
import jax.numpy as jnp
from jax import lax
import numpy as np

D_MODEL = 1024
BATCH = 2
SEQ = 8192
DEPTH = 2
DEC_BATCH = 128
DEC_SEQ = 8
PAST_LEN = 8192
PAGE_SIZE = 128

N_EVEN = (DEPTH + 1) // 2
N_ODD = DEPTH // 2

NSA_HEADS = 8
NSA_KV_HEADS = 2
NSA_DH = 64
NSA_REP = NSA_HEADS // NSA_KV_HEADS
CMP_STRIDE = 16
CMP_BLOCK = 2 * CMP_STRIDE
SEL_BLOCK = 64
SEL_TOPK = 16
WINDOW = 512
N_BRANCH = 3
CONV_CH = 512
CONV_W = 3
Q_COLS = NSA_HEADS * NSA_DH
KV_COLS = N_BRANCH * 2 * NSA_KV_HEADS * NSA_DH
GATE_COLS = NSA_HEADS * N_BRANCH
CONV_COLS = 3 * CONV_CH
EVEN_IN = Q_COLS + KV_COLS + GATE_COLS + CONV_COLS
EVEN_MIX = Q_COLS + CONV_CH
MLA_HEADS = 16
Q_LORA = 384
KV_LORA = 256
NOPE = 64
ROPE = 32
V_DIM = 64
ROPE_THETA = 10000.0
MLA_IN = Q_LORA + KV_LORA + ROPE
LAT = KV_LORA + ROPE
MLA_MIX = MLA_HEADS * V_DIM
MEM_TOKENS = 256
MEM_HEADS = 4
MEM_DH = D_MODEL // MEM_HEADS
N_GROUPS = 4
EXP_PER_GROUP = 4
N_EXPERTS = N_GROUPS * EXP_PER_GROUP
TOPK_IN_GROUP = 2
EXPERT_FF = 256

QBLK = 128
EPS = 1e-6
NEG = -1e30
TINY = 1e-30
FORCE_BONUS = 1e4

kernel_name = 'nsa_shortconv_mla_hmoe_decode_step'

F32 = jnp.float32
I32 = jnp.int32


def rmsnorm(x, g):
    xf = x.astype(F32)
    y = xf * lax.rsqrt(jnp.mean(xf * xf, axis=-1, keepdims=True) + EPS)
    return (y * g.astype(F32)).astype(x.dtype)


def masked_softmax(s, mask):
    s = jnp.where(mask, s, NEG)
    m = jnp.max(s, axis=-1, keepdims=True)
    p = jnp.exp(s - m) * mask
    return p / jnp.maximum(jnp.sum(p, axis=-1, keepdims=True), TINY)


def rope(x, pos):
    half = ROPE // 2
    inv = ROPE_THETA ** (-jnp.arange(half, dtype=F32) / half)
    ang = pos.astype(F32)[:, None] * inv[None, :]
    shp = (1, ang.shape[0]) + (1,) * (x.ndim - 3) + (half,)
    cos = jnp.cos(ang).reshape(shp)
    sin = jnp.sin(ang).reshape(shp)
    xf = x.astype(F32)
    x1, x2 = xf[..., :half], xf[..., half:]
    return jnp.concatenate([x1 * cos - x2 * sin, x1 * sin + x2 * cos], axis=-1).astype(x.dtype)


def compress_blocks(k, w, pe):
    b, lp, g, dh = k.shape
    kc = k.reshape(b, lp // CMP_STRIDE, CMP_STRIDE, g, dh)
    first = jnp.einsum('bcjgd,jde->bcge', kc[:, :-1], w[:CMP_STRIDE])
    second = jnp.einsum('bcjgd,jde->bcge', kc[:, 1:], w[CMP_STRIDE:])
    bias = jnp.einsum('jd,jde->e', pe, w)
    return first + second + bias


def to_sel_blocks(k):
    b, lp, g, dh = k.shape
    return k.reshape(b, lp // SEL_BLOCK, SEL_BLOCK, g, dh).transpose(0, 3, 1, 2, 4)


def short_conv(u_ext, w):
    t = u_ext.shape[1] - (CONV_W - 1)
    y = w[0] * u_ext[:, 0:t]
    for j in range(1, CONV_W):
        y = y + w[j] * u_ext[:, j:j + t]
    return y


def nsa_attend(q, qpos, ck, cv, ks_blk, vs_blk, kw, vw, kwpos, gates):
    b, tq = q.shape[0], q.shape[1]
    g, r, dh = NSA_KV_HEADS, NSA_REP, NSA_DH
    qg = q.reshape(b, tq, g, r, dh) * (dh ** -0.5)
    n_cmp = ck.shape[1]
    cmp_end = CMP_STRIDE * jnp.arange(n_cmp, dtype=I32) + (CMP_BLOCK - 1)
    s = jnp.einsum('bqgrd,bngd->bqgrn', qg, ck).astype(F32)
    cmask = (cmp_end[None, :] <= qpos[:, None])[None, :, None, None, :]
    p_cmp = masked_softmax(s, cmask)
    o_cmp = jnp.einsum('bqgrn,bngd->bqgrd', p_cmp.astype(cv.dtype), cv)
    imp = jnp.sum(p_cmp, axis=3)
    n_sel = ks_blk.shape[2]
    chunk = (jnp.pad(imp, ((0, 0), (0, 0), (0, 0), (0, 1)))
             + jnp.pad(imp, ((0, 0), (0, 0), (0, 0), (1, 0))))
    sel_imp = chunk.reshape(b, tq, g, n_sel, SEL_BLOCK // CMP_STRIDE).sum(-1)
    blk = jnp.arange(n_sel, dtype=I32)[None, :]
    qblk = (qpos // SEL_BLOCK)[:, None]
    valid = (blk <= qblk)[None, :, None, :]
    forced = ((blk == 0) | (blk == qblk) | (blk == qblk - 1))[None, :, None, :]
    score = jnp.where(forced, sel_imp + FORCE_BONUS, jnp.where(valid, sel_imp, NEG))
    k_top = min(SEL_TOPK, n_sel)
    _, idx = lax.top_k(score, k_top)
    bi = jnp.arange(b)[:, None, None, None]
    gi = jnp.arange(g)[None, None, :, None]
    ks = ks_blk[bi, gi, idx]
    vs = vs_blk[bi, gi, idx]
    spos = idx[..., None] * SEL_BLOCK + jnp.arange(SEL_BLOCK, dtype=I32)
    smask = (spos <= qpos[None, :, None, None, None]).reshape(b, tq, g, 1, k_top * SEL_BLOCK)
    s = jnp.einsum('bqgrd,bqgksd->bqgrks', qg, ks).astype(F32).reshape(b, tq, g, r, k_top * SEL_BLOCK)
    p = masked_softmax(s, smask).reshape(b, tq, g, r, k_top, SEL_BLOCK)
    o_slc = jnp.einsum('bqgrks,bqgksd->bqgrd', p.astype(vs.dtype), vs)
    s = jnp.einsum('bqgrd,blgd->bqgrl', qg, kw).astype(F32)
    dist = qpos[:, None] - kwpos[None, :]
    wmask = ((dist >= 0) & (dist < WINDOW) & (kwpos[None, :] >= 0))[None, :, None, None, :]
    p = masked_softmax(s, wmask)
    o_win = jnp.einsum('bqgrl,blgd->bqgrd', p.astype(vw.dtype), vw)
    gw = gates.reshape(b, tq, g, r, N_BRANCH, 1)
    o = gw[..., 0, :] * o_cmp + gw[..., 1, :] * o_slc + gw[..., 2, :] * o_win
    return o.reshape(b, tq, NSA_HEADS * dh)


def even_project(h, w_in):
    b, t = h.shape[0], h.shape[1]
    z = h @ w_in
    q = z[..., :Q_COLS].reshape(b, t, NSA_HEADS, NSA_DH)
    o = Q_COLS
    kv = z[..., o:o + KV_COLS].reshape(b, t, N_BRANCH, 2, NSA_KV_HEADS, NSA_DH)
    o += KV_COLS
    gates = jax.nn.sigmoid(z[..., o:o + GATE_COLS].astype(F32)).astype(h.dtype)
    gates = gates.reshape(b, t, NSA_HEADS, N_BRANCH)
    o += GATE_COLS
    gb, gc, hv = jnp.split(z[..., o:], 3, axis=-1)
    return q, kv, gates, gb, gc * hv


def even_mixer_prompt(h, w_in, w_out, cmp_w, cmp_pe, conv_w):
    b, t, _ = h.shape
    q, kv, gates, gb, u = even_project(h, w_in)
    ck = compress_blocks(kv[:, :, 0, 0], cmp_w[0], cmp_pe[0])
    cv = compress_blocks(kv[:, :, 0, 1], cmp_w[1], cmp_pe[1])
    ks_blk = to_sel_blocks(kv[:, :, 1, 0])
    vs_blk = to_sel_blocks(kv[:, :, 1, 1])
    kwp = jnp.pad(kv[:, :, 2], ((0, 0), (WINDOW, 0), (0, 0), (0, 0), (0, 0)))
    nq = t // QBLK

    def body(args):
        i, qb, gbk = args
        q0 = i * QBLK
        qpos = q0 + jnp.arange(QBLK, dtype=I32)
        kwb = lax.dynamic_slice_in_dim(kwp, q0, WINDOW + QBLK, axis=1)
        kwpos = q0 - WINDOW + jnp.arange(WINDOW + QBLK, dtype=I32)
        return nsa_attend(qb, qpos, ck, cv, ks_blk, vs_blk, kwb[:, :, 0], kwb[:, :, 1], kwpos, gbk)

    qbs = q.reshape(b, nq, QBLK, NSA_HEADS, NSA_DH).swapaxes(0, 1)
    gbs = gates.reshape(b, nq, QBLK, NSA_HEADS, N_BRANCH).swapaxes(0, 1)
    o = lax.map(body, (jnp.arange(nq, dtype=I32), qbs, gbs))
    o_nsa = o.swapaxes(0, 1).reshape(b, t, Q_COLS)
    u_ext = jnp.pad(u, ((0, 0), (CONV_W - 1, 0), (0, 0)))
    o_conv = gb * short_conv(u_ext, conv_w)
    y = jnp.concatenate([o_nsa, o_conv], axis=-1) @ w_out
    wb = min(WINDOW, t)
    return y, (kv[:, :, 0], kv[:, :, 1], kv[:, t - wb:, 2], u[:, t - (CONV_W - 1):])


def even_mixer_sample(h, pool_cmp, pool_slc, e, win_buf, conv_buf, page_table, w_in, w_out, cmp_w, cmp_pe, conv_w):
    b, s, _ = h.shape
    q, kv, gates, gb, u = even_project(h, w_in)
    past = page_table.shape[1] * pool_cmp.shape[2]
    qpos = past + jnp.arange(s, dtype=I32)
    pad = (-(past + s)) % SEL_BLOCK

    def full_seq(pool, new):
        old = pool[e, page_table].reshape((b, past) + new.shape[2:])
        return jnp.pad(jnp.concatenate([old, new], axis=1), ((0, 0), (0, pad), (0, 0), (0, 0), (0, 0)))

    cmp_full = full_seq(pool_cmp, kv[:, :, 0])
    ck = compress_blocks(cmp_full[:, :, 0], cmp_w[0], cmp_pe[0])
    cv = compress_blocks(cmp_full[:, :, 1], cmp_w[1], cmp_pe[1])
    slc_full = full_seq(pool_slc, kv[:, :, 1])
    ks_blk = to_sel_blocks(slc_full[:, :, 0])
    vs_blk = to_sel_blocks(slc_full[:, :, 1])
    wb = win_buf.shape[1]
    win_all = jnp.concatenate([win_buf, kv[:, :, 2]], axis=1)
    kwpos = past - wb + jnp.arange(wb + s, dtype=I32)
    o_nsa = nsa_attend(q, qpos, ck, cv, ks_blk, vs_blk, win_all[:, :, 0], win_all[:, :, 1], kwpos, gates)
    u_ext = jnp.concatenate([conv_buf, u], axis=1)
    o_conv = gb * short_conv(u_ext, conv_w)
    y = jnp.concatenate([o_nsa, o_conv], axis=-1) @ w_out
    return y, (kv[:, :, 0], kv[:, :, 1], win_all[:, s:], u_ext[:, s:])


def mla_project(h, pos, w_in, q_norm, w_uq, kv_norm):
    b, t = h.shape[0], h.shape[1]
    z = h @ w_in
    cq = rmsnorm(z[..., :Q_LORA], q_norm)
    ckv = rmsnorm(z[..., Q_LORA:Q_LORA + KV_LORA], kv_norm)
    kpe = rope(z[..., Q_LORA + KV_LORA:], pos)
    qf = (cq @ w_uq).reshape(b, t, MLA_HEADS, NOPE + ROPE)
    q_nope = qf[..., :NOPE]
    q_pe = rope(qf[..., NOPE:], pos)
    lat = jnp.concatenate([ckv, kpe], axis=-1)
    return q_nope, q_pe, lat


def mla_prompt(h, w_in, q_norm, w_uq, kv_norm, w_uk, w_uv, w_out):
    b, t, _ = h.shape
    pos = jnp.arange(t, dtype=I32)
    q_nope, q_pe, lat = mla_project(h, pos, w_in, q_norm, w_uq, kv_norm)
    ckv, kpe = lat[..., :KV_LORA], lat[..., KV_LORA:]
    k_nope = jnp.einsum('btc,chd->bthd', ckv, w_uk)
    v = jnp.einsum('btc,chd->bthd', ckv, w_uv)
    scale = (NOPE + ROPE) ** -0.5
    outs = []
    for i in range(t // QBLK):
        q0, q1 = i * QBLK, (i + 1) * QBLK
        s = (jnp.einsum('bqhd,bkhd->bhqk', q_nope[:, q0:q1], k_nope[:, :q1])
             + jnp.einsum('bqhr,bkr->bhqk', q_pe[:, q0:q1], kpe[:, :q1])).astype(F32) * scale
        mask = jnp.arange(q1, dtype=I32)[None, :] <= (q0 + jnp.arange(QBLK, dtype=I32))[:, None]
        p = masked_softmax(s, mask)
        outs.append(jnp.einsum('bhqk,bkhd->bqhd', p.astype(v.dtype), v[:, :q1]))
    o = jnp.concatenate(outs, axis=1).reshape(b, t, MLA_MIX)
    return o @ w_out, lat


def mla_sample(h, pool, o_idx, page_table, w_in, q_norm, w_uq, kv_norm, w_uk, w_uv, w_out):
    b, s, _ = h.shape
    past = page_table.shape[1] * pool.shape[2]
    pos = past + jnp.arange(s, dtype=I32)
    q_nope, q_pe, lat = mla_project(h, pos, w_in, q_norm, w_uq, kv_norm)
    old = pool[o_idx, page_table].reshape(b, past, LAT)
    q_cat = jnp.concatenate([jnp.einsum('bqhd,chd->bqhc', q_nope, w_uk), q_pe], axis=-1)
    scale = (NOPE + ROPE) ** -0.5
    s_all = jnp.concatenate([jnp.einsum('bqhc,bkc->bhqk', q_cat, old),
                             jnp.einsum('bqhc,bkc->bhqk', q_cat, lat)], axis=-1).astype(F32) * scale
    causal = jnp.arange(s, dtype=I32)[None, :] <= jnp.arange(s, dtype=I32)[:, None]
    mask = jnp.concatenate([jnp.ones((s, past), dtype=bool), causal], axis=-1)
    p = masked_softmax(s_all, mask).astype(lat.dtype)
    o_lat = (jnp.einsum('bhqk,bkc->bqhc', p[..., :past], old)
             + jnp.einsum('bhqk,bkc->bqhc', p[..., past:], lat))[..., :KV_LORA]
    o = jnp.einsum('bqhc,chd->bqhd', o_lat, w_uv).reshape(b, s, MLA_MIX)
    return o @ w_out, lat


def mem_project(mem, w_kv):
    b, m, _ = mem.shape
    return (mem @ w_kv).reshape(b, m, 2, MEM_HEADS, MEM_DH)


def mem_attend(h, kv, w_q, w_o):
    b, t, _ = h.shape
    q = (h @ w_q).reshape(b, t, MEM_HEADS, MEM_DH) * (MEM_DH ** -0.5)
    s = jnp.einsum('bqhd,bmhd->bhqm', q, kv[:, :, 0]).astype(F32)
    p = jax.nn.softmax(s, axis=-1).astype(kv.dtype)
    o = jnp.einsum('bhqm,bmhd->bqhd', p, kv[:, :, 1]).reshape(b, t, MEM_HEADS * MEM_DH)
    return o @ w_o


def hier_moe(h, w_group, b_group, w_expert, b_expert, w13, w2):
    shp = h.shape
    x = h.reshape(-1, shp[-1])
    n = x.shape[0]
    g_logit = (x @ w_group).astype(F32) + b_group.astype(F32)
    g_prob = jax.nn.softmax(g_logit, axis=-1)
    g_idx = jnp.argmax(g_logit, axis=-1)
    g_gate = jnp.take_along_axis(g_prob, g_idx[:, None], axis=-1)
    e_logit = ((x @ w_expert).astype(F32) + b_expert.astype(F32)).reshape(n, N_GROUPS, EXP_PER_GROUP)
    e_logit = jnp.take_along_axis(e_logit, g_idx[:, None, None], axis=1)[:, 0]
    e_prob = jax.nn.softmax(e_logit, axis=-1)
    top_p, top_i = lax.top_k(e_prob, TOPK_IN_GROUP)
    wts = top_p / jnp.sum(top_p, axis=-1, keepdims=True) * g_gate
    combine = jnp.sum(jax.nn.one_hot(g_idx[:, None] * EXP_PER_GROUP + top_i, N_EXPERTS, dtype=F32)
                      * wts[..., None], axis=1)
    hcat = jnp.einsum('nd,edf->nef', x, w13)
    a, bb = jnp.split(hcat, 2, axis=-1)
    act = jax.nn.silu(a) * bb * combine[..., None].astype(x.dtype)
    return jnp.einsum('nef,efd->nd', act, w2).reshape(shp)


def setup_inputs(seed: int = 0) -> dict:
    key = jax.random.key(seed)
    keys = iter(jax.random.split(key, 48))

    def nk():
        return next(keys)

    def w(shape, fan_in):
        return jax.random.normal(nk(), shape, F32) * (fan_in ** -0.5)

    def gain(shape):
        return 1.0 + 0.05 * jax.random.normal(nk(), shape, F32)

    def small(shape, scale):
        return scale * jax.random.normal(nk(), shape, F32)

    n_pages = PAST_LEN // PAGE_SIZE
    n_used = DEC_BATCH * n_pages
    n_pool = n_used + max(1, n_used // 4)
    wb = min(WINDOW, PAST_LEN)
    page_table = jax.random.permutation(nk(), n_pool)[:n_used].reshape(DEC_BATCH, n_pages).astype(I32)
    return {
        'x_prompt': jax.random.normal(nk(), (BATCH, SEQ, D_MODEL), F32),
        'x_sample': jax.random.normal(nk(), (DEC_BATCH, DEC_SEQ, D_MODEL), F32),
        'mem_prompt': jax.random.normal(nk(), (BATCH, MEM_TOKENS, D_MODEL), F32),
        'cache_nsa_cmp_kv': jax.random.normal(nk(), (N_EVEN, n_pool, PAGE_SIZE, 2, NSA_KV_HEADS, NSA_DH), F32),
        'cache_nsa_slc_kv': jax.random.normal(nk(), (N_EVEN, n_pool, PAGE_SIZE, 2, NSA_KV_HEADS, NSA_DH), F32),
        'state_nsa_win_kv': jax.random.normal(nk(), (N_EVEN, DEC_BATCH, wb, 2, NSA_KV_HEADS, NSA_DH), F32),
        'state_conv': jax.random.normal(nk(), (N_EVEN, DEC_BATCH, CONV_W - 1, CONV_CH), F32),
        'cache_mla_latent': jax.random.normal(nk(), (N_ODD, n_pool, PAGE_SIZE, LAT), F32),
        'cache_mem_kv': jax.random.normal(nk(), (DEPTH, DEC_BATCH, MEM_TOKENS, 2, MEM_HEADS, MEM_DH), F32),
        'page_table': page_table,
        'norm_mix': gain((DEPTH, D_MODEL)),
        'norm_mem': gain((DEPTH, D_MODEL)),
        'norm_ffn': gain((DEPTH, D_MODEL)),
        'norm_final': gain((D_MODEL,)),
        'e_w_in': w((N_EVEN, D_MODEL, EVEN_IN), D_MODEL),
        'e_w_out': w((N_EVEN, EVEN_MIX, D_MODEL), EVEN_MIX),
        'e_cmp_w': w((N_EVEN, 2, CMP_BLOCK, NSA_DH, NSA_DH), CMP_BLOCK * NSA_DH),
        'e_cmp_pe': small((N_EVEN, 2, CMP_BLOCK, NSA_DH), 0.2),
        'e_conv_w': w((N_EVEN, CONV_W, CONV_CH), CONV_W),
        'o_w_in': w((N_ODD, D_MODEL, MLA_IN), D_MODEL),
        'o_q_norm': gain((N_ODD, Q_LORA)),
        'o_w_uq': w((N_ODD, Q_LORA, MLA_HEADS * (NOPE + ROPE)), Q_LORA),
        'o_kv_norm': gain((N_ODD, KV_LORA)),
        'o_w_uk': w((N_ODD, KV_LORA, MLA_HEADS, NOPE), KV_LORA),
        'o_w_uv': w((N_ODD, KV_LORA, MLA_HEADS, V_DIM), KV_LORA),
        'o_w_out': w((N_ODD, MLA_MIX, D_MODEL), MLA_MIX),
        'mem_w_q': w((DEPTH, D_MODEL, MEM_HEADS * MEM_DH), D_MODEL),
        'mem_w_kv': w((DEPTH, D_MODEL, 2 * MEM_HEADS * MEM_DH), D_MODEL),
        'mem_w_o': w((DEPTH, MEM_HEADS * MEM_DH, D_MODEL), MEM_HEADS * MEM_DH),
        'moe_w_group': w((DEPTH, D_MODEL, N_GROUPS), D_MODEL),
        'moe_b_group': small((DEPTH, N_GROUPS), 0.01),
        'moe_w_expert': w((DEPTH, D_MODEL, N_EXPERTS), D_MODEL),
        'moe_b_expert': small((DEPTH, N_EXPERTS), 0.01),
        'moe_w13': w((DEPTH, N_EXPERTS, D_MODEL, 2 * EXPERT_FF), D_MODEL),
        'moe_w2': w((DEPTH, N_EXPERTS, EXPERT_FF, D_MODEL), EXPERT_FF),
    }


def reference(x_prompt, x_sample, mem_prompt, cache_nsa_cmp_kv, cache_nsa_slc_kv, state_nsa_win_kv,
              state_conv, cache_mla_latent, cache_mem_kv, page_table, norm_mix, norm_mem, norm_ffn,
              norm_final, e_w_in, e_w_out, e_cmp_w, e_cmp_pe, e_conv_w, o_w_in, o_q_norm, o_w_uq,
              o_kv_norm, o_w_uk, o_w_uv, o_w_out, mem_w_q, mem_w_kv, mem_w_o, moe_w_group,
              moe_b_group, moe_w_expert, moe_b_expert, moe_w13, moe_w2):
    yp, ys = x_prompt, x_sample
    p_cmp, p_slc, p_win, p_conv, p_mla, p_mem = [], [], [], [], [], []
    s_cmp, s_slc, s_win, s_conv, s_mla = [], [], [], [], []
    for l in range(DEPTH):
        hp = rmsnorm(yp, norm_mix[l])
        hs = rmsnorm(ys, norm_mix[l])
        if l % 2 == 0:
            e = l // 2
            dp, (c_p, sl_p, w_p, cv_p) = even_mixer_prompt(hp, e_w_in[e], e_w_out[e], e_cmp_w[e], e_cmp_pe[e], e_conv_w[e])
            ds, (c_s, sl_s, w_s, cv_s) = even_mixer_sample(hs, cache_nsa_cmp_kv, cache_nsa_slc_kv, e, state_nsa_win_kv[e],
                                                           state_conv[e], page_table, e_w_in[e], e_w_out[e],
                                                           e_cmp_w[e], e_cmp_pe[e], e_conv_w[e])
            p_cmp.append(c_p); p_slc.append(sl_p); p_win.append(w_p); p_conv.append(cv_p)
            s_cmp.append(c_s); s_slc.append(sl_s); s_win.append(w_s); s_conv.append(cv_s)
        else:
            o = l // 2
            dp, lat_p = mla_prompt(hp, o_w_in[o], o_q_norm[o], o_w_uq[o], o_kv_norm[o], o_w_uk[o], o_w_uv[o], o_w_out[o])
            ds, lat_s = mla_sample(hs, cache_mla_latent, o, page_table, o_w_in[o], o_q_norm[o], o_w_uq[o],
                                   o_kv_norm[o], o_w_uk[o], o_w_uv[o], o_w_out[o])
            p_mla.append(lat_p); s_mla.append(lat_s)
        yp = yp + dp
        ys = ys + ds
        kv_p = mem_project(mem_prompt, mem_w_kv[l])
        p_mem.append(kv_p)
        yp = yp + mem_attend(rmsnorm(yp, norm_mem[l]), kv_p, mem_w_q[l], mem_w_o[l])
        ys = ys + mem_attend(rmsnorm(ys, norm_mem[l]), cache_mem_kv[l], mem_w_q[l], mem_w_o[l])
        yp = yp + hier_moe(rmsnorm(yp, norm_ffn[l]), moe_w_group[l], moe_b_group[l], moe_w_expert[l],
                           moe_b_expert[l], moe_w13[l], moe_w2[l])
        ys = ys + hier_moe(rmsnorm(ys, norm_ffn[l]), moe_w_group[l], moe_b_group[l], moe_w_expert[l],
                           moe_b_expert[l], moe_w13[l], moe_w2[l])
    y_prompt = rmsnorm(yp, norm_final)
    y_sample = rmsnorm(ys, norm_final)
    return (y_prompt, y_sample, jnp.stack(p_cmp), jnp.stack(p_slc), jnp.stack(p_win), jnp.stack(p_conv),
            jnp.stack(p_mla), jnp.stack(p_mem), jnp.stack(s_cmp), jnp.stack(s_slc), jnp.stack(s_win),
            jnp.stack(s_conv), jnp.stack(s_mla))
```

```python
import functools
import math

import jax
import jax.numpy as jnp
from jax import lax
from jax.experimental import pallas as pl
from jax.experimental.pallas import tpu as pltpu

F32 = jnp.float32
BF16 = jnp.bfloat16
I32 = jnp.int32

EPS = 1e-6
NEG = -1e30
TINY = 1e-30
FORCE_BONUS = 1e4

NSA_HEADS = 8
NSA_GROUPS = 2
NSA_REP = NSA_HEADS // NSA_GROUPS
NSA_DH = 64
NSA_KV_ROW = 2 * NSA_GROUPS * NSA_DH
CMP_STRIDE = 16
CMP_BLOCK = 32
SEL_BLOCK = 64
SEL_TOPK = 16
WINDOW = 512
CONV_CH = 512
CONV_W = 3
MLA_HEADS = 16
Q_LORA = 384
KV_LORA = 256
NOPE = 64
ROPE = 32
V_DIM = 64
ROPE_THETA = 10000.0
LAT = KV_LORA + ROPE
MLA_QK = 128
MEM_HEADS = 4
N_GROUPS = 4
EXP_PER_GROUP = 4
N_EXPERTS = 16
EXPERT_FF = 256
ROUTE_LANES = 128

V7X_VMEM_BYTES = 64 * 2**20


def _params(sem, vmem_mb):
    assert vmem_mb * 2**20 < V7X_VMEM_BYTES
    return pltpu.CompilerParams(dimension_semantics=sem, vmem_limit_bytes=vmem_mb * 2**20)


def _dot(a, b):
    return jnp.dot(a, b, preferred_element_type=F32)


def _dot_t(a, b):
    return lax.dot_general(a, b, (((1,), (1,)), ((), ())), preferred_element_type=F32)


def _split3(x):
    hi = x.astype(BF16)
    r1 = x - hi.astype(F32)
    mid = r1.astype(BF16)
    lo = (r1 - mid.astype(F32)).astype(BF16)
    return hi, mid, lo


def _rms(x, g):
    return x * lax.rsqrt(jnp.mean(x * x, axis=-1, keepdims=True) + EPS) * g


def _row_spec(shape, tm, period=None):
    lead = len(shape) - 2
    blk = tuple(shape[:lead]) + (tm, shape[-1])
    if period is None:
        return pl.BlockSpec(blk, lambda i, lead=lead: (0,) * lead + (i, 0))
    return pl.BlockSpec(blk, lambda i, lead=lead, p=period: (0,) * lead + (i % p, 0))


def _const_spec(shape):
    return pl.BlockSpec(tuple(shape), lambda i, n=len(shape): (0,) * n)


def _rowwise(body, rows, consts, outs, tm, name, vmem_mb=48, periods=None):
    n_rows = rows[0].shape[-2]
    assert n_rows % tm == 0
    periods = periods or {}
    in_specs = [_row_spec(a.shape, tm, periods.get(k)) for k, a in enumerate(rows)]
    in_specs += [_const_spec(c.shape) for c in consts]
    out_specs = [_row_spec(o.shape, tm) for o in outs]
    nr, nc = len(rows), len(consts)

    def kern(*refs):
        body(refs[:nr], refs[nr:nr + nc], refs[nr + nc:])

    return pl.pallas_call(
        kern, grid=(n_rows // tm,), in_specs=in_specs, out_specs=out_specs, out_shape=outs,
        compiler_params=_params(("arbitrary",), vmem_mb), name=name)(*rows, *consts)


def _sds(shape, dtype=F32):
    return jax.ShapeDtypeStruct(tuple(shape), dtype)


def _row_tile(n_rows, want):
    tm = min(want, n_rows)
    assert n_rows % tm == 0
    return tm


def _even_proj_body(rows, consts, outs):
    (x_ref,) = rows
    g_ref, wq_ref, wkv_ref, wg_ref, wc_ref = consts
    q_ref, kvc_ref, kvs_ref, kvw_ref, kvb_ref, gate_ref, gb_ref, u_ref = outs
    h = _rms(x_ref[...], g_ref[...]).astype(BF16)
    q = _dot(h, wq_ref[...]) * (NSA_DH ** -0.5)
    for hd in range(NSA_HEADS):
        q_ref[hd // NSA_REP, hd % NSA_REP] = q[:, hd * NSA_DH:(hd + 1) * NSA_DH].astype(BF16)
    kv = _dot(h, wkv_ref[...])
    kvc_ref[...] = kv[:, :NSA_KV_ROW]
    kvs_ref[...] = kv[:, NSA_KV_ROW:2 * NSA_KV_ROW]
    kvw_ref[...] = kv[:, 2 * NSA_KV_ROW:]
    for j in range(3 * 2 * NSA_GROUPS):
        kvb_ref[j] = kv[:, j * NSA_DH:(j + 1) * NSA_DH].astype(BF16)
    gate_ref[...] = 1.0 / (1.0 + jnp.exp(-_dot(h, wg_ref[...])))
    c = _dot(h, wc_ref[...])
    gb_ref[...] = c[:, :CONV_CH]
    u_ref[...] = c[:, CONV_CH:2 * CONV_CH] * c[:, 2 * CONV_CH:]


def _even_proj(x, g, w_in):
    r, d = x.shape
    qc = NSA_HEADS * NSA_DH
    kc = 3 * NSA_KV_ROW
    gc = NSA_HEADS * 3
    wq = w_in[:, :qc].astype(BF16)
    wkv = w_in[:, qc:qc + kc].astype(BF16)
    wg = w_in[:, qc + kc:qc + kc + gc].astype(BF16)
    wc = w_in[:, qc + kc + gc:].astype(BF16)
    tm = _row_tile(r, 512)
    outs = [
        _sds((NSA_GROUPS, NSA_REP, r, NSA_DH), BF16),
        _sds((r, NSA_KV_ROW)), _sds((r, NSA_KV_ROW)), _sds((r, NSA_KV_ROW)),
        _sds((12, r, NSA_DH), BF16),
        _sds((r, gc)), _sds((r, CONV_CH)), _sds((r, CONV_CH)),
    ]
    return _rowwise(_even_proj_body, [x], [g.reshape(1, d), wq, wkv, wg, wc], outs, tm, "even_proj")


def _compress_weights(cmp_w, cmp_pe):
    eye_g = jnp.eye(NSA_GROUPS, dtype=F32)
    eye_k = jnp.eye(2, dtype=F32)

    def big(wh):
        return jnp.einsum("kjde,kl,gh->jkgdlhe", wh, eye_k, eye_g).reshape(
            CMP_STRIDE * NSA_KV_ROW, NSA_KV_ROW)

    w = jnp.concatenate([big(cmp_w[:, :CMP_STRIDE]), big(cmp_w[:, CMP_STRIDE:])], axis=1)

    def pe_row(peh):
        return jnp.broadcast_to(peh.transpose(1, 0, 2)[:, :, None, :],
                                (CMP_STRIDE, 2, NSA_GROUPS, NSA_DH)).reshape(1, -1)

    pe = jnp.concatenate([pe_row(cmp_pe[:, :CMP_STRIDE]), pe_row(cmp_pe[:, CMP_STRIDE:]),
                          jnp.zeros((6, CMP_STRIDE * NSA_KV_ROW), F32)], axis=0)
    return w.astype(BF16), pe


def _compress_core(x, w_ref, pe_ref):
    n = x.shape[0]
    w = w_ref[...]
    p = _dot(x, w)
    hi, mid, lo = _split3(pe_ref[...])
    pb = _dot(hi, w) + _dot(mid, w) + _dot(lo, w)
    bias = pb[0:1, :NSA_KV_ROW] + pb[1:2, NSA_KV_ROW:]
    return p[:, :NSA_KV_ROW] + pltpu.roll(p[:, NSA_KV_ROW:], n - 1, 0) + bias


def _compress_prompt_kernel(x_ref, w_ref, pe_ref, o_ref):
    c = _compress_core(x_ref[0].astype(BF16), w_ref, pe_ref)
    for j in range(2 * NSA_GROUPS):
        o_ref[0, j] = c[:, j * NSA_DH:(j + 1) * NSA_DH].astype(BF16)


def _compress_prompt(kvc, b, t, w, pe):
    n = t // CMP_STRIDE
    x = kvc.reshape(b, n, CMP_STRIDE * NSA_KV_ROW)
    return pl.pallas_call(
        _compress_prompt_kernel, grid=(b,),
        in_specs=[pl.BlockSpec((1, n, CMP_STRIDE * NSA_KV_ROW), lambda i: (i, 0, 0)),
                  pl.BlockSpec(w.shape, lambda i: (0, 0)), pl.BlockSpec(pe.shape, lambda i: (0, 0))],
        out_specs=pl.BlockSpec((1, 4, n, NSA_DH), lambda i: (i, 0, 0, 0)),
        out_shape=_sds((b, 4, n, NSA_DH), BF16),
        compiler_params=_params(("arbitrary",), 48), name="nsa_compress_prompt")(x, w, pe)


def _overlap_matrix(n_cmp, n_sel):
    i = jnp.arange(n_cmp)[:, None]
    j = jnp.arange(n_sel)[None, :]
    per = SEL_BLOCK // CMP_STRIDE
    return ((i // per == j).astype(F32) + ((i + 1) // per == j).astype(F32)).astype(BF16)


def _topk_mask(score, k):
    n = score.shape[-1]
    lane = lax.broadcasted_iota(I32, score.shape, score.ndim - 1).astype(F32)
    sel = jnp.zeros_like(score)
    for _ in range(k):
        m = jnp.max(score, axis=-1, keepdims=True)
        first = jnp.min(jnp.where(score == m, lane, float(n)), axis=-1, keepdims=True)
        pick = lane == first
        sel = jnp.where(pick, 1.0, sel)
        score = jnp.where(pick, -jnp.inf, score)
    return sel


def _masked_softmax(s, mask):
    s = jnp.where(mask, s, NEG)
    m = jnp.max(s, axis=-1, keepdims=True)
    p = jnp.where(mask, jnp.exp(s - m), 0.0)
    return p / jnp.maximum(jnp.sum(p, axis=-1, keepdims=True), TINY)


def _cmp_branch(q, ck, cv, qpos, ovl, topk):
    tq = qpos.shape[0]
    n = ck.shape[0]
    n_sel = ovl.shape[1]
    cmp_end = CMP_STRIDE * lax.broadcasted_iota(I32, (tq, n), 1) + (CMP_BLOCK - 1)
    cmask = (cmp_end <= qpos)[None]
    s = _dot_t(q, ck).reshape(NSA_REP, tq, n)
    p = _masked_softmax(s, cmask)
    o = _dot(p.reshape(NSA_REP * tq, n).astype(BF16), cv)
    imp = jnp.sum(p, axis=0)
    hi, mid, lo = _split3(imp)
    sel_imp = _dot(hi, ovl) + _dot(mid, ovl) + _dot(lo, ovl)
    blk = lax.broadcasted_iota(I32, (tq, n_sel), 1)
    qblk = qpos // SEL_BLOCK
    forced = (blk == 0) | (blk == qblk) | (blk == qblk - 1)
    score = jnp.where(forced, sel_imp + FORCE_BONUS, jnp.where(blk <= qblk, sel_imp, NEG))
    return o, _topk_mask(score, topk)


def _nsa_cmp_prompt_kernel(q_ref, c_ref, ovl_ref, o_ref, sel_ref, *, tq):
    qpos = pl.program_id(1) * tq + lax.broadcasted_iota(I32, (tq, 1), 0)
    n_sel = ovl_ref.shape[1]
    for g in range(NSA_GROUPS):
        q = q_ref[g, :, 0].reshape(NSA_REP * tq, NSA_DH)
        o, sel = _cmp_branch(q, c_ref[0, g], c_ref[0, NSA_GROUPS + g], qpos, ovl_ref[...],
                             min(SEL_TOPK, n_sel))
        for r in range(NSA_REP):
            hd = g * NSA_REP + r
            o_ref[:, hd * NSA_DH:(hd + 1) * NSA_DH] = o[r * tq:(r + 1) * tq]
        sel_ref[:, g * n_sel:(g + 1) * n_sel] = sel.astype(BF16)


def _nsa_cmp_prompt(q, cmp, b, t, tq=256):
    n = t // CMP_STRIDE
    n_sel = t // SEL_BLOCK
    tq = min(tq, t)
    nq = t // tq
    ovl = _overlap_matrix(n, n_sel)
    q5 = q.reshape(NSA_GROUPS, NSA_REP, b, t, NSA_DH)
    return pl.pallas_call(
        functools.partial(_nsa_cmp_prompt_kernel, tq=tq), grid=(b, nq),
        in_specs=[pl.BlockSpec((NSA_GROUPS, NSA_REP, 1, tq, NSA_DH), lambda i, j: (0, 0, i, j, 0)),
                  pl.BlockSpec((1, 4, n, NSA_DH), lambda i, j: (i, 0, 0, 0)),
                  pl.BlockSpec(ovl.shape, lambda i, j: (0, 0))],
        out_specs=[pl.BlockSpec((tq, NSA_HEADS * NSA_DH), lambda i, j, nq=nq: (i * nq + j, 0)),
                   pl.BlockSpec((tq, NSA_GROUPS * n_sel), lambda i, j, nq=nq: (i * nq + j, 0))],
        out_shape=[_sds((b * t, NSA_HEADS * NSA_DH)), _sds((b * t, NSA_GROUPS * n_sel), BF16)],
        compiler_params=_params(("arbitrary", "arbitrary"), 48), name="nsa_cmp_prompt")(q5, cmp, ovl)


def _flash_update(q, k, v, mask, m_ref, l_ref, acc_ref):
    tq, tk = mask.shape
    rep = q.shape[0] // tq
    s = _dot_t(q, k).reshape(rep, tq, tk)
    s = jnp.where(mask[None], s, NEG)
    m_prev = m_ref[...]
    m_new = jnp.maximum(m_prev, jnp.max(s, axis=-1, keepdims=True))
    p = jnp.where(mask[None], jnp.exp(s - m_new), 0.0)
    alpha = jnp.exp(m_prev - m_new)
    l_ref[...] = alpha * l_ref[...] + jnp.sum(p, axis=-1, keepdims=True)
    pv = _dot(p.reshape(rep * tq, tk).astype(BF16), v)
    acc_ref[...] = alpha.reshape(rep * tq, 1) * acc_ref[...] + pv
    m_ref[...] = m_new


def _nsa_sw_prompt_kernel(q_ref, ks_ref, vs_ref, kw_ref, vw_ref, sel_ref, ocmp_ref, gate_ref,
                          o_ref, m_ref, l_ref, acc_ref, *, tq, tk):
    qi = pl.program_id(1)
    ki = pl.program_id(2)
    q0 = qi * tq
    last = (q0 + tq - 1) // tk
    first_w = jnp.maximum(q0 - (WINDOW - 1), 0) // tk
    n_sel = sel_ref.shape[1] // NSA_GROUPS
    per = tk // SEL_BLOCK

    @pl.when(ki == 0)
    def _():
        m_ref[...] = jnp.full(m_ref.shape, NEG, F32)
        l_ref[...] = jnp.zeros(l_ref.shape, F32)
        acc_ref[...] = jnp.zeros(acc_ref.shape, F32)

    qpos = q0 + lax.broadcasted_iota(I32, (tq, tk), 0)
    kpos = ki * tk + lax.broadcasted_iota(I32, (tq, tk), 1)
    causal = kpos <= qpos

    @pl.when(ki <= last)
    def _():
        expand = (lax.broadcasted_iota(I32, (n_sel, tk), 0)
                  == ki * per + lax.broadcasted_iota(I32, (n_sel, tk), 1) // SEL_BLOCK)
        expand = jnp.where(expand, 1.0, 0.0).astype(BF16)
        for g in range(NSA_GROUPS):
            q = q_ref[g, :, 0].reshape(NSA_REP * tq, NSA_DH)
            chosen = _dot(sel_ref[:, g * n_sel:(g + 1) * n_sel], expand)
            mask = jnp.where(causal, chosen, 0.0) > 0.5
            _flash_update(q, ks_ref[g, 0], vs_ref[g, 0], mask,
                          m_ref.at[g], l_ref.at[g], acc_ref.at[g])

    @pl.when((ki >= first_w) & (ki <= last))
    def _():
        mask = jnp.where(causal, qpos - kpos, WINDOW) < WINDOW
        for g in range(NSA_GROUPS):
            q = q_ref[g, :, 0].reshape(NSA_REP * tq, NSA_DH)
            _flash_update(q, kw_ref[g, 0], vw_ref[g, 0], mask,
                          m_ref.at[NSA_GROUPS + g], l_ref.at[NSA_GROUPS + g],
                          acc_ref.at[NSA_GROUPS + g])

    @pl.when(ki == last)
    def _():
        gates = gate_ref[...]
        ocmp = ocmp_ref[...]
        for g in range(NSA_GROUPS):
            o_s = acc_ref[g] / jnp.maximum(l_ref[g].reshape(NSA_REP * tq, 1), TINY)
            o_w = acc_ref[NSA_GROUPS + g] / jnp.maximum(
                l_ref[NSA_GROUPS + g].reshape(NSA_REP * tq, 1), TINY)
            for r in range(NSA_REP):
                hd = g * NSA_REP + r
                rows = slice(r * tq, (r + 1) * tq)
                cols = slice(hd * NSA_DH, (hd + 1) * NSA_DH)
                o_ref[:, cols] = (gates[:, 3 * hd:3 * hd + 1] * ocmp[:, cols]
                                  + gates[:, 3 * hd + 1:3 * hd + 2] * o_s[rows]
                                  + gates[:, 3 * hd + 2:3 * hd + 3] * o_w[rows])


def _nsa_sw_prompt(q, kvb, sel, o_cmp, gates, b, t, tq=256, tk=512):
    tq = min(tq, t)
    tk = min(tk, t)
    nq, nk = t // tq, t // tk
    q5 = q.reshape(NSA_GROUPS, NSA_REP, b, t, NSA_DH)
    kv4 = kvb.reshape(12, b, t, NSA_DH)

    def last_of(j):
        return (j * tq + tq - 1) // tk

    def slc_idx(blk):
        return lambda i, j, k: (blk, i, jnp.minimum(k, last_of(j)), 0)

    def win_idx(blk):
        def f(i, j, k):
            first = jnp.maximum(j * tq - (WINDOW - 1), 0) // tk
            return (blk, i, jnp.clip(k, first, last_of(j)), 0)
        return f

    kv_blk = (NSA_GROUPS, 1, tk, NSA_DH)
    row = lambda i, j, k: (i * nq + j, 0)
    return pl.pallas_call(
        functools.partial(_nsa_sw_prompt_kernel, tq=tq, tk=tk), grid=(b, nq, nk),
        in_specs=[pl.BlockSpec((NSA_GROUPS, NSA_REP, 1, tq, NSA_DH), lambda i, j, k: (0, 0, i, j, 0)),
                  pl.BlockSpec(kv_blk, slc_idx(2)), pl.BlockSpec(kv_blk, slc_idx(3)),
                  pl.BlockSpec(kv_blk, win_idx(4)), pl.BlockSpec(kv_blk, win_idx(5)),
                  pl.BlockSpec((tq, sel.shape[1]), row),
                  pl.BlockSpec((tq, o_cmp.shape[1]), row),
                  pl.BlockSpec((tq, gates.shape[1]), row)],
        out_specs=pl.BlockSpec((tq, NSA_HEADS * NSA_DH), row),
        out_shape=_sds((b * t, NSA_HEADS * NSA_DH)),
        scratch_shapes=[pltpu.VMEM((2 * NSA_GROUPS, NSA_REP, tq, 1), F32),
                        pltpu.VMEM((2 * NSA_GROUPS, NSA_REP, tq, 1), F32),
                        pltpu.VMEM((2 * NSA_GROUPS, NSA_REP * tq, NSA_DH), F32)],
        compiler_params=_params(("arbitrary", "arbitrary", "arbitrary"), 48),
        name="nsa_slc_win_prompt")(q5, kv4, kv4, kv4, kv4, sel, o_cmp, gates)


def _conv_prompt_kernel(u_ref, gb_ref, w_ref, o_ref, carry_ref):
    tm = u_ref.shape[0]

    @pl.when(pl.program_id(1) == 0)
    def _():
        carry_ref[...] = jnp.zeros(carry_ref.shape, F32)

    u = u_ref[...]
    prev = carry_ref[...]
    row = lax.broadcasted_iota(I32, u.shape, 0)
    u1 = jnp.where(row == 0, prev[7:8], pltpu.roll(u, 1, 0))
    u2 = jnp.where(row == 0, prev[6:7], jnp.where(row == 1, prev[7:8], pltpu.roll(u, 2, 0)))
    w = w_ref[...]
    o_ref[...] = gb_ref[...] * (w[0:1] * u2 + w[1:2] * u1 + w[2:3] * u)
    carry_ref[...] = u[tm - 8:]


def _conv_prompt(u, gb, conv_w, b, t, tm=1024):
    tm = min(tm, t)
    nt = t // tm
    c = u.shape[1]
    row = lambda i, j: (i * nt + j, 0)
    return pl.pallas_call(
        _conv_prompt_kernel, grid=(b, nt),
        in_specs=[pl.BlockSpec((tm, c), row), pl.BlockSpec((tm, c), row),
                  pl.BlockSpec(conv_w.shape, lambda i, j: (0, 0))],
        out_specs=pl.BlockSpec((tm, c), row), out_shape=_sds(u.shape),
        scratch_shapes=[pltpu.VMEM((8, c), F32)],
        compiler_params=_params(("arbitrary", "arbitrary"), 32), name="conv_prompt")(u, gb, conv_w)


def _matmul_res_body(n_a, rows, consts, outs):
    acc = rows[n_a][...]
    for k in range(n_a):
        acc = acc + _dot(rows[k][...].astype(BF16), consts[k][...])
    outs[0][...] = acc


def _matmul_res(acts, weights, res, name):
    r = res.shape[0]
    tm = _row_tile(r, 512)
    ws = [w.astype(BF16) for w in weights]
    return _rowwise(functools.partial(_matmul_res_body, len(acts)), list(acts) + [res], ws,
                    [_sds(res.shape)], tm, name)[0]


def _even_layer_prompt(x, b, t, g_mix, w_in, w_out, cmp_w, cmp_pe, conv_w):
    q, kvc, kvs, kvw, kvb, gates, gb, u = _even_proj(x, g_mix, w_in)
    cw, cpe = _compress_weights(cmp_w, cmp_pe)
    cmp = _compress_prompt(kvc, b, t, cw, cpe)
    o_cmp, sel = _nsa_cmp_prompt(q, cmp, b, t)
    o_nsa = _nsa_sw_prompt(q, kvb, sel, o_cmp, gates, b, t)
    o_conv = _conv_prompt(u, gb, conv_w, b, t)
    qc = NSA_HEADS * NSA_DH
    y = _matmul_res([o_nsa, o_conv], [w_out[:qc], w_out[qc:]], x, "even_out_proj")
    wb = min(WINDOW, t)
    caches = (kvc.reshape(b, t, 2, NSA_GROUPS, NSA_DH), kvs.reshape(b, t, 2, NSA_GROUPS, NSA_DH),
              kvw.reshape(b, t, 2, NSA_GROUPS, NSA_DH)[:, t - wb:],
              u.reshape(b, t, CONV_CH)[:, t - (CONV_W - 1):])
    return y, caches


def _page_copies(pool_ref, layer, pt_ref, batch, n_pages, buf_ref, slot, sem_ref, rows):
    return [pltpu.make_async_copy(pool_ref.at[layer, pt_ref[batch * n_pages + p]],
                                  buf_ref.at[slot, pl.ds(p * rows, rows)], sem_ref.at[slot])
            for p in range(n_pages)]


def _paged_fetch(pool_ref, layer, pt_ref, n_pages, buf_ref, sem_ref, rows):
    b = pl.program_id(0)
    slot = b % 2

    @pl.when(b == 0)
    def _():
        for c in _page_copies(pool_ref, layer, pt_ref, b, n_pages, buf_ref, slot, sem_ref, rows):
            c.start()

    @pl.when(b + 1 < pl.num_programs(0))
    def _():
        for c in _page_copies(pool_ref, layer, pt_ref, b + 1, n_pages, buf_ref, 1 - slot, sem_ref, rows):
            c.start()

    for c in _page_copies(pool_ref, layer, pt_ref, b, n_pages, buf_ref, slot, sem_ref, rows):
        c.wait()
    return slot


def _online(state, s, mask, v):
    m_prev, l_prev, acc = state
    if mask is not None:
        s = jnp.where(mask, s, NEG)
    m_new = jnp.maximum(m_prev, jnp.max(s, axis=-1, keepdims=True))
    p = jnp.exp(s - m_new)
    if mask is not None:
        p = jnp.where(mask, p, 0.0)
    alpha = jnp.exp(m_prev - m_new)
    return (m_new, alpha * l_prev + jnp.sum(p, axis=-1, keepdims=True),
            alpha * acc + _dot(p.astype(BF16), v))


def _online_init(m, w):
    return (jnp.full((m, 1), NEG, F32), jnp.zeros((m, 1), F32), jnp.zeros((m, w), F32))


def _group_query_rows(q_ref, s):
    blocks = []
    for g in range(NSA_GROUPS):
        q = q_ref[g, :, 0].astype(F32).reshape(NSA_REP * s, NSA_DH)
        z = jnp.zeros_like(q)
        blocks.append(jnp.concatenate([q if j == g else z for j in range(2 * NSA_GROUPS)], axis=1))
    return jnp.concatenate(blocks, axis=0).astype(BF16)


def _group_rows(x):
    return jnp.concatenate([jnp.tile(xg, (NSA_REP, 1)) for xg in x], axis=0)


def _nsa_cmp_sample_kernel(pt_ref, pool_ref, q_ref, w_ref, pe_ref, ovl_ref, o_ref, sel_ref,
                           buf_ref, sem_ref, *, layer, n_pages, s, topk):
    slot = _paged_fetch(pool_ref, layer, pt_ref, n_pages, buf_ref, sem_ref, PAGE_CHUNKS)
    c = _compress_core(buf_ref[slot].astype(BF16), w_ref, pe_ref)
    n_sel = ovl_ref.shape[1]
    qpos = n_pages * PAGE_ROWS + lax.broadcasted_iota(I32, (s, 1), 0)
    for g in range(NSA_GROUPS):
        q = q_ref[g, :, 0].astype(F32).reshape(NSA_REP * s, NSA_DH).astype(BF16)
        ck = c[:, g * NSA_DH:(g + 1) * NSA_DH].astype(BF16)
        cv = c[:, (NSA_GROUPS + g) * NSA_DH:(NSA_GROUPS + g + 1) * NSA_DH].astype(BF16)
        o, sel = _cmp_branch(q, ck, cv, qpos, ovl_ref[...], topk)
        for r in range(NSA_REP):
            hd = g * NSA_REP + r
            o_ref[0, :, hd * NSA_DH:(hd + 1) * NSA_DH] = o[r * s:(r + 1) * s]
        sel_ref[0, :, g * n_sel:(g + 1) * n_sel] = sel.astype(BF16)


PAGE_ROWS = 128
PAGE_CHUNKS = PAGE_ROWS // CMP_STRIDE


def _nsa_cmp_sample(q, pool, layer, page_table, w, pe, sb, s):
    n_pages = page_table.shape[1]
    past = n_pages * PAGE_ROWS
    assert past % SEL_BLOCK == 0 and s <= CMP_STRIDE and pool.shape[2] == PAGE_ROWS
    n = past // CMP_STRIDE
    n_sel = past // SEL_BLOCK
    ovl = _overlap_matrix(n, n_sel)
    pool4 = pool.reshape(pool.shape[0], pool.shape[1], PAGE_CHUNKS, CMP_STRIDE * NSA_KV_ROW)
    q5 = q.reshape(NSA_GROUPS, NSA_REP, sb, s, NSA_DH)
    kern = functools.partial(_nsa_cmp_sample_kernel, layer=layer, n_pages=n_pages, s=s,
                             topk=SEL_TOPK - 1)
    grid_spec = pltpu.PrefetchScalarGridSpec(
        num_scalar_prefetch=1, grid=(sb,),
        in_specs=[pl.BlockSpec(memory_space=pl.ANY),
                  pl.BlockSpec((NSA_GROUPS, NSA_REP, 1, s, NSA_DH), lambda i, pt: (0, 0, i, 0, 0)),
                  pl.BlockSpec(w.shape, lambda i, pt: (0, 0)),
                  pl.BlockSpec(pe.shape, lambda i, pt: (0, 0)),
                  pl.BlockSpec(ovl.shape, lambda i, pt: (0, 0))],
        out_specs=[pl.BlockSpec((1, s, NSA_HEADS * NSA_DH), lambda i, pt: (i, 0, 0)),
                   pl.BlockSpec((1, s, NSA_GROUPS * n_sel), lambda i, pt: (i, 0, 0))],
        scratch_shapes=[pltpu.VMEM((2, n, CMP_STRIDE * NSA_KV_ROW), F32),
                        pltpu.SemaphoreType.DMA((2,))])
    return pl.pallas_call(
        kern, grid_spec=grid_spec,
        out_shape=[_sds((sb, s, NSA_HEADS * NSA_DH)), _sds((sb, s, NSA_GROUPS * n_sel), BF16)],
        compiler_params=_params(("arbitrary",), 56), name="nsa_cmp_sample")(
            page_table.reshape(-1), pool4, q5, w, pe, ovl)


def _nsa_sw_sample_kernel(pt_ref, pool_ref, q_ref, knew_ref, wnew_ref, win_ref, sel_ref, ocmp_ref,
                          gate_ref, o_ref, buf_ref, sem_ref, *, layer, n_pages, s, chunk):
    slot = _paged_fetch(pool_ref, layer, pt_ref, n_pages, buf_ref, sem_ref, PAGE_ROWS)
    past = n_pages * PAGE_ROWS
    n_sel = sel_ref.shape[2] // NSA_GROUPS
    m = NSA_HEADS * s
    qm = _group_query_rows(q_ref, s)
    sq = lax.broadcasted_iota(I32, (s, s), 0)
    sk = lax.broadcasted_iota(I32, (s, s), 1)
    new_mask = _group_rows([sk <= sq] * NSA_GROUPS)

    state = _online_init(m, NSA_KV_ROW)
    for c0 in range(0, past, chunk):
        kv = buf_ref[slot, c0:c0 + chunk].astype(BF16)
        expand = (lax.broadcasted_iota(I32, (n_sel, chunk), 0)
                  == (c0 + lax.broadcasted_iota(I32, (n_sel, chunk), 1)) // SEL_BLOCK)
        expand = jnp.where(expand, 1.0, 0.0).astype(BF16)
        chosen = [_dot(sel_ref[0, :, g * n_sel:(g + 1) * n_sel], expand) > 0.5
                  for g in range(NSA_GROUPS)]
        state = _online(state, _dot_t(qm, kv), _group_rows(chosen), kv)
    knew = knew_ref[0].astype(BF16)
    state = _online(state, _dot_t(qm, knew), new_mask, knew)
    o_slc = state[2] / jnp.maximum(state[1], TINY)

    wb = win_ref.shape[1]
    win = win_ref[0].astype(BF16)
    qrow = lax.broadcasted_iota(I32, (s, wb), 0)
    dist = qrow + wb - lax.broadcasted_iota(I32, (s, wb), 1)
    old_mask = _group_rows([(dist >= 0) & (dist < WINDOW)] * NSA_GROUPS)
    state = _online(_online_init(m, NSA_KV_ROW), _dot_t(qm, win), old_mask, win)
    wnew = wnew_ref[0].astype(BF16)
    state = _online(state, _dot_t(qm, wnew), new_mask, wnew)
    o_win = state[2] / jnp.maximum(state[1], TINY)

    gates = gate_ref[0]
    ocmp = ocmp_ref[0]
    for g in range(NSA_GROUPS):
        vcols = slice((NSA_GROUPS + g) * NSA_DH, (NSA_GROUPS + g + 1) * NSA_DH)
        for r in range(NSA_REP):
            hd = g * NSA_REP + r
            rows = slice(hd * s, (hd + 1) * s)
            cols = slice(hd * NSA_DH, (hd + 1) * NSA_DH)
            o_ref[0, :, cols] = (gates[:, 3 * hd:3 * hd + 1] * ocmp[:, cols]
                                 + gates[:, 3 * hd + 1:3 * hd + 2] * o_slc[rows, vcols]
                                 + gates[:, 3 * hd + 2:3 * hd + 3] * o_win[rows, vcols])


def _nsa_sw_sample(q, pool, layer, page_table, kvs_new, kvw_new, win_buf, sel, o_cmp, gates, sb, s):
    n_pages = page_table.shape[1]
    past = n_pages * PAGE_ROWS
    wb = win_buf.shape[1]
    assert wb == WINDOW and s <= SEL_BLOCK and past % SEL_BLOCK == 0
    pool4 = pool.reshape(pool.shape[0], pool.shape[1], PAGE_ROWS, NSA_KV_ROW)
    q5 = q.reshape(NSA_GROUPS, NSA_REP, sb, s, NSA_DH)
    per_b = lambda a: a.reshape(sb, s, a.shape[-1])
    blk = lambda a: pl.BlockSpec((1,) + a.shape[1:], lambda i, pt: (i, 0, 0))
    rows = [per_b(kvs_new), per_b(kvw_new), win_buf.reshape(sb, wb, NSA_KV_ROW), sel, o_cmp,
            per_b(gates)]
    kern = functools.partial(_nsa_sw_sample_kernel, layer=layer, n_pages=n_pages, s=s,
                             chunk=min(2048, past))
    grid_spec = pltpu.PrefetchScalarGridSpec(
        num_scalar_prefetch=1, grid=(sb,),
        in_specs=[pl.BlockSpec(memory_space=pl.ANY),
                  pl.BlockSpec((NSA_GROUPS, NSA_REP, 1, s, NSA_DH), lambda i, pt: (0, 0, i, 0, 0))]
        + [blk(a) for a in rows],
        out_specs=pl.BlockSpec((1, s, NSA_HEADS * NSA_DH), lambda i, pt: (i, 0, 0)),
        scratch_shapes=[pltpu.VMEM((2, past, NSA_KV_ROW), F32), pltpu.SemaphoreType.DMA((2,))])
    return pl.pallas_call(
        kern, grid_spec=grid_spec, out_shape=_sds((sb, s, NSA_HEADS * NSA_DH)),
        compiler_params=_params(("arbitrary",), 56), name="nsa_slc_win_sample")(
            page_table.reshape(-1), pool4, q5, *rows)


def _conv_sample_kernel(u_ref, gb_ref, cb_ref, w_ref, o_ref):
    u = u_ref[...]
    cb = cb_ref[...]
    row = lax.broadcasted_iota(I32, u.shape, 1)
    u1 = jnp.where(row == 0, cb[:, 1:2], pltpu.roll(u, 1, 1))
    u2 = jnp.where(row == 0, cb[:, 0:1], jnp.where(row == 1, cb[:, 1:2], pltpu.roll(u, 2, 1)))
    w = w_ref[...]
    o_ref[...] = gb_ref[...] * (w[0:1][None] * u2 + w[1:2][None] * u1 + w[2:3][None] * u)


def _conv_sample(u, gb, conv_buf, conv_w, sb, s):
    assert s >= CONV_W - 1
    c = u.shape[1]
    full = lambda shape: pl.BlockSpec(shape, lambda i, n=len(shape): (0,) * n)
    return pl.pallas_call(
        _conv_sample_kernel, grid=(1,),
        in_specs=[full((sb, s, c)), full((sb, s, c)), full(conv_buf.shape), full(conv_w.shape)],
        out_specs=full((sb, s, c)), out_shape=_sds((sb, s, c)),
        compiler_params=_params(("arbitrary",), 32), name="conv_sample")(
            u.reshape(sb, s, c), gb.reshape(sb, s, c), conv_buf, conv_w).reshape(sb * s, c)


def _even_layer_sample(x, sb, s, layer, pool_cmp, pool_slc, win_buf, conv_buf, page_table,
                       g_mix, w_in, w_out, cmp_w, cmp_pe, conv_w):
    q, kvc, kvs, kvw, _, gates, gb, u = _even_proj(x, g_mix, w_in)
    cw, cpe = _compress_weights(cmp_w, cmp_pe)
    o_cmp, sel = _nsa_cmp_sample(q, pool_cmp, layer, page_table, cw, cpe, sb, s)
    o_nsa = _nsa_sw_sample(q, pool_slc, layer, page_table, kvs, kvw, win_buf, sel, o_cmp, gates,
                           sb, s)
    o_conv = _conv_sample(u, gb, conv_buf, conv_w, sb, s)
    qc = NSA_HEADS * NSA_DH
    y = _matmul_res([o_nsa.reshape(sb * s, qc), o_conv], [w_out[:qc], w_out[qc:]], x,
                    "even_out_proj")
    five = lambda a: a.reshape(sb, s, 2, NSA_GROUPS, NSA_DH)
    win_all = jnp.concatenate([win_buf, five(kvw)], axis=1)[:, s:]
    conv_all = jnp.concatenate([conv_buf, u.reshape(sb, s, CONV_CH)], axis=1)[:, s:]
    return y, (five(kvc), five(kvs), win_all, conv_all)


def _rot_half_cols(w):
    half = w.shape[-1] // 2
    return jnp.concatenate([-w[..., half:], w[..., :half]], axis=-1)


def _rope_tables(pos):
    half = ROPE // 2
    inv = ROPE_THETA ** (-jnp.arange(half, dtype=F32) / half)
    ang = pos.astype(F32)[:, None] * inv[None, :]
    cos, sin = jnp.cos(ang), jnp.sin(ang)
    cos32 = jnp.concatenate([cos, cos], axis=1)
    sin32 = jnp.concatenate([sin, sin], axis=1)
    n = pos.shape[0]
    pad = MLA_QK - NOPE - ROPE
    cosq = jnp.concatenate([jnp.ones((n, NOPE), F32), cos32, jnp.zeros((n, pad), F32)], axis=1)
    sinq = jnp.concatenate([jnp.zeros((n, NOPE), F32), sin32, jnp.zeros((n, pad), F32)], axis=1)
    return cos32, sin32, cosq, sinq


def _mla_weights(w_in, w_uq, w_uk):
    wcq = w_in[:, :Q_LORA]
    wckv = w_in[:, Q_LORA:Q_LORA + KV_LORA]
    wkpe = w_in[:, Q_LORA + KV_LORA:]
    uq = w_uq.reshape(Q_LORA, MLA_HEADS, NOPE + ROPE)
    pad = jnp.zeros((Q_LORA, MLA_HEADS, MLA_QK - NOPE - ROPE), F32)
    wq = jnp.concatenate([uq, pad], axis=-1).reshape(Q_LORA, MLA_HEADS * MLA_QK)
    wq_rot = jnp.concatenate([jnp.zeros((Q_LORA, MLA_HEADS, NOPE), F32),
                              _rot_half_cols(uq[..., NOPE:]), pad], axis=-1).reshape(wq.shape)
    kpad = jnp.zeros((KV_LORA, MLA_HEADS, MLA_QK - NOPE), F32)
    wk_lat = jnp.concatenate([w_uk, kpad], axis=-1).reshape(KV_LORA, MLA_HEADS * MLA_QK)
    eye = jnp.broadcast_to(jnp.eye(ROPE, dtype=F32)[:, None, :], (ROPE, MLA_HEADS, ROPE))
    wk_pe = jnp.concatenate([jnp.zeros((ROPE, MLA_HEADS, NOPE), F32), eye,
                             jnp.zeros((ROPE, MLA_HEADS, MLA_QK - NOPE - ROPE), F32)],
                            axis=-1).reshape(ROPE, MLA_HEADS * MLA_QK)
    bf = lambda a: a.astype(BF16)
    return dict(wcq=bf(wcq), wckv=bf(wckv), wkpe=bf(wkpe), wkpe_rot=bf(_rot_half_cols(wkpe)),
                wq=bf(wq), wq_rot=bf(wq_rot), wk_lat=bf(wk_lat), wk_pe=bf(wk_pe))


def _mla_latent(x_ref, cos_ref, sin_ref, g_ref, qn_ref, kvn_ref, wcq_ref, wckv_ref, wkpe_ref,
                wkrot_ref, lat_ref):
    h = _rms(x_ref[...], g_ref[...]).astype(BF16)
    cq = _rms(_dot(h, wcq_ref[...]), qn_ref[...]).astype(BF16)
    ckv = _rms(_dot(h, wckv_ref[...]), kvn_ref[...])
    kpe = _dot(h, wkpe_ref[...]) * cos_ref[...] + _dot(h, wkrot_ref[...]) * sin_ref[...]
    lat_ref[:, :KV_LORA] = ckv
    lat_ref[:, KV_LORA:] = kpe
    return cq, ckv.astype(BF16), kpe.astype(BF16)


def _mla_queries(cq, wq_ref, wqrot_ref, cosq, sinq, head):
    cols = slice(head * MLA_QK, (head + 1) * MLA_QK)
    q = _dot(cq, wq_ref[:, cols]) * cosq + _dot(cq, wqrot_ref[:, cols]) * sinq
    return q * ((NOPE + ROPE) ** -0.5)


def _mla_proj_prompt_body(rows, consts, outs):
    x_ref, cos_ref, sin_ref, cosq_ref, sinq_ref = rows
    (g_ref, qn_ref, kvn_ref, wcq_ref, wckv_ref, wkpe_ref, wkrot_ref, wq_ref, wqrot_ref,
     wklat_ref, wkpe2_ref, wuv_ref) = consts
    lat_ref, q_ref, k_ref, v_ref = outs
    cq, ckv, kpe = _mla_latent(x_ref, cos_ref, sin_ref, g_ref, qn_ref, kvn_ref, wcq_ref,
                               wckv_ref, wkpe_ref, wkrot_ref, lat_ref)
    cosq, sinq = cosq_ref[...], sinq_ref[...]
    for hd in range(MLA_HEADS):
        cols = slice(hd * MLA_QK, (hd + 1) * MLA_QK)
        q_ref[:, cols] = _mla_queries(cq, wq_ref, wqrot_ref, cosq, sinq, hd).astype(BF16)
    k_ref[...] = (_dot(ckv, wklat_ref[...]) + _dot(kpe, wkpe2_ref[...])).astype(BF16)
    v_ref[...] = _dot(ckv, wuv_ref[...]).astype(BF16)


def _mla_proj_sample_body(rows, consts, outs):
    x_ref, cos_ref, sin_ref, cosq_ref, sinq_ref = rows
    (g_ref, qn_ref, kvn_ref, wcq_ref, wckv_ref, wkpe_ref, wkrot_ref, wq_ref, wqrot_ref,
     wukt_ref) = consts
    lat_ref, q_ref = outs
    cq, _, _ = _mla_latent(x_ref, cos_ref, sin_ref, g_ref, qn_ref, kvn_ref, wcq_ref, wckv_ref,
                           wkpe_ref, wkrot_ref, lat_ref)
    cosq, sinq = cosq_ref[...], sinq_ref[...]
    for hd in range(MLA_HEADS):
        q = _mla_queries(cq, wq_ref, wqrot_ref, cosq, sinq, hd)
        q_ref[hd, :, :KV_LORA] = _dot(q[:, :NOPE].astype(BF16), wukt_ref[hd])
        q_ref[hd, :, KV_LORA:] = q[:, NOPE:NOPE + ROPE]


def _mla_proj(x, pos, g_mix, q_norm, kv_norm, mw, extra, body, outs, name):
    r, d = x.shape
    tm = _row_tile(pos.shape[0], 512)
    cos32, sin32, cosq, sinq = _rope_tables(pos)
    n_per = pos.shape[0] // tm
    periods = {k: n_per for k in (1, 2, 3, 4)}
    consts = [g_mix.reshape(1, d), q_norm.reshape(1, -1), kv_norm.reshape(1, -1), mw["wcq"],
              mw["wckv"], mw["wkpe"], mw["wkpe_rot"], mw["wq"], mw["wq_rot"]] + extra
    return _rowwise(body, [x, cos32, sin32, cosq, sinq], consts, outs, tm, name, periods=periods)


def _mla_flash_kernel(qi_ref, ki_ref, q_ref, k_ref, v_ref, o_ref, m_ref, l_ref, acc_ref, *, tq, tk):
    n = pl.program_id(2)
    qi = qi_ref[n]
    ki = ki_ref[n]
    last = (qi * tq + tq - 1) // tk

    @pl.when(ki == 0)
    def _():
        m_ref[...] = jnp.full(m_ref.shape, NEG, F32)
        l_ref[...] = jnp.zeros(l_ref.shape, F32)
        acc_ref[...] = jnp.zeros(acc_ref.shape, F32)

    def step(masked):
        if masked:
            mask = (ki * tk + lax.broadcasted_iota(I32, (tq, tk), 1)
                    <= qi * tq + lax.broadcasted_iota(I32, (tq, tk), 0))
        for j in range(2):
            s = _dot_t(q_ref[:, j * MLA_QK:(j + 1) * MLA_QK], k_ref[:, j * MLA_QK:(j + 1) * MLA_QK])
            if masked:
                s = jnp.where(mask, s, NEG)
            m_prev = m_ref[j]
            m_new = jnp.maximum(m_prev, jnp.max(s, axis=-1, keepdims=True))
            p = jnp.exp(s - m_new)
            if masked:
                p = jnp.where(mask, p, 0.0)
            alpha = jnp.exp(m_prev - m_new)
            l_ref[j] = alpha * l_ref[j] + jnp.sum(p, axis=-1, keepdims=True)
            acc_ref[j] = alpha * acc_ref[j] + _dot(p.astype(BF16), v_ref[:, j * V_DIM:(j + 1) * V_DIM])
            m_ref[j] = m_new

    needs_mask = ki * tk + tk - 1 > qi * tq

    @pl.when(needs_mask)
    def _():
        step(True)

    @pl.when(jnp.logical_not(needs_mask))
    def _():
        step(False)

    @pl.when(ki == last)
    def _():
        for j in range(2):
            o_ref[:, j * V_DIM:(j + 1) * V_DIM] = acc_ref[j] / jnp.maximum(l_ref[j], TINY)


def _causal_steps(nq, tq, tk):
    qi, ki = [], []
    for j in range(nq):
        for k in range((j * tq + tq - 1) // tk + 1):
            qi.append(j)
            ki.append(k)
    return jnp.asarray(qi, I32), jnp.asarray(ki, I32)


def _mla_flash(q, k, v, b, t, tq=512, tk=512):
    tq, tk = min(tq, t), min(tk, t)
    nq, nk = t // tq, t // tk
    qi, ki = _causal_steps(nq, tq, tk)
    pairs = MLA_HEADS // 2
    grid_spec = pltpu.PrefetchScalarGridSpec(
        num_scalar_prefetch=2, grid=(b, pairs, qi.shape[0]),
        in_specs=[pl.BlockSpec((tq, 2 * MLA_QK), lambda i, h, n, qt, kt: (i * nq + qt[n], h)),
                  pl.BlockSpec((tk, 2 * MLA_QK), lambda i, h, n, qt, kt: (i * nk + kt[n], h)),
                  pl.BlockSpec((tk, 2 * V_DIM), lambda i, h, n, qt, kt: (i * nk + kt[n], h))],
        out_specs=pl.BlockSpec((tq, 2 * V_DIM), lambda i, h, n, qt, kt: (i * nq + qt[n], h)),
        scratch_shapes=[pltpu.VMEM((2, tq, 1), F32), pltpu.VMEM((2, tq, 1), F32),
                        pltpu.VMEM((2, tq, V_DIM), F32)])
    return pl.pallas_call(
        functools.partial(_mla_flash_kernel, tq=tq, tk=tk), grid_spec=grid_spec,
        out_shape=_sds((b * t, MLA_HEADS * V_DIM)),
        compiler_params=_params(("arbitrary", "arbitrary", "arbitrary"), 48),
        name="mla_flash_prompt")(qi, ki, q, k, v)


def _odd_layer_prompt(x, b, t, g_mix, w_in, q_norm, w_uq, kv_norm, w_uk, w_uv, w_out):
    mw = _mla_weights(w_in, w_uq, w_uk)
    r = x.shape[0]
    outs = [_sds((r, LAT)), _sds((r, MLA_HEADS * MLA_QK), BF16), _sds((r, MLA_HEADS * MLA_QK), BF16),
            _sds((r, MLA_HEADS * V_DIM), BF16)]
    extra = [mw["wk_lat"], mw["wk_pe"], w_uv.reshape(KV_LORA, MLA_HEADS * V_DIM).astype(BF16)]
    lat, q, k, v = _mla_proj(x, jnp.arange(t, dtype=I32), g_mix, q_norm, kv_norm, mw, extra,
                             _mla_proj_prompt_body, outs, "mla_proj_prompt")
    o = _mla_flash(q, k, v, b, t)
    y = _matmul_res([o], [w_out], x, "mla_out_proj")
    return y, lat.reshape(b, t, LAT)


def _mla_sample_kernel(pt_ref, pool_ref, q_ref, new_ref, o_ref, buf_ref, sem_ref, *,
                       layer, n_pages, s, chunk):
    slot = _paged_fetch(pool_ref, layer, pt_ref, n_pages, buf_ref, sem_ref, PAGE_ROWS)
    past = n_pages * PAGE_ROWS
    m = MLA_HEADS * s
    q = q_ref[:, 0].reshape(m, LAT).astype(BF16)
    state = _online_init(m, KV_LORA)
    for c0 in range(0, past, chunk):
        kv = buf_ref[slot, c0:c0 + chunk].astype(BF16)
        state = _online(state, _dot_t(q, kv), None, kv[:, :KV_LORA])
    new = new_ref[0].astype(BF16)
    causal = (lax.broadcasted_iota(I32, (s, s), 1) <= lax.broadcasted_iota(I32, (s, s), 0))
    state = _online(state, _dot_t(q, new), jnp.tile(causal, (MLA_HEADS, 1)), new[:, :KV_LORA])
    o = state[2] / jnp.maximum(state[1], TINY)
    o_ref[:, 0] = o.reshape(MLA_HEADS, s, KV_LORA)


def _mla_sample_attend(q, pool, layer, page_table, lat_new, sb, s):
    n_pages = page_table.shape[1]
    past = n_pages * PAGE_ROWS
    q4 = q.reshape(MLA_HEADS, sb, s, LAT)
    kern = functools.partial(_mla_sample_kernel, layer=layer, n_pages=n_pages, s=s,
                             chunk=min(2048, past))
    grid_spec = pltpu.PrefetchScalarGridSpec(
        num_scalar_prefetch=1, grid=(sb,),
        in_specs=[pl.BlockSpec(memory_space=pl.ANY),
                  pl.BlockSpec((MLA_HEADS, 1, s, LAT), lambda i, pt: (0, i, 0, 0)),
                  pl.BlockSpec((1, s, LAT), lambda i, pt: (i, 0, 0))],
        out_specs=pl.BlockSpec((MLA_HEADS, 1, s, KV_LORA), lambda i, pt: (0, i, 0, 0)),
        scratch_shapes=[pltpu.VMEM((2, past, LAT), F32), pltpu.SemaphoreType.DMA((2,))])
    o = pl.pallas_call(
        kern, grid_spec=grid_spec, out_shape=_sds((MLA_HEADS, sb, s, KV_LORA)),
        compiler_params=_params(("arbitrary",), 56), name="mla_paged_sample")(
            page_table.reshape(-1), pool, q4, lat_new.reshape(sb, s, LAT))
    return o.reshape(MLA_HEADS, sb * s, KV_LORA)


def _mla_out_sample_body(rows, consts, outs):
    o_ref, res_ref = rows
    wuv_ref, wout_ref = consts
    acc = res_ref[...]
    for hd in range(MLA_HEADS):
        o_h = _dot(o_ref[hd].astype(BF16), wuv_ref[hd])
        acc = acc + _dot(o_h.astype(BF16), wout_ref[hd])
    outs[0][...] = acc


def _odd_layer_sample(x, sb, s, layer, pool, page_table, g_mix, w_in, q_norm, w_uq, kv_norm, w_uk,
                      w_uv, w_out):
    mw = _mla_weights(w_in, w_uq, w_uk)
    r = x.shape[0]
    past = page_table.shape[1] * PAGE_ROWS
    pos = past + jnp.arange(_row_tile(r, 512), dtype=I32) % s
    outs = [_sds((r, LAT)), _sds((MLA_HEADS, r, LAT))]
    wukt = w_uk.transpose(1, 2, 0).astype(BF16)
    lat, q = _mla_proj(x, pos, g_mix, q_norm, kv_norm, mw, [wukt],
                       _mla_proj_sample_body, outs, "mla_proj_sample")
    o_lat = _mla_sample_attend(q, pool, layer, page_table, lat, sb, s)
    wuv = w_uv.transpose(1, 0, 2).astype(BF16)
    wout = w_out.reshape(MLA_HEADS, V_DIM, -1).astype(BF16)
    y = _rowwise(_mla_out_sample_body, [o_lat, x], [wuv, wout], [_sds(x.shape)],
                 _row_tile(r, 512), "mla_out_sample")[0]
    return y, lat.reshape(sb, s, LAT)


def _plain_matmul_body(rows, consts, outs):
    outs[0][...] = _dot(rows[0][...].astype(BF16), consts[0][...])


def _mem_q_body(rows, consts, outs):
    g_ref, w_ref = consts
    h = _rms(rows[0][...], g_ref[...]).astype(BF16)
    dh = w_ref.shape[1] // MEM_HEADS
    outs[0][...] = (_dot(h, w_ref[...]) * (dh ** -0.5)).astype(BF16)


def _mem_attn_kernel(q_ref, kv_ref, o_ref):
    d = q_ref.shape[2]
    dh = d // MEM_HEADS
    for hd in range(MEM_HEADS):
        k = kv_ref[0, :, hd * dh:(hd + 1) * dh].astype(BF16)
        v = kv_ref[0, :, d + hd * dh:d + (hd + 1) * dh].astype(BF16)
        s = _dot_t(q_ref[0, :, hd * dh:(hd + 1) * dh], k)
        p = jnp.exp(s - jnp.max(s, axis=-1, keepdims=True))
        p = p / jnp.sum(p, axis=-1, keepdims=True)
        o_ref[0, :, hd * dh:(hd + 1) * dh] = _dot(p.astype(BF16), v).astype(BF16)


def _mem_layer(y, nb, kv, g_mem, w_q, w_o):
    r, d = y.shape
    rows = r // nb
    q = _rowwise(_mem_q_body, [y], [g_mem.reshape(1, d), w_q.astype(BF16)], [_sds((r, d), BF16)],
                 _row_tile(r, 512), "mem_q_proj")[0]
    tq = _row_tile(rows, 512)
    nt = rows // tq
    mem = kv.shape[1]
    o = pl.pallas_call(
        _mem_attn_kernel, grid=(nb, nt),
        in_specs=[pl.BlockSpec((1, tq, d), lambda i, j: (i, j, 0)),
                  pl.BlockSpec((1, mem, 2 * d), lambda i, j: (i, 0, 0))],
        out_specs=pl.BlockSpec((1, tq, d), lambda i, j: (i, j, 0)),
        out_shape=_sds((nb, rows, d), BF16),
        compiler_params=_params(("arbitrary", "arbitrary"), 48), name="mem_attention")(
            q.reshape(nb, rows, d), kv)
    return _matmul_res([o.reshape(r, d)], [w_o], y, "mem_out_proj")


def _mem_project(mem, w_kv):
    nb, m, d = mem.shape
    kv = _rowwise(_plain_matmul_body, [mem.reshape(nb * m, d)], [w_kv.astype(BF16)],
                  [_sds((nb * m, w_kv.shape[1]))], _row_tile(nb * m, 512), "mem_kv_proj")[0]
    return kv.reshape(nb, m, w_kv.shape[1])


def _route(x, wr_ref, br_ref):
    xh, xm, xl = _split3(x)
    w0, w1, w2 = wr_ref[0], wr_ref[1], wr_ref[2]
    logit = (_dot(xh, w0) + (_dot(xh, w1) + _dot(xm, w0))
             + (_dot(xh, w2) + _dot(xl, w0) + _dot(xm, w1))) + br_ref[...]
    lane = lax.broadcasted_iota(I32, logit.shape, 1).astype(F32)
    big = float(ROUTE_LANES)
    is_g = lane < N_GROUPS
    gl = jnp.where(is_g, logit, -jnp.inf)
    gmax = jnp.max(gl, axis=-1, keepdims=True)
    gidx = jnp.min(jnp.where(gl == gmax, lane, big), axis=-1, keepdims=True)
    g_gate = 1.0 / jnp.sum(jnp.where(is_g, jnp.exp(gl - gmax), 0.0), axis=-1, keepdims=True)
    lo = N_GROUPS + gidx * EXP_PER_GROUP
    in_grp = (lane >= lo) & (lane < lo + EXP_PER_GROUP)
    el = jnp.where(in_grp, logit, -jnp.inf)
    ee = jnp.where(in_grp, jnp.exp(el - jnp.max(el, axis=-1, keepdims=True)), 0.0)
    prob = ee / jnp.sum(ee, axis=-1, keepdims=True)
    cand = jnp.where(in_grp, prob, -1.0)
    p1 = jnp.max(cand, axis=-1, keepdims=True)
    i1 = jnp.min(jnp.where(cand == p1, lane, big), axis=-1, keepdims=True)
    cand = jnp.where(lane == i1, -1.0, cand)
    p2 = jnp.max(cand, axis=-1, keepdims=True)
    i2 = jnp.min(jnp.where(cand == p2, lane, big), axis=-1, keepdims=True)
    tot = p1 + p2
    return (jnp.where(lane == i1, p1 / tot * g_gate, 0.0)
            + jnp.where(lane == i2, p2 / tot * g_gate, 0.0))


def _moe_kernel(y_ref, g_ref, wr_ref, br_ref, w13_ref, w2_ref, gf_ref, o_ref, h_ref, comb_ref,
                acc_ref, *, final_norm):
    e = pl.program_id(1)

    @pl.when(e == 0)
    def _():
        x = _rms(y_ref[...], g_ref[...])
        h_ref[...] = x.astype(BF16)
        comb_ref[...] = _route(x, wr_ref, br_ref)
        acc_ref[...] = jnp.zeros(acc_ref.shape, F32)

    hc = _dot(h_ref[...], w13_ref[0])
    ff = hc.shape[1] // 2
    a, bb = hc[:, :ff], hc[:, ff:]
    comb = comb_ref[...]
    lane = lax.broadcasted_iota(I32, comb.shape, 1)
    ce = jnp.sum(jnp.where(lane == e + N_GROUPS, comb, 0.0), axis=-1, keepdims=True)
    act = a * (1.0 / (1.0 + jnp.exp(-a))) * bb * ce
    acc_ref[...] += _dot(act.astype(BF16), w2_ref[0])

    @pl.when(e == pl.num_programs(1) - 1)
    def _():
        out = y_ref[...] + acc_ref[...]
        if final_norm:
            out = _rms(out, gf_ref[...])
        o_ref[...] = out


def _moe_layer(y, g_ffn, w_group, b_group, w_expert, b_expert, w13, w2, g_final=None):
    r, d = y.shape
    tm = _row_tile(r, 1024)
    n_route = N_GROUPS + N_EXPERTS
    wr = jnp.concatenate([w_group, w_expert, jnp.zeros((d, ROUTE_LANES - n_route), F32)], axis=1)
    wr = jnp.stack(_split3(wr))
    br = jnp.concatenate([b_group, b_expert, jnp.zeros((ROUTE_LANES - n_route,), F32)]).reshape(1, -1)
    gf = (g_ffn if g_final is None else g_final).reshape(1, d)
    ff2 = w13.shape[2]
    return pl.pallas_call(
        functools.partial(_moe_kernel, final_norm=g_final is not None), grid=(r // tm, N_EXPERTS),
        in_specs=[pl.BlockSpec((tm, d), lambda i, e: (i, 0)),
                  pl.BlockSpec((1, d), lambda i, e: (0, 0)),
                  pl.BlockSpec(wr.shape, lambda i, e: (0, 0, 0)),
                  pl.BlockSpec(br.shape, lambda i, e: (0, 0)),
                  pl.BlockSpec((1, d, ff2), lambda i, e: (e, 0, 0)),
                  pl.BlockSpec((1, ff2 // 2, d), lambda i, e: (e, 0, 0)),
                  pl.BlockSpec((1, d), lambda i, e: (0, 0))],
        out_specs=pl.BlockSpec((tm, d), lambda i, e: (i, 0)),
        out_shape=_sds((r, d)),
        scratch_shapes=[pltpu.VMEM((tm, d), BF16), pltpu.VMEM((tm, ROUTE_LANES), F32),
                        pltpu.VMEM((tm, d), F32)],
        compiler_params=_params(("arbitrary", "arbitrary"), 56), name="hier_moe")(
            y, g_ffn.reshape(1, d), wr, br, w13.astype(BF16), w2.astype(BF16), gf)


def kernel(x_prompt, x_sample, mem_prompt, cache_nsa_cmp_kv, cache_nsa_slc_kv, state_nsa_win_kv, state_conv, cache_mla_latent, cache_mem_kv, page_table, norm_mix, norm_mem, norm_ffn, norm_final, e_w_in, e_w_out, e_cmp_w, e_cmp_pe, e_conv_w, o_w_in, o_q_norm, o_w_uq, o_kv_norm, o_w_uk, o_w_uv, o_w_out, mem_w_q, mem_w_kv, mem_w_o, moe_w_group, moe_b_group, moe_w_expert, moe_b_expert, moe_w13, moe_w2):
    b, t, d = x_prompt.shape
    sb, s, _ = x_sample.shape
    depth = norm_mix.shape[0]
    yp = x_prompt.reshape(b * t, d)
    ys = x_sample.reshape(sb * s, d)
    p_even, s_even, p_mla, s_mla, p_mem = [], [], [], [], []
    for l in range(depth):
        if l % 2 == 0:
            e = l // 2
            w = (norm_mix[l], e_w_in[e], e_w_out[e], e_cmp_w[e], e_cmp_pe[e], e_conv_w[e])
            yp, cp = _even_layer_prompt(yp, b, t, *w)
            ys, cs = _even_layer_sample(ys, sb, s, e, cache_nsa_cmp_kv, cache_nsa_slc_kv,
                                        state_nsa_win_kv[e], state_conv[e], page_table, *w)
            p_even.append(cp)
            s_even.append(cs)
        else:
            o = l // 2
            w = (norm_mix[l], o_w_in[o], o_q_norm[o], o_w_uq[o], o_kv_norm[o], o_w_uk[o], o_w_uv[o],
                 o_w_out[o])
            yp, lp = _odd_layer_prompt(yp, b, t, *w)
            ys, ls = _odd_layer_sample(ys, sb, s, o, cache_mla_latent, page_table, *w)
            p_mla.append(lp)
            s_mla.append(ls)
        kv_p = _mem_project(mem_prompt, mem_w_kv[l])
        p_mem.append(kv_p.reshape(b, kv_p.shape[1], 2, MEM_HEADS, d // MEM_HEADS))
        yp = _mem_layer(yp, b, kv_p, norm_mem[l], mem_w_q[l], mem_w_o[l])
        ys = _mem_layer(ys, sb, cache_mem_kv[l].reshape(sb, -1, 2 * d), norm_mem[l], mem_w_q[l],
                        mem_w_o[l])
        g_final = norm_final if l == depth - 1 else None
        moe_w = (norm_ffn[l], moe_w_group[l], moe_b_group[l], moe_w_expert[l], moe_b_expert[l],
                 moe_w13[l], moe_w2[l])
        yp = _moe_layer(yp, *moe_w, g_final=g_final)
        ys = _moe_layer(ys, *moe_w, g_final=g_final)
    stack = lambda caches, k: jnp.stack([c[k] for c in caches])
    return (yp.reshape(b, t, d), ys.reshape(sb, s, d),
            stack(p_even, 0), stack(p_even, 1), stack(p_even, 2), stack(p_even, 3),
            jnp.stack(p_mla), jnp.stack(p_mem),
            stack(s_even, 0), stack(s_even, 1), stack(s_even, 2), stack(s_even, 3),
            jnp.stack(s_mla))
```

```python
import functools
import math

import jax
import jax.numpy as jnp
from jax import lax
from jax.experimental import pallas as pl
from jax.experimental.pallas import tpu as pltpu

F32 = jnp.float32
BF16 = jnp.bfloat16
I32 = jnp.int32

EPS = 1e-6
LOG2E = math.log2(math.e)
NEG = -1e30
TINY = 1e-30
FORCE_BONUS = 1e4

NSA_HEADS = 8
NSA_GROUPS = 2
NSA_REP = NSA_HEADS // NSA_GROUPS
NSA_DH = 64
NSA_KV_ROW = 2 * NSA_GROUPS * NSA_DH
CMP_STRIDE = 16
CMP_BLOCK = 32
SEL_BLOCK = 64
SEL_TOPK = 16
WINDOW = 512
CONV_CH = 512
CONV_W = 3
MLA_HEADS = 16
Q_LORA = 384
KV_LORA = 256
NOPE = 64
ROPE = 32
V_DIM = 64
ROPE_THETA = 10000.0
LAT = KV_LORA + ROPE
MLA_QK = 128
MEM_HEADS = 4
N_GROUPS = 4
EXP_PER_GROUP = 4
N_EXPERTS = 16
EXPERT_FF = 256
ROUTE_LANES = 128

V7X_VMEM_BYTES = 64 * 2**20
V7X_LANES = 128


def _params(sem, vmem_mb):
    assert vmem_mb * 2**20 < V7X_VMEM_BYTES
    return pltpu.CompilerParams(dimension_semantics=sem, vmem_limit_bytes=vmem_mb * 2**20)


def _dot(a, b):
    return jnp.dot(a, b, preferred_element_type=F32)


def _dot_t(a, b):
    return lax.dot_general(a, b, (((1,), (1,)), ((), ())), preferred_element_type=F32)


def _tdot(a, b):
    return lax.dot_general(a, b, (((0,), (0,)), ((), ())), preferred_element_type=F32)


def _split3(x):
    hi = x.astype(BF16)
    r1 = x - hi.astype(F32)
    mid = r1.astype(BF16)
    lo = (r1 - mid.astype(F32)).astype(BF16)
    return hi, mid, lo


def _rms(x, g):
    return x * lax.rsqrt(jnp.mean(x * x, axis=-1, keepdims=True) + EPS) * g


def _row_spec(shape, tm, period=None):
    lead = len(shape) - 2
    blk = tuple(shape[:lead]) + (tm, shape[-1])
    if period is None:
        return pl.BlockSpec(blk, lambda i, lead=lead: (0,) * lead + (i, 0))
    return pl.BlockSpec(blk, lambda i, lead=lead, p=period: (0,) * lead + (i % p, 0))


def _const_spec(shape):
    return pl.BlockSpec(tuple(shape), lambda i, n=len(shape): (0,) * n)


def _rowwise(body, rows, consts, outs, tm, name, vmem_mb=48, periods=None):
    n_rows = rows[0].shape[-2]
    assert n_rows % tm == 0
    periods = periods or {}
    in_specs = [_row_spec(a.shape, tm, periods.get(k)) for k, a in enumerate(rows)]
    in_specs += [_const_spec(c.shape) for c in consts]
    out_specs = [_row_spec(o.shape, tm) for o in outs]
    nr, nc = len(rows), len(consts)

    def kern(*refs):
        body(refs[:nr], refs[nr:nr + nc], refs[nr + nc:])

    return pl.pallas_call(
        kern, grid=(n_rows // tm,), in_specs=in_specs, out_specs=out_specs, out_shape=outs,
        compiler_params=_params(("arbitrary",), vmem_mb), name=name)(*rows, *consts)


def _sds(shape, dtype=F32):
    return jax.ShapeDtypeStruct(tuple(shape), dtype)


def _row_tile(n_rows, want):
    tm = min(want, n_rows)
    assert n_rows % tm == 0
    return tm


def _even_proj_body(rows, consts, outs):
    (x_ref,) = rows
    g_ref, wq_ref, wkv_ref, wg_ref, wc_ref = consts
    q_ref, kvc_ref, kvs_ref, kvw_ref, kvb_ref, gate_ref, gb_ref, u_ref = outs
    h = _rms(x_ref[...], g_ref[...]).astype(BF16)
    q = _dot(h, wq_ref[...]) * (NSA_DH ** -0.5 * LOG2E)
    for hd in range(NSA_HEADS):
        q_ref[hd // NSA_REP, hd % NSA_REP] = q[:, hd * NSA_DH:(hd + 1) * NSA_DH].astype(BF16)
    kv = _dot(h, wkv_ref[...])
    kvc_ref[...] = kv[:, :NSA_KV_ROW]
    kvs_ref[...] = kv[:, NSA_KV_ROW:2 * NSA_KV_ROW]
    kvw_ref[...] = kv[:, 2 * NSA_KV_ROW:]
    for j in range(3 * 2 * NSA_GROUPS):
        kvb_ref[j] = kv[:, j * NSA_DH:(j + 1) * NSA_DH].astype(BF16)
    gate_ref[...] = 1.0 / (1.0 + jnp.exp(-_dot(h, wg_ref[...])))
    c = _dot(h, wc_ref[...])
    gb_ref[...] = c[:, :CONV_CH]
    u_ref[...] = c[:, CONV_CH:2 * CONV_CH] * c[:, 2 * CONV_CH:]


def _even_proj(x, g, w_in):
    r, d = x.shape
    qc = NSA_HEADS * NSA_DH
    kc = 3 * NSA_KV_ROW
    gc = NSA_HEADS * 3
    wq = w_in[:, :qc].astype(BF16)
    wkv = w_in[:, qc:qc + kc].astype(BF16)
    wg = w_in[:, qc + kc:qc + kc + gc].astype(BF16)
    wc = w_in[:, qc + kc + gc:].astype(BF16)
    tm = _row_tile(r, 512)
    outs = [
        _sds((NSA_GROUPS, NSA_REP, r, NSA_DH), BF16),
        _sds((r, NSA_KV_ROW)), _sds((r, NSA_KV_ROW)), _sds((r, NSA_KV_ROW)),
        _sds((12, r, NSA_DH), BF16),
        _sds((r, gc)), _sds((r, CONV_CH)), _sds((r, CONV_CH)),
    ]
    return _rowwise(_even_proj_body, [x], [g.reshape(1, d), wq, wkv, wg, wc], outs, tm, "even_proj")


def _compress_weights(cmp_w, cmp_pe):
    eye_g = jnp.eye(NSA_GROUPS, dtype=F32)
    eye_k = jnp.eye(2, dtype=F32)

    def big(wh):
        return jnp.einsum("kjde,kl,gh->jkgdlhe", wh, eye_k, eye_g).reshape(
            CMP_STRIDE * NSA_KV_ROW, NSA_KV_ROW)

    w = jnp.concatenate([big(cmp_w[:, :CMP_STRIDE]), big(cmp_w[:, CMP_STRIDE:])], axis=1)

    def pe_row(peh):
        return jnp.broadcast_to(peh.transpose(1, 0, 2)[:, :, None, :],
                                (CMP_STRIDE, 2, NSA_GROUPS, NSA_DH)).reshape(1, -1)

    pe = jnp.concatenate([pe_row(cmp_pe[:, :CMP_STRIDE]), pe_row(cmp_pe[:, CMP_STRIDE:]),
                          jnp.zeros((6, CMP_STRIDE * NSA_KV_ROW), F32)], axis=0)
    return w.astype(BF16), pe


def _compress_core(x, w_ref, pe_ref):
    return _compress_finish(_dot(x, w_ref[...]), w_ref, pe_ref)


def _compress_finish(p, w_ref, pe_ref):
    n = p.shape[0]
    w = w_ref[...]
    hi, mid, lo = _split3(pe_ref[...])
    pb = _dot(hi, w) + _dot(mid, w) + _dot(lo, w)
    bias = pb[0:1, :NSA_KV_ROW] + pb[1:2, NSA_KV_ROW:]
    return p[:, :NSA_KV_ROW] + pltpu.roll(p[:, NSA_KV_ROW:], n - 1, 0) + bias


def _compress_prompt_kernel(x_ref, w_ref, pe_ref, o_ref):
    c = _compress_core(x_ref[0].astype(BF16), w_ref, pe_ref)
    for j in range(2 * NSA_GROUPS):
        o_ref[0, j] = c[:, j * NSA_DH:(j + 1) * NSA_DH].astype(BF16)


def _compress_prompt(kvc, b, t, w, pe):
    n = t // CMP_STRIDE
    x = kvc.reshape(b, n, CMP_STRIDE * NSA_KV_ROW)
    return pl.pallas_call(
        _compress_prompt_kernel, grid=(b,),
        in_specs=[pl.BlockSpec((1, n, CMP_STRIDE * NSA_KV_ROW), lambda i: (i, 0, 0)),
                  pl.BlockSpec(w.shape, lambda i: (0, 0)), pl.BlockSpec(pe.shape, lambda i: (0, 0))],
        out_specs=pl.BlockSpec((1, 4, n, NSA_DH), lambda i: (i, 0, 0, 0)),
        out_shape=_sds((b, 4, n, NSA_DH), BF16),
        compiler_params=_params(("arbitrary",), 48), name="nsa_compress_prompt")(x, w, pe)


def _overlap_matrix(n_cmp, n_sel):
    i = jnp.arange(n_cmp)[:, None]
    j = jnp.arange(n_sel)[None, :]
    per = SEL_BLOCK // CMP_STRIDE
    return ((i // per == j).astype(F32) + ((i + 1) // per == j).astype(F32)).astype(BF16)


def _topk_mask(score, k):
    n = score.shape[-1]
    lane = lax.broadcasted_iota(I32, score.shape, score.ndim - 1).astype(F32)
    sel = jnp.zeros_like(score)
    for _ in range(k):
        m = jnp.max(score, axis=-1, keepdims=True)
        first = jnp.min(jnp.where(score == m, lane, float(n)), axis=-1, keepdims=True)
        pick = lane == first
        sel = jnp.where(pick, 1.0, sel)
        score = jnp.where(pick, -jnp.inf, score)
    return sel


def _masked_softmax(s, mask):
    s = jnp.where(mask, s, NEG)
    m = jnp.max(s, axis=-1, keepdims=True)
    p = jnp.where(mask, jnp.exp2(s - m), 0.0)
    return p / jnp.maximum(jnp.sum(p, axis=-1, keepdims=True), TINY)


def _cmp_branch(q, ck, cv, qpos, ovl, topk):
    tq = qpos.shape[0]
    n = ck.shape[0]
    n_sel = ovl.shape[1]
    cmp_end = CMP_STRIDE * lax.broadcasted_iota(I32, (tq, n), 1) + (CMP_BLOCK - 1)
    cmask = (cmp_end <= qpos)[None]
    s = _dot_t(q, ck).reshape(NSA_REP, tq, n)
    p = _masked_softmax(s, cmask)
    o = _dot(p.reshape(NSA_REP * tq, n).astype(BF16), cv)
    imp = jnp.sum(p, axis=0)
    hi, mid, lo = _split3(imp)
    sel_imp = _dot(hi, ovl) + _dot(mid, ovl) + _dot(lo, ovl)
    blk = lax.broadcasted_iota(I32, (tq, n_sel), 1)
    qblk = qpos // SEL_BLOCK
    forced = (blk == 0) | (blk == qblk) | (blk == qblk - 1)
    score = jnp.where(forced, sel_imp + FORCE_BONUS, jnp.where(blk <= qblk, sel_imp, NEG))
    return o, _topk_mask(score, topk)


def _nsa_cmp_prompt_kernel(q_ref, c_ref, ovl_ref, o_ref, sel_ref, *, tq):
    qpos = pl.program_id(1) * tq + lax.broadcasted_iota(I32, (tq, 1), 0)
    n_sel = ovl_ref.shape[1]
    for g in range(NSA_GROUPS):
        q = q_ref[g, :, 0].reshape(NSA_REP * tq, NSA_DH)
        o, sel = _cmp_branch(q, c_ref[0, g], c_ref[0, NSA_GROUPS + g], qpos, ovl_ref[...],
                             min(SEL_TOPK, n_sel))
        for r in range(NSA_REP):
            hd = g * NSA_REP + r
            o_ref[:, hd * NSA_DH:(hd + 1) * NSA_DH] = o[r * tq:(r + 1) * tq]
        sel_ref[:, g * n_sel:(g + 1) * n_sel] = sel.astype(BF16)


def _nsa_cmp_prompt(q, cmp, b, t, tq=256):
    n = t // CMP_STRIDE
    n_sel = t // SEL_BLOCK
    tq = min(tq, t)
    nq = t // tq
    ovl = _overlap_matrix(n, n_sel)
    q5 = q.reshape(NSA_GROUPS, NSA_REP, b, t, NSA_DH)
    return pl.pallas_call(
        functools.partial(_nsa_cmp_prompt_kernel, tq=tq), grid=(b, nq),
        in_specs=[pl.BlockSpec((NSA_GROUPS, NSA_REP, 1, tq, NSA_DH), lambda i, j: (0, 0, i, j, 0)),
                  pl.BlockSpec((1, 4, n, NSA_DH), lambda i, j: (i, 0, 0, 0)),
                  pl.BlockSpec(ovl.shape, lambda i, j: (0, 0))],
        out_specs=[pl.BlockSpec((tq, NSA_HEADS * NSA_DH), lambda i, j, nq=nq: (i * nq + j, 0)),
                   pl.BlockSpec((tq, NSA_GROUPS * n_sel), lambda i, j, nq=nq: (i * nq + j, 0))],
        out_shape=[_sds((b * t, NSA_HEADS * NSA_DH)), _sds((b * t, NSA_GROUPS * n_sel), BF16)],
        compiler_params=_params(("arbitrary", "arbitrary"), 48), name="nsa_cmp_prompt")(q5, cmp, ovl)


def _flash_update(q, k, v, mask, m_ref, l_ref, acc_ref):
    tq, tk = mask.shape
    rep = q.shape[0] // tq
    s = _dot_t(q, k).reshape(rep, tq, tk)
    s = jnp.where(mask[None], s, NEG)
    m_prev = m_ref[...]
    m_new = jnp.maximum(m_prev, jnp.max(s, axis=-1, keepdims=True))
    p = jnp.where(mask[None], jnp.exp(s - m_new), 0.0)
    alpha = jnp.exp(m_prev - m_new)
    l_ref[...] = alpha * l_ref[...] + jnp.sum(p, axis=-1, keepdims=True)
    pv = _dot(p.reshape(rep * tq, tk).astype(BF16), v)
    acc_ref[...] = alpha.reshape(rep * tq, 1) * acc_ref[...] + pv
    m_ref[...] = m_new


def _nsa_sw_prompt_kernel(q_ref, ks_ref, vs_ref, kw_ref, vw_ref, sel_ref, ocmp_ref, gate_ref,
                          o_ref, m_ref, l_ref, acc_ref, *, tq, tk):
    qi = pl.program_id(1)
    ki = pl.program_id(2)
    q0 = qi * tq
    last = (q0 + tq - 1) // tk
    first_w = jnp.maximum(q0 - (WINDOW - 1), 0) // tk
    n_sel = sel_ref.shape[1] // NSA_GROUPS
    per = tk // SEL_BLOCK

    @pl.when(ki == 0)
    def _():
        m_ref[...] = jnp.full(m_ref.shape, NEG, F32)
        l_ref[...] = jnp.zeros(l_ref.shape, F32)
        acc_ref[...] = jnp.zeros(acc_ref.shape, F32)

    qpos = q0 + lax.broadcasted_iota(I32, (tq, tk), 0)
    kpos = ki * tk + lax.broadcasted_iota(I32, (tq, tk), 1)
    causal = kpos <= qpos

    @pl.when(ki <= last)
    def _():
        expand = (lax.broadcasted_iota(I32, (n_sel, tk), 0)
                  == ki * per + lax.broadcasted_iota(I32, (n_sel, tk), 1) // SEL_BLOCK)
        expand = jnp.where(expand, 1.0, 0.0).astype(BF16)
        for g in range(NSA_GROUPS):
            q = q_ref[g, :, 0].reshape(NSA_REP * tq, NSA_DH)
            chosen = _dot(sel_ref[:, g * n_sel:(g + 1) * n_sel], expand)
            mask = jnp.where(causal, chosen, 0.0) > 0.5
            _flash_update(q, ks_ref[g, 0], vs_ref[g, 0], mask,
                          m_ref.at[g], l_ref.at[g], acc_ref.at[g])

    @pl.when((ki >= first_w) & (ki <= last))
    def _():
        mask = jnp.where(causal, qpos - kpos, WINDOW) < WINDOW
        for g in range(NSA_GROUPS):
            q = q_ref[g, :, 0].reshape(NSA_REP * tq, NSA_DH)
            _flash_update(q, kw_ref[g, 0], vw_ref[g, 0], mask,
                          m_ref.at[NSA_GROUPS + g], l_ref.at[NSA_GROUPS + g],
                          acc_ref.at[NSA_GROUPS + g])

    @pl.when(ki == last)
    def _():
        gates = gate_ref[...]
        ocmp = ocmp_ref[...]
        for g in range(NSA_GROUPS):
            o_s = acc_ref[g] / jnp.maximum(l_ref[g].reshape(NSA_REP * tq, 1), TINY)
            o_w = acc_ref[NSA_GROUPS + g] / jnp.maximum(
                l_ref[NSA_GROUPS + g].reshape(NSA_REP * tq, 1), TINY)
            for r in range(NSA_REP):
                hd = g * NSA_REP + r
                rows = slice(r * tq, (r + 1) * tq)
                cols = slice(hd * NSA_DH, (hd + 1) * NSA_DH)
                o_ref[:, cols] = (gates[:, 3 * hd:3 * hd + 1] * ocmp[:, cols]
                                  + gates[:, 3 * hd + 1:3 * hd + 2] * o_s[rows]
                                  + gates[:, 3 * hd + 2:3 * hd + 3] * o_w[rows])


def _nsa_sw_prompt(q, kvb, sel, o_cmp, gates, b, t, tq=256, tk=512):
    tq = min(tq, t)
    tk = min(tk, t)
    nq, nk = t // tq, t // tk
    q5 = q.reshape(NSA_GROUPS, NSA_REP, b, t, NSA_DH)
    kv4 = kvb.reshape(12, b, t, NSA_DH)

    def last_of(j):
        return (j * tq + tq - 1) // tk

    def slc_idx(blk):
        return lambda i, j, k: (blk, i, jnp.minimum(k, last_of(j)), 0)

    def win_idx(blk):
        def f(i, j, k):
            first = jnp.maximum(j * tq - (WINDOW - 1), 0) // tk
            return (blk, i, jnp.clip(k, first, last_of(j)), 0)
        return f

    kv_blk = (NSA_GROUPS, 1, tk, NSA_DH)
    row = lambda i, j, k: (i * nq + j, 0)
    return pl.pallas_call(
        functools.partial(_nsa_sw_prompt_kernel, tq=tq, tk=tk), grid=(b, nq, nk),
        in_specs=[pl.BlockSpec((NSA_GROUPS, NSA_REP, 1, tq, NSA_DH), lambda i, j, k: (0, 0, i, j, 0)),
                  pl.BlockSpec(kv_blk, slc_idx(2)), pl.BlockSpec(kv_blk, slc_idx(3)),
                  pl.BlockSpec(kv_blk, win_idx(4)), pl.BlockSpec(kv_blk, win_idx(5)),
                  pl.BlockSpec((tq, sel.shape[1]), row),
                  pl.BlockSpec((tq, o_cmp.shape[1]), row),
                  pl.BlockSpec((tq, gates.shape[1]), row)],
        out_specs=pl.BlockSpec((tq, NSA_HEADS * NSA_DH), row),
        out_shape=_sds((b * t, NSA_HEADS * NSA_DH)),
        scratch_shapes=[pltpu.VMEM((2 * NSA_GROUPS, NSA_REP, tq, 1), F32),
                        pltpu.VMEM((2 * NSA_GROUPS, NSA_REP, tq, 1), F32),
                        pltpu.VMEM((2 * NSA_GROUPS, NSA_REP * tq, NSA_DH), F32)],
        compiler_params=_params(("arbitrary", "arbitrary", "arbitrary"), 48),
        name="nsa_slc_win_prompt")(q5, kv4, kv4, kv4, kv4, sel, o_cmp, gates)


def _flash_update_t(q, k, v, mask_t, m_ref, l_ref, acc_ref):
    s = _dot_t(k, q)
    if mask_t is not None:
        s = jnp.where(mask_t, s, NEG)
    m_prev = m_ref[...]
    m_new = jnp.maximum(m_prev, jnp.max(s, axis=0, keepdims=True))
    p = jnp.exp2(s - m_new)
    if mask_t is not None:
        p = jnp.where(mask_t, p, 0.0)
    alpha = jnp.exp2(m_prev - m_new)
    l_ref[...] = alpha * l_ref[...] + jnp.sum(p, axis=0, keepdims=True)
    acc_ref[...] = alpha * acc_ref[...] + _tdot(v, p.astype(BF16))
    m_ref[...] = m_new


def _causal_steps(nq, tq, tk):
    qi, ki = [], []
    for j in range(nq):
        for k in range((j * tq + tq - 1) // tk + 1):
            qi.append(j)
            ki.append(k)
    return jnp.asarray(qi, I32), jnp.asarray(ki, I32)


def _nsa_swt_prompt_kernel(qi_ref, ki_ref, q_ref, ks_ref, vs_ref, kw_ref, vw_ref, sel_ref, ocmp_ref,
                           gate_ref, o_ref, m_ref, l_ref, acc_ref, *, tq, tk):
    n = pl.program_id(1)
    qi = qi_ref[n]
    ki = ki_ref[n]
    q0 = qi * tq
    last = (q0 + tq - 1) // tk
    first_w = jnp.maximum(q0 - (WINDOW - 1), 0) // tk
    n_sel = sel_ref.shape[1] // NSA_GROUPS
    n_q = NSA_REP * tq

    @pl.when(ki == 0)
    def _():
        m_ref[...] = jnp.full(m_ref.shape, NEG, F32)
        l_ref[...] = jnp.zeros(l_ref.shape, F32)
        acc_ref[...] = jnp.zeros(acc_ref.shape, F32)

    kpos = ki * tk + lax.broadcasted_iota(I32, (tk, tq), 0)
    qpos = q0 + lax.broadcasted_iota(I32, (tk, tq), 1)
    causal = kpos <= qpos
    heads = lambda mask: jnp.concatenate([mask] * NSA_REP, axis=1)

    expand = (lax.broadcasted_iota(I32, (tk, n_sel), 1)
              == (ki * tk + lax.broadcasted_iota(I32, (tk, n_sel), 0)) // SEL_BLOCK)
    expand = jnp.where(expand, 1.0, 0.0).astype(BF16)
    for g in range(NSA_GROUPS):
        q = q_ref[g, :, 0].reshape(n_q, NSA_DH)
        chosen = _dot_t(expand, sel_ref[:, g * n_sel:(g + 1) * n_sel])
        mask = jnp.where(causal, chosen, 0.0) > 0.5
        _flash_update_t(q, ks_ref[g, 0], vs_ref[g, 0], heads(mask),
                        m_ref.at[g], l_ref.at[g], acc_ref.at[g])

    @pl.when(ki >= first_w)
    def _():
        mask = heads(jnp.where(causal, qpos - kpos, WINDOW) < WINDOW)
        for g in range(NSA_GROUPS):
            q = q_ref[g, :, 0].reshape(n_q, NSA_DH)
            _flash_update_t(q, kw_ref[g, 0], vw_ref[g, 0], mask, m_ref.at[NSA_GROUPS + g],
                            l_ref.at[NSA_GROUPS + g], acc_ref.at[NSA_GROUPS + g])

    @pl.when(ki == last)
    def _():
        gates = gate_ref[...]
        ocmp = ocmp_ref[...]
        for g in range(NSA_GROUPS):
            o_s = acc_ref[g] / jnp.maximum(l_ref[g], TINY)
            o_w = acc_ref[NSA_GROUPS + g] / jnp.maximum(l_ref[NSA_GROUPS + g], TINY)
            for r in range(NSA_REP):
                hd = g * NSA_REP + r
                lanes = slice(r * tq, (r + 1) * tq)
                cols = slice(hd * NSA_DH, (hd + 1) * NSA_DH)
                o_ref[:, cols] = (gates[:, 3 * hd:3 * hd + 1] * ocmp[:, cols]
                                  + gates[:, 3 * hd + 1:3 * hd + 2] * o_s[:, lanes].T
                                  + gates[:, 3 * hd + 2:3 * hd + 3] * o_w[:, lanes].T)


def _nsa_swt_prompt(q, kvb, sel, o_cmp, gates, b, t, tq=256, tk=512):
    tq = min(tq, t)
    tk = min(tk, t)
    nq = t // tq
    qi, ki = _causal_steps(nq, tq, tk)
    q5 = q.reshape(NSA_GROUPS, NSA_REP, b, t, NSA_DH)
    kv4 = kvb.reshape(12, b, t, NSA_DH)

    def slc_idx(blk):
        return lambda i, n, qt, kt: (blk, i, kt[n], 0)

    def win_idx(blk):
        def f(i, n, qt, kt):
            first = jnp.maximum(qt[n] * tq - (WINDOW - 1), 0) // tk
            return (blk, i, jnp.maximum(kt[n], first), 0)
        return f

    kv_blk = (NSA_GROUPS, 1, tk, NSA_DH)
    row = lambda i, n, qt, kt: (i * nq + qt[n], 0)
    n_q = NSA_REP * tq
    grid_spec = pltpu.PrefetchScalarGridSpec(
        num_scalar_prefetch=2, grid=(b, qi.shape[0]),
        in_specs=[pl.BlockSpec((NSA_GROUPS, NSA_REP, 1, tq, NSA_DH),
                               lambda i, n, qt, kt: (0, 0, i, qt[n], 0)),
                  pl.BlockSpec(kv_blk, slc_idx(2)), pl.BlockSpec(kv_blk, slc_idx(3)),
                  pl.BlockSpec(kv_blk, win_idx(4)), pl.BlockSpec(kv_blk, win_idx(5)),
                  pl.BlockSpec((tq, sel.shape[1]), row),
                  pl.BlockSpec((tq, o_cmp.shape[1]), row),
                  pl.BlockSpec((tq, gates.shape[1]), row)],
        out_specs=pl.BlockSpec((tq, NSA_HEADS * NSA_DH), row),
        scratch_shapes=[pltpu.VMEM((2 * NSA_GROUPS, 1, n_q), F32),
                        pltpu.VMEM((2 * NSA_GROUPS, 1, n_q), F32),
                        pltpu.VMEM((2 * NSA_GROUPS, NSA_DH, n_q), F32)])
    return pl.pallas_call(
        functools.partial(_nsa_swt_prompt_kernel, tq=tq, tk=tk), grid_spec=grid_spec,
        out_shape=_sds((b * t, NSA_HEADS * NSA_DH)),
        compiler_params=_params(("arbitrary", "arbitrary"), 48),
        name="nsa_slc_win_prompt")(qi, ki, q5, kv4, kv4, kv4, kv4, sel, o_cmp, gates)


def _conv_prompt_kernel(u_ref, gb_ref, w_ref, o_ref, carry_ref):
    tm = u_ref.shape[0]

    @pl.when(pl.program_id(1) == 0)
    def _():
        carry_ref[...] = jnp.zeros(carry_ref.shape, F32)

    u = u_ref[...]
    prev = carry_ref[...]
    row = lax.broadcasted_iota(I32, u.shape, 0)
    u1 = jnp.where(row == 0, prev[7:8], pltpu.roll(u, 1, 0))
    u2 = jnp.where(row == 0, prev[6:7], jnp.where(row == 1, prev[7:8], pltpu.roll(u, 2, 0)))
    w = w_ref[...]
    o_ref[...] = gb_ref[...] * (w[0:1] * u2 + w[1:2] * u1 + w[2:3] * u)
    carry_ref[...] = u[tm - 8:]


def _conv_prompt(u, gb, conv_w, b, t, tm=1024):
    tm = min(tm, t)
    nt = t // tm
    c = u.shape[1]
    row = lambda i, j: (i * nt + j, 0)
    return pl.pallas_call(
        _conv_prompt_kernel, grid=(b, nt),
        in_specs=[pl.BlockSpec((tm, c), row), pl.BlockSpec((tm, c), row),
                  pl.BlockSpec(conv_w.shape, lambda i, j: (0, 0))],
        out_specs=pl.BlockSpec((tm, c), row), out_shape=_sds(u.shape),
        scratch_shapes=[pltpu.VMEM((8, c), F32)],
        compiler_params=_params(("arbitrary", "arbitrary"), 32), name="conv_prompt")(u, gb, conv_w)


def _matmul_res_body(n_a, rows, consts, outs):
    acc = rows[n_a][...]
    for k in range(n_a):
        acc = acc + _dot(rows[k][...].astype(BF16), consts[k][...])
    outs[0][...] = acc


def _matmul_res(acts, weights, res, name):
    r = res.shape[0]
    tm = _row_tile(r, 512)
    ws = [w.astype(BF16) for w in weights]
    return _rowwise(functools.partial(_matmul_res_body, len(acts)), list(acts) + [res], ws,
                    [_sds(res.shape)], tm, name)[0]


def _even_layer_prompt(x, b, t, g_mix, w_in, w_out, cmp_w, cmp_pe, conv_w):
    q, kvc, kvs, kvw, kvb, gates, gb, u = _even_proj(x, g_mix, w_in)
    cw, cpe = _compress_weights(cmp_w, cmp_pe)
    cmp = _compress_prompt(kvc, b, t, cw, cpe)
    o_cmp, sel = _nsa_cmp_prompt(q, cmp, b, t)
    o_nsa = _nsa_swt_prompt(q, kvb, sel, o_cmp, gates, b, t)
    o_conv = _conv_prompt(u, gb, conv_w, b, t)
    qc = NSA_HEADS * NSA_DH
    y = _matmul_res([o_nsa, o_conv], [w_out[:qc], w_out[qc:]], x, "even_out_proj")
    wb = min(WINDOW, t)
    caches = (kvc.reshape(b, t, 2, NSA_GROUPS, NSA_DH), kvs.reshape(b, t, 2, NSA_GROUPS, NSA_DH),
              kvw.reshape(b, t, 2, NSA_GROUPS, NSA_DH)[:, t - wb:],
              u.reshape(b, t, CONV_CH)[:, t - (CONV_W - 1):])
    return y, caches


def _channel_major(pool):
    nd = pool.ndim
    t = pool.transpose((0, 1) + tuple(range(3, nd)) + (2,))
    return t.reshape(pool.shape[0], pool.shape[1], -1, pool.shape[2])


def _page_copies(pool_ref, layer, pt_ref, batch, n_pages, buf_ref, slot, sem_ref):
    return [pltpu.make_async_copy(pool_ref.at[layer, pt_ref[batch * n_pages + p]],
                                  buf_ref.at[slot, :, pl.ds(p * PAGE_ROWS, PAGE_ROWS)],
                                  sem_ref.at[slot])
            for p in range(n_pages)]


def _paged_fetch(pool_ref, layer, pt_ref, n_pages, buf_ref, sem_ref):
    b = pl.program_id(0)
    slot = b % 2

    @pl.when(b == 0)
    def _():
        for c in _page_copies(pool_ref, layer, pt_ref, b, n_pages, buf_ref, slot, sem_ref):
            c.start()

    @pl.when(b + 1 < pl.num_programs(0))
    def _():
        for c in _page_copies(pool_ref, layer, pt_ref, b + 1, n_pages, buf_ref, 1 - slot, sem_ref):
            c.start()

    for c in _page_copies(pool_ref, layer, pt_ref, b, n_pages, buf_ref, slot, sem_ref):
        c.wait()
    return slot


def _online(state, s, mask, v, v_t=False):
    m_prev, l_prev, acc = state
    if mask is not None:
        s = jnp.where(mask, s, NEG)
    m_new = jnp.maximum(m_prev, jnp.max(s, axis=-1, keepdims=True))
    p = jnp.exp2(s - m_new)
    if mask is not None:
        p = jnp.where(mask, p, 0.0)
    alpha = jnp.exp2(m_prev - m_new)
    return (m_new, alpha * l_prev + jnp.sum(p, axis=-1, keepdims=True),
            alpha * acc + (_dot_t(p.astype(BF16), v) if v_t else _dot(p.astype(BF16), v)))


def _online_init(m, w):
    return (jnp.full((m, 1), NEG, F32), jnp.zeros((m, 1), F32), jnp.zeros((m, w), F32))


def _group_query_rows(q_ref, s):
    blocks = []
    for g in range(NSA_GROUPS):
        q = q_ref[g, :, 0].astype(F32).reshape(NSA_REP * s, NSA_DH)
        z = jnp.zeros_like(q)
        blocks.append(jnp.concatenate([q if j == g else z for j in range(2 * NSA_GROUPS)], axis=1))
    return jnp.concatenate(blocks, axis=0).astype(BF16)


def _group_rows(x):
    return jnp.concatenate([jnp.tile(xg, (NSA_REP, 1)) for xg in x], axis=0)


def _nsa_cmp_sample_kernel(pt_ref, pool_ref, q_ref, w_ref, pe_ref, ovl_ref, o_ref, sel_ref,
                           buf_ref, sem_ref, x_ref, *, layer, n_pages, s, topk):
    slot = _paged_fetch(pool_ref, layer, pt_ref, n_pages, buf_ref, sem_ref)
    past = n_pages * PAGE_ROWS
    step = min(1024, past)
    halves = x_ref.shape[0]
    lanes = x_ref.shape[2]
    for c0 in range(0, past, step):
        for hf in range(halves):
            x_ref[hf, c0:c0 + step, :] = buf_ref[slot, hf * lanes:(hf + 1) * lanes, c0:c0 + step].T
    p = None
    for j in range(CMP_STRIDE):
        for hf in range(halves):
            xj = x_ref.at[hf][pl.ds(j, past // CMP_STRIDE, stride=CMP_STRIDE), :].astype(BF16)
            w0 = j * NSA_KV_ROW + hf * lanes
            t = _dot(xj, w_ref[w0:w0 + lanes, :])
            p = t if p is None else p + t
    c = _compress_finish(p, w_ref, pe_ref)
    n_sel = ovl_ref.shape[1]
    qpos = n_pages * PAGE_ROWS + lax.broadcasted_iota(I32, (s, 1), 0)
    for g in range(NSA_GROUPS):
        q = q_ref[g, :, 0].astype(F32).reshape(NSA_REP * s, NSA_DH).astype(BF16)
        ck = c[:, g * NSA_DH:(g + 1) * NSA_DH].astype(BF16)
        cv = c[:, (NSA_GROUPS + g) * NSA_DH:(NSA_GROUPS + g + 1) * NSA_DH].astype(BF16)
        o, sel = _cmp_branch(q, ck, cv, qpos, ovl_ref[...], topk)
        for r in range(NSA_REP):
            hd = g * NSA_REP + r
            o_ref[0, :, hd * NSA_DH:(hd + 1) * NSA_DH] = o[r * s:(r + 1) * s]
        sel_ref[0, :, g * n_sel:(g + 1) * n_sel] = sel.astype(BF16)


PAGE_ROWS = 128
PAGE_CHUNKS = PAGE_ROWS // CMP_STRIDE


def _nsa_cmp_sample(q, pool, layer, page_table, w, pe, sb, s):
    n_pages = page_table.shape[1]
    past = n_pages * PAGE_ROWS
    assert past % SEL_BLOCK == 0 and s <= CMP_STRIDE and pool.shape[2] == PAGE_ROWS
    n = past // CMP_STRIDE
    n_sel = past // SEL_BLOCK
    ovl = _overlap_matrix(n, n_sel)
    pool4 = _channel_major(pool)
    q5 = q.reshape(NSA_GROUPS, NSA_REP, sb, s, NSA_DH)
    kern = functools.partial(_nsa_cmp_sample_kernel, layer=layer, n_pages=n_pages, s=s,
                             topk=SEL_TOPK - 1)
    grid_spec = pltpu.PrefetchScalarGridSpec(
        num_scalar_prefetch=1, grid=(sb,),
        in_specs=[pl.BlockSpec(memory_space=pl.ANY),
                  pl.BlockSpec((NSA_GROUPS, NSA_REP, 1, s, NSA_DH), lambda i, pt: (0, 0, i, 0, 0)),
                  pl.BlockSpec(w.shape, lambda i, pt: (0, 0)),
                  pl.BlockSpec(pe.shape, lambda i, pt: (0, 0)),
                  pl.BlockSpec(ovl.shape, lambda i, pt: (0, 0))],
        out_specs=[pl.BlockSpec((1, s, NSA_HEADS * NSA_DH), lambda i, pt: (i, 0, 0)),
                   pl.BlockSpec((1, s, NSA_GROUPS * n_sel), lambda i, pt: (i, 0, 0))],
        scratch_shapes=[pltpu.VMEM((2, NSA_KV_ROW, past), F32), pltpu.SemaphoreType.DMA((2,)),
                        pltpu.VMEM((NSA_KV_ROW // V7X_LANES, past, V7X_LANES), F32)])
    return pl.pallas_call(
        kern, grid_spec=grid_spec,
        out_shape=[_sds((sb, s, NSA_HEADS * NSA_DH)), _sds((sb, s, NSA_GROUPS * n_sel), BF16)],
        compiler_params=_params(("arbitrary",), 56), name="nsa_cmp_sample")(
            page_table.reshape(-1), pool4, q5, w, pe, ovl)


def _nsa_sw_sample_kernel(pt_ref, pool_ref, q_ref, knew_ref, wnew_ref, win_ref, sel_ref, ocmp_ref,
                          gate_ref, o_ref, buf_ref, sem_ref, *, layer, n_pages, s, chunk):
    slot = _paged_fetch(pool_ref, layer, pt_ref, n_pages, buf_ref, sem_ref)
    past = n_pages * PAGE_ROWS
    n_sel = sel_ref.shape[2] // NSA_GROUPS
    m = NSA_HEADS * s
    vrow = NSA_KV_ROW // 2
    qm = _group_query_rows(q_ref, s)
    sq = lax.broadcasted_iota(I32, (s, s), 0)
    sk = lax.broadcasted_iota(I32, (s, s), 1)
    new_mask = _group_rows([sk <= sq] * NSA_GROUPS)

    state = _online_init(m, vrow)
    for c0 in range(0, past, chunk):
        kv = buf_ref[slot, :, c0:c0 + chunk].astype(BF16)
        expand = (lax.broadcasted_iota(I32, (n_sel, chunk), 0)
                  == (c0 + lax.broadcasted_iota(I32, (n_sel, chunk), 1)) // SEL_BLOCK)
        expand = jnp.where(expand, 1.0, 0.0).astype(BF16)
        chosen = [_dot(sel_ref[0, :, g * n_sel:(g + 1) * n_sel], expand) > 0.5
                  for g in range(NSA_GROUPS)]
        state = _online(state, _dot(qm, kv), _group_rows(chosen), kv[vrow:], v_t=True)
    knew = knew_ref[0].astype(BF16)
    state = _online(state, _dot_t(qm, knew), new_mask, knew[:, vrow:])
    o_slc = state[2] / jnp.maximum(state[1], TINY)

    wb = win_ref.shape[2]
    win = win_ref[0].astype(BF16)
    qrow = lax.broadcasted_iota(I32, (s, wb), 0)
    dist = qrow + wb - lax.broadcasted_iota(I32, (s, wb), 1)
    old_mask = _group_rows([(dist >= 0) & (dist < WINDOW)] * NSA_GROUPS)
    state = _online(_online_init(m, vrow), _dot(qm, win), old_mask, win[vrow:], v_t=True)
    wnew = wnew_ref[0].astype(BF16)
    state = _online(state, _dot_t(qm, wnew), new_mask, wnew[:, vrow:])
    o_win = state[2] / jnp.maximum(state[1], TINY)

    gates = gate_ref[0]
    ocmp = ocmp_ref[0]
    for g in range(NSA_GROUPS):
        vcols = slice(g * NSA_DH, (g + 1) * NSA_DH)
        for r in range(NSA_REP):
            hd = g * NSA_REP + r
            rows = slice(hd * s, (hd + 1) * s)
            cols = slice(hd * NSA_DH, (hd + 1) * NSA_DH)
            o_ref[0, :, cols] = (gates[:, 3 * hd:3 * hd + 1] * ocmp[:, cols]
                                 + gates[:, 3 * hd + 1:3 * hd + 2] * o_slc[rows, vcols]
                                 + gates[:, 3 * hd + 2:3 * hd + 3] * o_win[rows, vcols])


def _nsa_sw_sample(q, pool, layer, page_table, kvs_new, kvw_new, win_buf, sel, o_cmp, gates, sb, s):
    n_pages = page_table.shape[1]
    past = n_pages * PAGE_ROWS
    wb = win_buf.shape[1]
    assert wb == WINDOW and s <= SEL_BLOCK and past % SEL_BLOCK == 0
    pool4 = _channel_major(pool)
    q5 = q.reshape(NSA_GROUPS, NSA_REP, sb, s, NSA_DH)
    per_b = lambda a: a.reshape(sb, s, a.shape[-1])
    blk = lambda a: pl.BlockSpec((1,) + a.shape[1:], lambda i, pt: (i, 0, 0))
    win_t = _channel_major(win_buf[None])[0]
    rows = [per_b(kvs_new), per_b(kvw_new), win_t, sel, o_cmp, per_b(gates)]
    kern = functools.partial(_nsa_sw_sample_kernel, layer=layer, n_pages=n_pages, s=s,
                             chunk=min(2048, past))
    grid_spec = pltpu.PrefetchScalarGridSpec(
        num_scalar_prefetch=1, grid=(sb,),
        in_specs=[pl.BlockSpec(memory_space=pl.ANY),
                  pl.BlockSpec((NSA_GROUPS, NSA_REP, 1, s, NSA_DH), lambda i, pt: (0, 0, i, 0, 0))]
        + [blk(a) for a in rows],
        out_specs=pl.BlockSpec((1, s, NSA_HEADS * NSA_DH), lambda i, pt: (i, 0, 0)),
        scratch_shapes=[pltpu.VMEM((2, NSA_KV_ROW, past), F32), pltpu.SemaphoreType.DMA((2,))])
    return pl.pallas_call(
        kern, grid_spec=grid_spec, out_shape=_sds((sb, s, NSA_HEADS * NSA_DH)),
        compiler_params=_params(("arbitrary",), 56), name="nsa_slc_win_sample")(
            page_table.reshape(-1), pool4, q5, *rows)


def _conv_sample_kernel(u_ref, gb_ref, cb_ref, w_ref, o_ref):
    u = u_ref[...]
    cb = cb_ref[...]
    row = lax.broadcasted_iota(I32, u.shape, 1)
    u1 = jnp.where(row == 0, cb[:, 1:2], pltpu.roll(u, 1, 1))
    u2 = jnp.where(row == 0, cb[:, 0:1], jnp.where(row == 1, cb[:, 1:2], pltpu.roll(u, 2, 1)))
    w = w_ref[...]
    o_ref[...] = gb_ref[...] * (w[0:1][None] * u2 + w[1:2][None] * u1 + w[2:3][None] * u)


def _conv_sample(u, gb, conv_buf, conv_w, sb, s):
    assert s >= CONV_W - 1
    c = u.shape[1]
    full = lambda shape: pl.BlockSpec(shape, lambda i, n=len(shape): (0,) * n)
    return pl.pallas_call(
        _conv_sample_kernel, grid=(1,),
        in_specs=[full((sb, s, c)), full((sb, s, c)), full(conv_buf.shape), full(conv_w.shape)],
        out_specs=full((sb, s, c)), out_shape=_sds((sb, s, c)),
        compiler_params=_params(("arbitrary",), 32), name="conv_sample")(
            u.reshape(sb, s, c), gb.reshape(sb, s, c), conv_buf, conv_w).reshape(sb * s, c)


def _even_layer_sample(x, sb, s, layer, pool_cmp, pool_slc, win_buf, conv_buf, page_table,
                       g_mix, w_in, w_out, cmp_w, cmp_pe, conv_w):
    q, kvc, kvs, kvw, _, gates, gb, u = _even_proj(x, g_mix, w_in)
    cw, cpe = _compress_weights(cmp_w, cmp_pe)
    o_cmp, sel = _nsa_cmp_sample(q, pool_cmp, layer, page_table, cw, cpe, sb, s)
    o_nsa = _nsa_sw_sample(q, pool_slc, layer, page_table, kvs, kvw, win_buf, sel, o_cmp, gates,
                           sb, s)
    o_conv = _conv_sample(u, gb, conv_buf, conv_w, sb, s)
    qc = NSA_HEADS * NSA_DH
    y = _matmul_res([o_nsa.reshape(sb * s, qc), o_conv], [w_out[:qc], w_out[qc:]], x,
                    "even_out_proj")
    five = lambda a: a.reshape(sb, s, 2, NSA_GROUPS, NSA_DH)
    win_all = jnp.concatenate([win_buf, five(kvw)], axis=1)[:, s:]
    conv_all = jnp.concatenate([conv_buf, u.reshape(sb, s, CONV_CH)], axis=1)[:, s:]
    return y, (five(kvc), five(kvs), win_all, conv_all)


def _rot_half_cols(w):
    half = w.shape[-1] // 2
    return jnp.concatenate([-w[..., half:], w[..., :half]], axis=-1)


def _rope_tables(pos):
    half = ROPE // 2
    inv = ROPE_THETA ** (-jnp.arange(half, dtype=F32) / half)
    ang = pos.astype(F32)[:, None] * inv[None, :]
    cos, sin = jnp.cos(ang), jnp.sin(ang)
    cos32 = jnp.concatenate([cos, cos], axis=1)
    sin32 = jnp.concatenate([sin, sin], axis=1)
    n = pos.shape[0]
    pad = MLA_QK - NOPE - ROPE
    cosq = jnp.concatenate([jnp.ones((n, NOPE), F32), cos32, jnp.zeros((n, pad), F32)], axis=1)
    sinq = jnp.concatenate([jnp.zeros((n, NOPE), F32), sin32, jnp.zeros((n, pad), F32)], axis=1)
    return cos32, sin32, cosq, sinq


def _mla_weights(w_in, w_uq, w_uk):
    wcq = w_in[:, :Q_LORA]
    wckv = w_in[:, Q_LORA:Q_LORA + KV_LORA]
    wkpe = w_in[:, Q_LORA + KV_LORA:]
    uq = w_uq.reshape(Q_LORA, MLA_HEADS, NOPE + ROPE)
    pad = jnp.zeros((Q_LORA, MLA_HEADS, MLA_QK - NOPE - ROPE), F32)
    wq = jnp.concatenate([uq, pad], axis=-1).reshape(Q_LORA, MLA_HEADS * MLA_QK)
    wq_rot = jnp.concatenate([jnp.zeros((Q_LORA, MLA_HEADS, NOPE), F32),
                              _rot_half_cols(uq[..., NOPE:]), pad], axis=-1).reshape(wq.shape)
    kpad = jnp.zeros((KV_LORA, MLA_HEADS, MLA_QK - NOPE), F32)
    wk_lat = jnp.concatenate([w_uk, kpad], axis=-1).reshape(KV_LORA, MLA_HEADS * MLA_QK)
    eye = jnp.broadcast_to(jnp.eye(ROPE, dtype=F32)[:, None, :], (ROPE, MLA_HEADS, ROPE))
    wk_pe = jnp.concatenate([jnp.zeros((ROPE, MLA_HEADS, NOPE), F32), eye,
                             jnp.zeros((ROPE, MLA_HEADS, MLA_QK - NOPE - ROPE), F32)],
                            axis=-1).reshape(ROPE, MLA_HEADS * MLA_QK)
    bf = lambda a: a.astype(BF16)
    return dict(wcq=bf(wcq), wckv=bf(wckv), wkpe=bf(wkpe), wkpe_rot=bf(_rot_half_cols(wkpe)),
                wq=bf(wq), wq_rot=bf(wq_rot), wk_lat=bf(wk_lat), wk_pe=bf(wk_pe))


def _mla_latent(x_ref, cos_ref, sin_ref, g_ref, qn_ref, kvn_ref, wcq_ref, wckv_ref, wkpe_ref,
                wkrot_ref, lat_ref):
    h = _rms(x_ref[...], g_ref[...]).astype(BF16)
    cq = _rms(_dot(h, wcq_ref[...]), qn_ref[...]).astype(BF16)
    ckv = _rms(_dot(h, wckv_ref[...]), kvn_ref[...])
    kpe = _dot(h, wkpe_ref[...]) * cos_ref[...] + _dot(h, wkrot_ref[...]) * sin_ref[...]
    lat_ref[:, :KV_LORA] = ckv
    lat_ref[:, KV_LORA:] = kpe
    return cq, ckv.astype(BF16), kpe.astype(BF16)


def _mla_queries(cq, wq_ref, wqrot_ref, cosq, sinq, head):
    cols = slice(head * MLA_QK, (head + 1) * MLA_QK)
    q = _dot(cq, wq_ref[:, cols]) * cosq + _dot(cq, wqrot_ref[:, cols]) * sinq
    return q * ((NOPE + ROPE) ** -0.5 * LOG2E)


def _mla_proj_prompt_body(rows, consts, outs):
    x_ref, cos_ref, sin_ref, cosq_ref, sinq_ref = rows
    (g_ref, qn_ref, kvn_ref, wcq_ref, wckv_ref, wkpe_ref, wkrot_ref, wq_ref, wqrot_ref,
     wklat_ref, wkpe2_ref, wuv_ref) = consts
    lat_ref, q_ref, k_ref, v_ref = outs
    cq, ckv, kpe = _mla_latent(x_ref, cos_ref, sin_ref, g_ref, qn_ref, kvn_ref, wcq_ref,
                               wckv_ref, wkpe_ref, wkrot_ref, lat_ref)
    cosq, sinq = cosq_ref[...], sinq_ref[...]
    for hd in range(MLA_HEADS):
        cols = slice(hd * MLA_QK, (hd + 1) * MLA_QK)
        q_ref[:, cols] = _mla_queries(cq, wq_ref, wqrot_ref, cosq, sinq, hd).astype(BF16)
    k_ref[...] = (_dot(ckv, wklat_ref[...]) + _dot(kpe, wkpe2_ref[...])).astype(BF16)
    v_ref[...] = _dot(ckv, wuv_ref[...]).astype(BF16)


def _mla_proj_sample_body(rows, consts, outs):
    x_ref, cos_ref, sin_ref, cosq_ref, sinq_ref = rows
    (g_ref, qn_ref, kvn_ref, wcq_ref, wckv_ref, wkpe_ref, wkrot_ref, wq_ref, wqrot_ref,
     wukt_ref) = consts
    lat_ref, q_ref = outs
    cq, _, _ = _mla_latent(x_ref, cos_ref, sin_ref, g_ref, qn_ref, kvn_ref, wcq_ref, wckv_ref,
                           wkpe_ref, wkrot_ref, lat_ref)
    cosq, sinq = cosq_ref[...], sinq_ref[...]
    for hd in range(MLA_HEADS):
        q = _mla_queries(cq, wq_ref, wqrot_ref, cosq, sinq, hd)
        q_ref[hd, :, :KV_LORA] = _dot(q[:, :NOPE].astype(BF16), wukt_ref[hd])
        q_ref[hd, :, KV_LORA:] = q[:, NOPE:NOPE + ROPE]


def _mla_proj(x, pos, g_mix, q_norm, kv_norm, mw, extra, body, outs, name):
    r, d = x.shape
    tm = _row_tile(pos.shape[0], 512)
    cos32, sin32, cosq, sinq = _rope_tables(pos)
    n_per = pos.shape[0] // tm
    periods = {k: n_per for k in (1, 2, 3, 4)}
    consts = [g_mix.reshape(1, d), q_norm.reshape(1, -1), kv_norm.reshape(1, -1), mw["wcq"],
              mw["wckv"], mw["wkpe"], mw["wkpe_rot"], mw["wq"], mw["wq_rot"]] + extra
    return _rowwise(body, [x, cos32, sin32, cosq, sinq], consts, outs, tm, name, periods=periods)


def _mla_flash_kernel(qi_ref, ki_ref, q_ref, k_ref, v_ref, o_ref, m_ref, l_ref, acc_ref, *, tq, tk):
    n = pl.program_id(2)
    qi = qi_ref[n]
    ki = ki_ref[n]
    last = (qi * tq + tq - 1) // tk

    @pl.when(ki == 0)
    def _():
        m_ref[...] = jnp.full(m_ref.shape, NEG, F32)
        l_ref[...] = jnp.zeros(l_ref.shape, F32)
        acc_ref[...] = jnp.zeros(acc_ref.shape, F32)

    def step(masked):
        if masked:
            mask = (ki * tk + lax.broadcasted_iota(I32, (tq, tk), 1)
                    <= qi * tq + lax.broadcasted_iota(I32, (tq, tk), 0))
        for j in range(2):
            s = _dot_t(q_ref[:, j * MLA_QK:(j + 1) * MLA_QK], k_ref[:, j * MLA_QK:(j + 1) * MLA_QK])
            if masked:
                s = jnp.where(mask, s, NEG)
            m_prev = m_ref[j]
            m_new = jnp.maximum(m_prev, jnp.max(s, axis=-1, keepdims=True))
            p = jnp.exp(s - m_new)
            if masked:
                p = jnp.where(mask, p, 0.0)
            alpha = jnp.exp(m_prev - m_new)
            l_ref[j] = alpha * l_ref[j] + jnp.sum(p, axis=-1, keepdims=True)
            acc_ref[j] = alpha * acc_ref[j] + _dot(p.astype(BF16), v_ref[:, j * V_DIM:(j + 1) * V_DIM])
            m_ref[j] = m_new

    needs_mask = ki * tk + tk - 1 > qi * tq

    @pl.when(needs_mask)
    def _():
        step(True)

    @pl.when(jnp.logical_not(needs_mask))
    def _():
        step(False)

    @pl.when(ki == last)
    def _():
        for j in range(2):
            o_ref[:, j * V_DIM:(j + 1) * V_DIM] = acc_ref[j] / jnp.maximum(l_ref[j], TINY)


def _causal_steps(nq, tq, tk):
    qi, ki = [], []
    for j in range(nq):
        for k in range((j * tq + tq - 1) // tk + 1):
            qi.append(j)
            ki.append(k)
    return jnp.asarray(qi, I32), jnp.asarray(ki, I32)


def _mla_flash(q, k, v, b, t, tq=512, tk=512):
    tq, tk = min(tq, t), min(tk, t)
    nq, nk = t // tq, t // tk
    qi, ki = _causal_steps(nq, tq, tk)
    pairs = MLA_HEADS // 2
    grid_spec = pltpu.PrefetchScalarGridSpec(
        num_scalar_prefetch=2, grid=(b, pairs, qi.shape[0]),
        in_specs=[pl.BlockSpec((tq, 2 * MLA_QK), lambda i, h, n, qt, kt: (i * nq + qt[n], h)),
                  pl.BlockSpec((tk, 2 * MLA_QK), lambda i, h, n, qt, kt: (i * nk + kt[n], h)),
                  pl.BlockSpec((tk, 2 * V_DIM), lambda i, h, n, qt, kt: (i * nk + kt[n], h))],
        out_specs=pl.BlockSpec((tq, 2 * V_DIM), lambda i, h, n, qt, kt: (i * nq + qt[n], h)),
        scratch_shapes=[pltpu.VMEM((2, tq, 1), F32), pltpu.VMEM((2, tq, 1), F32),
                        pltpu.VMEM((2, tq, V_DIM), F32)])
    return pl.pallas_call(
        functools.partial(_mla_flash_kernel, tq=tq, tk=tk), grid_spec=grid_spec,
        out_shape=_sds((b * t, MLA_HEADS * V_DIM)),
        compiler_params=_params(("arbitrary", "arbitrary", "arbitrary"), 48),
        name="mla_flash_prompt")(qi, ki, q, k, v)


def _mla_flash_t_kernel(qi_ref, ki_ref, q_ref, k_ref, v_ref, o_ref, m_ref, l_ref, acc_ref, *, tq, tk):
    n = pl.program_id(2)
    qi = qi_ref[n]
    ki = ki_ref[n]
    last = (qi * tq + tq - 1) // tk

    @pl.when(ki == 0)
    def _():
        m_ref[...] = jnp.full(m_ref.shape, NEG, F32)
        l_ref[...] = jnp.zeros(l_ref.shape, F32)
        acc_ref[...] = jnp.zeros(acc_ref.shape, F32)

    def step(masked):
        mask = None
        if masked:
            mask = (ki * tk + lax.broadcasted_iota(I32, (tk, tq), 0)
                    <= qi * tq + lax.broadcasted_iota(I32, (tk, tq), 1))
        for j in range(2):
            qk = slice(j * MLA_QK, (j + 1) * MLA_QK)
            _flash_update_t(q_ref[:, qk], k_ref[:, qk], v_ref[:, j * V_DIM:(j + 1) * V_DIM], mask,
                            m_ref.at[j], l_ref.at[j], acc_ref.at[j])

    needs_mask = ki * tk + tk - 1 > qi * tq

    @pl.when(needs_mask)
    def _():
        step(True)

    @pl.when(jnp.logical_not(needs_mask))
    def _():
        step(False)

    @pl.when(ki == last)
    def _():
        for j in range(2):
            o = acc_ref[j] / jnp.maximum(l_ref[j], TINY)
            o_ref[:, j * V_DIM:(j + 1) * V_DIM] = o.T


def _mla_flash_t(q, k, v, b, t, tq=512, tk=512):
    tq, tk = min(tq, t), min(tk, t)
    nq, nk = t // tq, t // tk
    qi, ki = _causal_steps(nq, tq, tk)
    pairs = MLA_HEADS // 2
    grid_spec = pltpu.PrefetchScalarGridSpec(
        num_scalar_prefetch=2, grid=(b, pairs, qi.shape[0]),
        in_specs=[pl.BlockSpec((tq, 2 * MLA_QK), lambda i, h, n, qt, kt: (i * nq + qt[n], h)),
                  pl.BlockSpec((tk, 2 * MLA_QK), lambda i, h, n, qt, kt: (i * nk + kt[n], h)),
                  pl.BlockSpec((tk, 2 * V_DIM), lambda i, h, n, qt, kt: (i * nk + kt[n], h))],
        out_specs=pl.BlockSpec((tq, 2 * V_DIM), lambda i, h, n, qt, kt: (i * nq + qt[n], h)),
        scratch_shapes=[pltpu.VMEM((2, 1, tq), F32), pltpu.VMEM((2, 1, tq), F32),
                        pltpu.VMEM((2, V_DIM, tq), F32)])
    return pl.pallas_call(
        functools.partial(_mla_flash_t_kernel, tq=tq, tk=tk), grid_spec=grid_spec,
        out_shape=_sds((b * t, MLA_HEADS * V_DIM)),
        compiler_params=_params(("arbitrary", "arbitrary", "arbitrary"), 48),
        name="mla_flash_prompt")(qi, ki, q, k, v)


def _odd_layer_prompt(x, b, t, g_mix, w_in, q_norm, w_uq, kv_norm, w_uk, w_uv, w_out):
    mw = _mla_weights(w_in, w_uq, w_uk)
    r = x.shape[0]
    outs = [_sds((r, LAT)), _sds((r, MLA_HEADS * MLA_QK), BF16), _sds((r, MLA_HEADS * MLA_QK), BF16),
            _sds((r, MLA_HEADS * V_DIM), BF16)]
    extra = [mw["wk_lat"], mw["wk_pe"], w_uv.reshape(KV_LORA, MLA_HEADS * V_DIM).astype(BF16)]
    lat, q, k, v = _mla_proj(x, jnp.arange(t, dtype=I32), g_mix, q_norm, kv_norm, mw, extra,
                             _mla_proj_prompt_body, outs, "mla_proj_prompt")
    o = _mla_flash_t(q, k, v, b, t)
    y = _matmul_res([o], [w_out], x, "mla_out_proj")
    return y, lat.reshape(b, t, LAT)


def _mla_sample_kernel(pt_ref, pool_ref, q_ref, new_ref, o_ref, buf_ref, sem_ref, *,
                       layer, n_pages, s, chunk):
    slot = _paged_fetch(pool_ref, layer, pt_ref, n_pages, buf_ref, sem_ref)
    past = n_pages * PAGE_ROWS
    m = MLA_HEADS * s
    q = q_ref[:, 0].reshape(m, LAT).astype(BF16)
    state = _online_init(m, KV_LORA)
    for c0 in range(0, past, chunk):
        kv = buf_ref[slot, :, c0:c0 + chunk].astype(BF16)
        state = _online(state, _dot(q, kv), None, kv[:KV_LORA], v_t=True)
    new = new_ref[0].astype(BF16)
    causal = (lax.broadcasted_iota(I32, (s, s), 1) <= lax.broadcasted_iota(I32, (s, s), 0))
    state = _online(state, _dot_t(q, new), jnp.tile(causal, (MLA_HEADS, 1)), new[:, :KV_LORA])
    o = state[2] / jnp.maximum(state[1], TINY)
    o_ref[:, 0] = o.reshape(MLA_HEADS, s, KV_LORA)


def _mla_sample_attend(q, pool, layer, page_table, lat_new, sb, s):
    n_pages = page_table.shape[1]
    past = n_pages * PAGE_ROWS
    q4 = q.reshape(MLA_HEADS, sb, s, LAT)
    kern = functools.partial(_mla_sample_kernel, layer=layer, n_pages=n_pages, s=s,
                             chunk=min(2048, past))
    grid_spec = pltpu.PrefetchScalarGridSpec(
        num_scalar_prefetch=1, grid=(sb,),
        in_specs=[pl.BlockSpec(memory_space=pl.ANY),
                  pl.BlockSpec((MLA_HEADS, 1, s, LAT), lambda i, pt: (0, i, 0, 0)),
                  pl.BlockSpec((1, s, LAT), lambda i, pt: (i, 0, 0))],
        out_specs=pl.BlockSpec((MLA_HEADS, 1, s, KV_LORA), lambda i, pt: (0, i, 0, 0)),
        scratch_shapes=[pltpu.VMEM((2, LAT, past), F32), pltpu.SemaphoreType.DMA((2,))])
    o = pl.pallas_call(
        kern, grid_spec=grid_spec, out_shape=_sds((MLA_HEADS, sb, s, KV_LORA)),
        compiler_params=_params(("arbitrary",), 56), name="mla_paged_sample")(
            page_table.reshape(-1), _channel_major(pool), q4, lat_new.reshape(sb, s, LAT))
    return o.reshape(MLA_HEADS, sb * s, KV_LORA)


def _mla_out_sample_body(rows, consts, outs):
    o_ref, res_ref = rows
    wuv_ref, wout_ref = consts
    acc = res_ref[...]
    for hd in range(MLA_HEADS):
        o_h = _dot(o_ref[hd].astype(BF16), wuv_ref[hd])
        acc = acc + _dot(o_h.astype(BF16), wout_ref[hd])
    outs[0][...] = acc


def _odd_layer_sample(x, sb, s, layer, pool, page_table, g_mix, w_in, q_norm, w_uq, kv_norm, w_uk,
                      w_uv, w_out):
    mw = _mla_weights(w_in, w_uq, w_uk)
    r = x.shape[0]
    past = page_table.shape[1] * PAGE_ROWS
    pos = past + jnp.arange(_row_tile(r, 512), dtype=I32) % s
    outs = [_sds((r, LAT)), _sds((MLA_HEADS, r, LAT))]
    wukt = w_uk.transpose(1, 2, 0).astype(BF16)
    lat, q = _mla_proj(x, pos, g_mix, q_norm, kv_norm, mw, [wukt],
                       _mla_proj_sample_body, outs, "mla_proj_sample")
    o_lat = _mla_sample_attend(q, pool, layer, page_table, lat, sb, s)
    wuv = w_uv.transpose(1, 0, 2).astype(BF16)
    wout = w_out.reshape(MLA_HEADS, V_DIM, -1).astype(BF16)
    y = _rowwise(_mla_out_sample_body, [o_lat, x], [wuv, wout], [_sds(x.shape)],
                 _row_tile(r, 512), "mla_out_sample")[0]
    return y, lat.reshape(sb, s, LAT)


def _plain_matmul_body(rows, consts, outs):
    outs[0][...] = _dot(rows[0][...].astype(BF16), consts[0][...])


def _mem_q_body(rows, consts, outs):
    g_ref, w_ref = consts
    h = _rms(rows[0][...], g_ref[...]).astype(BF16)
    dh = w_ref.shape[1] // MEM_HEADS
    outs[0][...] = (_dot(h, w_ref[...]) * (dh ** -0.5 * LOG2E)).astype(BF16)


def _mem_attn_kernel(q_ref, kv_ref, o_ref, *, split):
    d = q_ref.shape[2]
    dh = d // MEM_HEADS
    for hd in range(MEM_HEADS):
        if split:
            k = kv_ref[0, 0, :, 0, hd, :].astype(BF16)
            v = kv_ref[0, 0, :, 1, hd, :].astype(BF16)
        else:
            k = kv_ref[0, :, hd * dh:(hd + 1) * dh].astype(BF16)
            v = kv_ref[0, :, d + hd * dh:d + (hd + 1) * dh].astype(BF16)
        s = _dot_t(q_ref[0, :, hd * dh:(hd + 1) * dh], k)
        p = jnp.exp2(s - jnp.max(s, axis=-1, keepdims=True))
        p = p / jnp.sum(p, axis=-1, keepdims=True)
        o_ref[0, :, hd * dh:(hd + 1) * dh] = _dot(p.astype(BF16), v).astype(BF16)


def _mem_layer(y, nb, kv, g_mem, w_q, w_o, layer=None):
    r, d = y.shape
    rows = r // nb
    q = _rowwise(_mem_q_body, [y], [g_mem.reshape(1, d), w_q.astype(BF16)], [_sds((r, d), BF16)],
                 _row_tile(r, 512), "mem_q_proj")[0]
    tq = _row_tile(rows, 512)
    nt = rows // tq
    if layer is None:
        kv_spec = pl.BlockSpec((1,) + kv.shape[1:], lambda i, j: (i, 0, 0))
    else:
        kv_spec = pl.BlockSpec((1, 1) + kv.shape[2:], lambda i, j: (layer, i, 0, 0, 0, 0))
    o = pl.pallas_call(
        functools.partial(_mem_attn_kernel, split=layer is not None), grid=(nb, nt),
        in_specs=[pl.BlockSpec((1, tq, d), lambda i, j: (i, j, 0)), kv_spec],
        out_specs=pl.BlockSpec((1, tq, d), lambda i, j: (i, j, 0)),
        out_shape=_sds((nb, rows, d), BF16),
        compiler_params=_params(("arbitrary", "arbitrary"), 48), name="mem_attention")(
            q.reshape(nb, rows, d), kv)
    return _matmul_res([o.reshape(r, d)], [w_o], y, "mem_out_proj")


def _mem_project(mem, w_kv):
    nb, m, d = mem.shape
    kv = _rowwise(_plain_matmul_body, [mem.reshape(nb * m, d)], [w_kv.astype(BF16)],
                  [_sds((nb * m, w_kv.shape[1]))], _row_tile(nb * m, 512), "mem_kv_proj")[0]
    return kv.reshape(nb, m, w_kv.shape[1])


def _route(x, wr_ref, br_ref):
    xh, xm, xl = _split3(x)
    w0, w1, w2 = wr_ref[0], wr_ref[1], wr_ref[2]
    logit = (_dot(xh, w0) + (_dot(xh, w1) + _dot(xm, w0))
             + (_dot(xh, w2) + _dot(xl, w0) + _dot(xm, w1))) + br_ref[...]
    lane = lax.broadcasted_iota(I32, logit.shape, 1).astype(F32)
    big = float(ROUTE_LANES)
    is_g = lane < N_GROUPS
    gl = jnp.where(is_g, logit, -jnp.inf)
    gmax = jnp.max(gl, axis=-1, keepdims=True)
    gidx = jnp.min(jnp.where(gl == gmax, lane, big), axis=-1, keepdims=True)
    g_gate = 1.0 / jnp.sum(jnp.where(is_g, jnp.exp(gl - gmax), 0.0), axis=-1, keepdims=True)
    lo = N_GROUPS + gidx * EXP_PER_GROUP
    in_grp = (lane >= lo) & (lane < lo + EXP_PER_GROUP)
    el = jnp.where(in_grp, logit, -jnp.inf)
    ee = jnp.where(in_grp, jnp.exp(el - jnp.max(el, axis=-1, keepdims=True)), 0.0)
    prob = ee / jnp.sum(ee, axis=-1, keepdims=True)
    cand = jnp.where(in_grp, prob, -1.0)
    p1 = jnp.max(cand, axis=-1, keepdims=True)
    i1 = jnp.min(jnp.where(cand == p1, lane, big), axis=-1, keepdims=True)
    cand = jnp.where(lane == i1, -1.0, cand)
    p2 = jnp.max(cand, axis=-1, keepdims=True)
    i2 = jnp.min(jnp.where(cand == p2, lane, big), axis=-1, keepdims=True)
    tot = p1 + p2
    return (jnp.where(lane == i1, p1 / tot * g_gate, 0.0)
            + jnp.where(lane == i2, p2 / tot * g_gate, 0.0))


def _moe_kernel(y_ref, g_ref, wr_ref, br_ref, w13_ref, w2_ref, gf_ref, o_ref, h_ref, comb_ref,
                acc_ref, *, final_norm):
    e = pl.program_id(1)

    @pl.when(e == 0)
    def _():
        x = _rms(y_ref[...], g_ref[...])
        h_ref[...] = x.astype(BF16)
        comb_ref[...] = _route(x, wr_ref, br_ref)
        acc_ref[...] = jnp.zeros(acc_ref.shape, F32)

    hc = _dot(h_ref[...], w13_ref[0])
    ff = hc.shape[1] // 2
    a, bb = hc[:, :ff], hc[:, ff:]
    comb = comb_ref[...]
    lane = lax.broadcasted_iota(I32, comb.shape, 1)
    ce = jnp.sum(jnp.where(lane == e + N_GROUPS, comb, 0.0), axis=-1, keepdims=True)
    act = a * (1.0 / (1.0 + jnp.exp(-a))) * bb * ce
    acc_ref[...] += _dot(act.astype(BF16), w2_ref[0])

    @pl.when(e == pl.num_programs(1) - 1)
    def _():
        out = y_ref[...] + acc_ref[...]
        if final_norm:
            out = _rms(out, gf_ref[...])
        o_ref[...] = out


def _moe_layer(y, g_ffn, w_group, b_group, w_expert, b_expert, w13, w2, g_final=None):
    r, d = y.shape
    tm = _row_tile(r, 1024)
    n_route = N_GROUPS + N_EXPERTS
    wr = jnp.concatenate([w_group, w_expert, jnp.zeros((d, ROUTE_LANES - n_route), F32)], axis=1)
    wr = jnp.stack(_split3(wr))
    br = jnp.concatenate([b_group, b_expert, jnp.zeros((ROUTE_LANES - n_route,), F32)]).reshape(1, -1)
    gf = (g_ffn if g_final is None else g_final).reshape(1, d)
    ff2 = w13.shape[2]
    return pl.pallas_call(
        functools.partial(_moe_kernel, final_norm=g_final is not None), grid=(r // tm, N_EXPERTS),
        in_specs=[pl.BlockSpec((tm, d), lambda i, e: (i, 0)),
                  pl.BlockSpec((1, d), lambda i, e: (0, 0)),
                  pl.BlockSpec(wr.shape, lambda i, e: (0, 0, 0)),
                  pl.BlockSpec(br.shape, lambda i, e: (0, 0)),
                  pl.BlockSpec((1, d, ff2), lambda i, e: (e, 0, 0)),
                  pl.BlockSpec((1, ff2 // 2, d), lambda i, e: (e, 0, 0)),
                  pl.BlockSpec((1, d), lambda i, e: (0, 0))],
        out_specs=pl.BlockSpec((tm, d), lambda i, e: (i, 0)),
        out_shape=_sds((r, d)),
        scratch_shapes=[pltpu.VMEM((tm, d), BF16), pltpu.VMEM((tm, ROUTE_LANES), F32),
                        pltpu.VMEM((tm, d), F32)],
        compiler_params=_params(("arbitrary", "arbitrary"), 56), name="hier_moe")(
            y, g_ffn.reshape(1, d), wr, br, w13.astype(BF16), w2.astype(BF16), gf)


def kernel(x_prompt, x_sample, mem_prompt, cache_nsa_cmp_kv, cache_nsa_slc_kv, state_nsa_win_kv, state_conv, cache_mla_latent, cache_mem_kv, page_table, norm_mix, norm_mem, norm_ffn, norm_final, e_w_in, e_w_out, e_cmp_w, e_cmp_pe, e_conv_w, o_w_in, o_q_norm, o_w_uq, o_kv_norm, o_w_uk, o_w_uv, o_w_out, mem_w_q, mem_w_kv, mem_w_o, moe_w_group, moe_b_group, moe_w_expert, moe_b_expert, moe_w13, moe_w2):
    b, t, d = x_prompt.shape
    sb, s, _ = x_sample.shape
    depth = norm_mix.shape[0]
    yp = x_prompt.reshape(b * t, d)
    ys = x_sample.reshape(sb * s, d)
    p_even, s_even, p_mla, s_mla, p_mem = [], [], [], [], []
    for l in range(depth):
        if l % 2 == 0:
            e = l // 2
            w = (norm_mix[l], e_w_in[e], e_w_out[e], e_cmp_w[e], e_cmp_pe[e], e_conv_w[e])
            yp, cp = _even_layer_prompt(yp, b, t, *w)
            ys, cs = _even_layer_sample(ys, sb, s, e, cache_nsa_cmp_kv, cache_nsa_slc_kv,
                                        state_nsa_win_kv[e], state_conv[e], page_table, *w)
            p_even.append(cp)
            s_even.append(cs)
        else:
            o = l // 2
            w = (norm_mix[l], o_w_in[o], o_q_norm[o], o_w_uq[o], o_kv_norm[o], o_w_uk[o], o_w_uv[o],
                 o_w_out[o])
            yp, lp = _odd_layer_prompt(yp, b, t, *w)
            ys, ls = _odd_layer_sample(ys, sb, s, o, cache_mla_latent, page_table, *w)
            p_mla.append(lp)
            s_mla.append(ls)
        kv_p = _mem_project(mem_prompt, mem_w_kv[l])
        p_mem.append(kv_p.reshape(b, kv_p.shape[1], 2, MEM_HEADS, d // MEM_HEADS))
        yp = _mem_layer(yp, b, kv_p, norm_mem[l], mem_w_q[l], mem_w_o[l])
        ys = _mem_layer(ys, sb, cache_mem_kv, norm_mem[l], mem_w_q[l], mem_w_o[l], layer=l)
        g_final = norm_final if l == depth - 1 else None
        moe_w = (norm_ffn[l], moe_w_group[l], moe_b_group[l], moe_w_expert[l], moe_b_expert[l],
                 moe_w13[l], moe_w2[l])
        yp = _moe_layer(yp, *moe_w, g_final=g_final)
        ys = _moe_layer(ys, *moe_w, g_final=g_final)
    stack = lambda caches, k: jnp.stack([c[k] for c in caches])
    return (yp.reshape(b, t, d), ys.reshape(sb, s, d),
            stack(p_even, 0), stack(p_even, 1), stack(p_even, 2), stack(p_even, 3),
            jnp.stack(p_mla), jnp.stack(p_mem),
            stack(s_even, 0), stack(s_even, 1), stack(s_even, 2), stack(s_even, 3),
            jnp.stack(s_mla))
```

```python
import functools
import math

import jax
import jax.numpy as jnp
from jax import lax
from jax.experimental import pallas as pl
from jax.experimental.pallas import tpu as pltpu

F32 = jnp.float32
BF16 = jnp.bfloat16
I32 = jnp.int32

EPS = 1e-6
LOG2E = math.log2(math.e)
NEG = -1e30
TINY = 1e-30
FORCE_BONUS = 1e4

NSA_HEADS = 8
NSA_GROUPS = 2
NSA_REP = NSA_HEADS // NSA_GROUPS
NSA_DH = 64
NSA_KV_ROW = 2 * NSA_GROUPS * NSA_DH
CMP_STRIDE = 16
CMP_BLOCK = 32
SEL_BLOCK = 64
SEL_TOPK = 16
WINDOW = 512
CONV_CH = 512
CONV_W = 3
MLA_HEADS = 16
Q_LORA = 384
KV_LORA = 256
NOPE = 64
ROPE = 32
V_DIM = 64
ROPE_THETA = 10000.0
LAT = KV_LORA + ROPE
MLA_QK = 128
MEM_HEADS = 4
N_GROUPS = 4
EXP_PER_GROUP = 4
N_EXPERTS = 16
EXPERT_FF = 256
ROUTE_LANES = 128

V7X_VMEM_BYTES = 64 * 2**20
V7X_LANES = 128


def _params(sem, vmem_mb):
    assert vmem_mb * 2**20 < V7X_VMEM_BYTES
    return pltpu.CompilerParams(dimension_semantics=sem, vmem_limit_bytes=vmem_mb * 2**20)


def _dot(a, b):
    return jnp.dot(a, b, preferred_element_type=F32)


def _dot_t(a, b):
    return lax.dot_general(a, b, (((1,), (1,)), ((), ())), preferred_element_type=F32)


def _tdot(a, b):
    return lax.dot_general(a, b, (((0,), (0,)), ((), ())), preferred_element_type=F32)


def _split3(x):
    hi = x.astype(BF16)
    r1 = x - hi.astype(F32)
    mid = r1.astype(BF16)
    lo = (r1 - mid.astype(F32)).astype(BF16)
    return hi, mid, lo


def _rms(x, g):
    return x * lax.rsqrt(jnp.mean(x * x, axis=-1, keepdims=True) + EPS) * g


def _row_spec(shape, tm, period=None):
    lead = len(shape) - 2
    blk = tuple(shape[:lead]) + (tm, shape[-1])
    if period is None:
        return pl.BlockSpec(blk, lambda i, lead=lead: (0,) * lead + (i, 0))
    return pl.BlockSpec(blk, lambda i, lead=lead, p=period: (0,) * lead + (i % p, 0))


def _const_spec(shape):
    return pl.BlockSpec(tuple(shape), lambda i, n=len(shape): (0,) * n)


def _rowwise(body, rows, consts, outs, tm, name, vmem_mb=48, periods=None):
    n_rows = rows[0].shape[-2]
    assert n_rows % tm == 0
    periods = periods or {}
    in_specs = [_row_spec(a.shape, tm, periods.get(k)) for k, a in enumerate(rows)]
    in_specs += [_const_spec(c.shape) for c in consts]
    out_specs = [_row_spec(o.shape, tm) for o in outs]
    nr, nc = len(rows), len(consts)

    def kern(*refs):
        body(refs[:nr], refs[nr:nr + nc], refs[nr + nc:])

    return pl.pallas_call(
        kern, grid=(n_rows // tm,), in_specs=in_specs, out_specs=out_specs, out_shape=outs,
        compiler_params=_params(("arbitrary",), vmem_mb), name=name)(*rows, *consts)


def _sds(shape, dtype=F32):
    return jax.ShapeDtypeStruct(tuple(shape), dtype)


def _row_tile(n_rows, want):
    tm = min(want, n_rows)
    assert n_rows % tm == 0
    return tm


def _even_proj_body(rows, consts, outs):
    (x_ref,) = rows
    g_ref, wq_ref, wkv_ref, wg_ref, wc_ref = consts
    q_ref, kvc_ref, kvs_ref, kvw_ref, kvb_ref, gate_ref, gb_ref, u_ref = outs
    h = _rms(x_ref[...], g_ref[...]).astype(BF16)
    q = _dot(h, wq_ref[...]) * (NSA_DH ** -0.5 * LOG2E)
    for hd in range(NSA_HEADS):
        q_ref[hd // NSA_REP, hd % NSA_REP] = q[:, hd * NSA_DH:(hd + 1) * NSA_DH].astype(BF16)
    kv = _dot(h, wkv_ref[...])
    kvc_ref[...] = kv[:, :NSA_KV_ROW]
    kvs_ref[...] = kv[:, NSA_KV_ROW:2 * NSA_KV_ROW]
    kvw_ref[...] = kv[:, 2 * NSA_KV_ROW:]
    for j in range(3 * 2 * NSA_GROUPS):
        kvb_ref[j] = kv[:, j * NSA_DH:(j + 1) * NSA_DH].astype(BF16)
    gate_ref[...] = 1.0 / (1.0 + jnp.exp(-_dot(h, wg_ref[...])))
    c = _dot(h, wc_ref[...])
    gb_ref[...] = c[:, :CONV_CH]
    u_ref[...] = c[:, CONV_CH:2 * CONV_CH] * c[:, 2 * CONV_CH:]


def _even_proj(x, g, w_in):
    r, d = x.shape
    qc = NSA_HEADS * NSA_DH
    kc = 3 * NSA_KV_ROW
    gc = NSA_HEADS * 3
    wq = w_in[:, :qc].astype(BF16)
    wkv = w_in[:, qc:qc + kc].astype(BF16)
    wg = w_in[:, qc + kc:qc + kc + gc].astype(BF16)
    wc = w_in[:, qc + kc + gc:].astype(BF16)
    tm = _row_tile(r, 512)
    outs = [
        _sds((NSA_GROUPS, NSA_REP, r, NSA_DH), BF16),
        _sds((r, NSA_KV_ROW)), _sds((r, NSA_KV_ROW)), _sds((r, NSA_KV_ROW)),
        _sds((12, r, NSA_DH), BF16),
        _sds((r, gc)), _sds((r, CONV_CH)), _sds((r, CONV_CH)),
    ]
    return _rowwise(_even_proj_body, [x], [g.reshape(1, d), wq, wkv, wg, wc], outs, tm, "even_proj")


def _compress_weights(cmp_w, cmp_pe):
    eye_g = jnp.eye(NSA_GROUPS, dtype=F32)
    eye_k = jnp.eye(2, dtype=F32)

    def big(wh):
        return jnp.einsum("kjde,kl,gh->jkgdlhe", wh, eye_k, eye_g).reshape(
            CMP_STRIDE * NSA_KV_ROW, NSA_KV_ROW)

    w = jnp.concatenate([big(cmp_w[:, :CMP_STRIDE]), big(cmp_w[:, CMP_STRIDE:])], axis=1)

    def pe_row(peh):
        return jnp.broadcast_to(peh.transpose(1, 0, 2)[:, :, None, :],
                                (CMP_STRIDE, 2, NSA_GROUPS, NSA_DH)).reshape(1, -1)

    pe = jnp.concatenate([pe_row(cmp_pe[:, :CMP_STRIDE]), pe_row(cmp_pe[:, CMP_STRIDE:]),
                          jnp.zeros((6, CMP_STRIDE * NSA_KV_ROW), F32)], axis=0)
    return w.astype(BF16), pe


def _compress_pair_weights(cmp_w, cmp_pe):
    pairs = CMP_STRIDE // 2
    w = cmp_w.reshape(2, 2, pairs, 2, NSA_DH, NSA_DH)
    w = jnp.einsum("khpjde,gG->kpjgdhGe", w, jnp.eye(NSA_GROUPS, dtype=F32))
    w = w.reshape(2, pairs, 2 * NSA_GROUPS * NSA_DH, 2 * NSA_GROUPS * NSA_DH)
    pe = cmp_pe.reshape(2, 2, pairs, 2, 1, NSA_DH)
    pe = jnp.broadcast_to(pe, (2, 2, pairs, 2, NSA_GROUPS, NSA_DH)).transpose(0, 2, 1, 3, 4, 5)
    pe = pe.reshape(2, pairs, 2, 2 * NSA_GROUPS * NSA_DH)
    pe = jnp.concatenate([pe, jnp.zeros((2, pairs, 6, pe.shape[-1]), F32)], axis=2)
    return w.astype(BF16), pe


def _compress_core(x, w_ref, pe_ref):
    return _compress_finish(_dot(x, w_ref[...]), w_ref, pe_ref)


def _compress_finish(p, w_ref, pe_ref):
    n = p.shape[0]
    w = w_ref[...]
    hi, mid, lo = _split3(pe_ref[...])
    pb = _dot(hi, w) + _dot(mid, w) + _dot(lo, w)
    bias = pb[0:1, :NSA_KV_ROW] + pb[1:2, NSA_KV_ROW:]
    return p[:, :NSA_KV_ROW] + pltpu.roll(p[:, NSA_KV_ROW:], n - 1, 0) + bias


def _compress_prompt_kernel(x_ref, w_ref, pe_ref, o_ref):
    c = _compress_core(x_ref[0].astype(BF16), w_ref, pe_ref)
    for j in range(2 * NSA_GROUPS):
        o_ref[0, j] = c[:, j * NSA_DH:(j + 1) * NSA_DH].astype(BF16)


def _compress_prompt(kvc, b, t, w, pe):
    n = t // CMP_STRIDE
    x = kvc.reshape(b, n, CMP_STRIDE * NSA_KV_ROW)
    return pl.pallas_call(
        _compress_prompt_kernel, grid=(b,),
        in_specs=[pl.BlockSpec((1, n, CMP_STRIDE * NSA_KV_ROW), lambda i: (i, 0, 0)),
                  pl.BlockSpec(w.shape, lambda i: (0, 0)), pl.BlockSpec(pe.shape, lambda i: (0, 0))],
        out_specs=pl.BlockSpec((1, 4, n, NSA_DH), lambda i: (i, 0, 0, 0)),
        out_shape=_sds((b, 4, n, NSA_DH), BF16),
        compiler_params=_params(("arbitrary",), 48), name="nsa_compress_prompt")(x, w, pe)


def _overlap_matrix(n_cmp, n_sel):
    i = jnp.arange(n_cmp)[:, None]
    j = jnp.arange(n_sel)[None, :]
    per = SEL_BLOCK // CMP_STRIDE
    return ((i // per == j).astype(F32) + ((i + 1) // per == j).astype(F32)).astype(BF16)


def _topk_mask(score, k):
    n = score.shape[-1]
    lane = lax.broadcasted_iota(I32, score.shape, score.ndim - 1).astype(F32)
    sel = jnp.zeros_like(score)
    for _ in range(k):
        m = jnp.max(score, axis=-1, keepdims=True)
        first = jnp.min(jnp.where(score == m, lane, float(n)), axis=-1, keepdims=True)
        pick = lane == first
        sel = jnp.where(pick, 1.0, sel)
        score = jnp.where(pick, -jnp.inf, score)
    return sel


def _masked_softmax(s, mask):
    s = jnp.where(mask, s, NEG)
    m = jnp.max(s, axis=-1, keepdims=True)
    p = jnp.where(mask, jnp.exp2(s - m), 0.0)
    return p / jnp.maximum(jnp.sum(p, axis=-1, keepdims=True), TINY)


def _cmp_branch(q, ck, cv, qpos, ovl, topk):
    tq = qpos.shape[0]
    n = ck.shape[0]
    n_sel = ovl.shape[1]
    cmp_end = CMP_STRIDE * lax.broadcasted_iota(I32, (tq, n), 1) + (CMP_BLOCK - 1)
    cmask = (cmp_end <= qpos)[None]
    s = _dot_t(q, ck).reshape(NSA_REP, tq, n)
    p = _masked_softmax(s, cmask)
    o = _dot(p.reshape(NSA_REP * tq, n).astype(BF16), cv)
    imp = jnp.sum(p, axis=0)
    hi, mid, lo = _split3(imp)
    sel_imp = _dot(hi, ovl) + _dot(mid, ovl) + _dot(lo, ovl)
    blk = lax.broadcasted_iota(I32, (tq, n_sel), 1)
    qblk = qpos // SEL_BLOCK
    forced = (blk == 0) | (blk == qblk) | (blk == qblk - 1)
    score = jnp.where(forced, sel_imp + FORCE_BONUS, jnp.where(blk <= qblk, sel_imp, NEG))
    return o, _topk_mask(score, topk)


def _nsa_cmp_prompt_kernel(q_ref, c_ref, ovl_ref, o_ref, sel_ref, *, tq):
    qpos = pl.program_id(1) * tq + lax.broadcasted_iota(I32, (tq, 1), 0)
    n_sel = ovl_ref.shape[1]
    for g in range(NSA_GROUPS):
        q = q_ref[g, :, 0].reshape(NSA_REP * tq, NSA_DH)
        o, sel = _cmp_branch(q, c_ref[0, g], c_ref[0, NSA_GROUPS + g], qpos, ovl_ref[...],
                             min(SEL_TOPK, n_sel))
        for r in range(NSA_REP):
            hd = g * NSA_REP + r
            o_ref[:, hd * NSA_DH:(hd + 1) * NSA_DH] = o[r * tq:(r + 1) * tq]
        sel_ref[:, g * n_sel:(g + 1) * n_sel] = sel.astype(BF16)


def _nsa_cmp_prompt(q, cmp, b, t, tq=256):
    n = t // CMP_STRIDE
    n_sel = t // SEL_BLOCK
    tq = min(tq, t)
    nq = t // tq
    ovl = _overlap_matrix(n, n_sel)
    q5 = q.reshape(NSA_GROUPS, NSA_REP, b, t, NSA_DH)
    return pl.pallas_call(
        functools.partial(_nsa_cmp_prompt_kernel, tq=tq), grid=(b, nq),
        in_specs=[pl.BlockSpec((NSA_GROUPS, NSA_REP, 1, tq, NSA_DH), lambda i, j: (0, 0, i, j, 0)),
                  pl.BlockSpec((1, 4, n, NSA_DH), lambda i, j: (i, 0, 0, 0)),
                  pl.BlockSpec(ovl.shape, lambda i, j: (0, 0))],
        out_specs=[pl.BlockSpec((tq, NSA_HEADS * NSA_DH), lambda i, j, nq=nq: (i * nq + j, 0)),
                   pl.BlockSpec((tq, NSA_GROUPS * n_sel), lambda i, j, nq=nq: (i * nq + j, 0))],
        out_shape=[_sds((b * t, NSA_HEADS * NSA_DH)), _sds((b * t, NSA_GROUPS * n_sel), BF16)],
        compiler_params=_params(("arbitrary", "arbitrary"), 48), name="nsa_cmp_prompt")(q5, cmp, ovl)


def _flash_update(q, k, v, mask, m_ref, l_ref, acc_ref):
    tq, tk = mask.shape
    rep = q.shape[0] // tq
    s = _dot_t(q, k).reshape(rep, tq, tk)
    s = jnp.where(mask[None], s, NEG)
    m_prev = m_ref[...]
    m_new = jnp.maximum(m_prev, jnp.max(s, axis=-1, keepdims=True))
    p = jnp.where(mask[None], jnp.exp(s - m_new), 0.0)
    alpha = jnp.exp(m_prev - m_new)
    l_ref[...] = alpha * l_ref[...] + jnp.sum(p, axis=-1, keepdims=True)
    pv = _dot(p.reshape(rep * tq, tk).astype(BF16), v)
    acc_ref[...] = alpha.reshape(rep * tq, 1) * acc_ref[...] + pv
    m_ref[...] = m_new


def _nsa_sw_prompt_kernel(q_ref, ks_ref, vs_ref, kw_ref, vw_ref, sel_ref, ocmp_ref, gate_ref,
                          o_ref, m_ref, l_ref, acc_ref, *, tq, tk):
    qi = pl.program_id(1)
    ki = pl.program_id(2)
    q0 = qi * tq
    last = (q0 + tq - 1) // tk
    first_w = jnp.maximum(q0 - (WINDOW - 1), 0) // tk
    n_sel = sel_ref.shape[1] // NSA_GROUPS
    per = tk // SEL_BLOCK

    @pl.when(ki == 0)
    def _():
        m_ref[...] = jnp.full(m_ref.shape, NEG, F32)
        l_ref[...] = jnp.zeros(l_ref.shape, F32)
        acc_ref[...] = jnp.zeros(acc_ref.shape, F32)

    qpos = q0 + lax.broadcasted_iota(I32, (tq, tk), 0)
    kpos = ki * tk + lax.broadcasted_iota(I32, (tq, tk), 1)
    causal = kpos <= qpos

    @pl.when(ki <= last)
    def _():
        expand = (lax.broadcasted_iota(I32, (n_sel, tk), 0)
                  == ki * per + lax.broadcasted_iota(I32, (n_sel, tk), 1) // SEL_BLOCK)
        expand = jnp.where(expand, 1.0, 0.0).astype(BF16)
        for g in range(NSA_GROUPS):
            q = q_ref[g, :, 0].reshape(NSA_REP * tq, NSA_DH)
            chosen = _dot(sel_ref[:, g * n_sel:(g + 1) * n_sel], expand)
            mask = jnp.where(causal, chosen, 0.0) > 0.5
            _flash_update(q, ks_ref[g, 0], vs_ref[g, 0], mask,
                          m_ref.at[g], l_ref.at[g], acc_ref.at[g])

    @pl.when((ki >= first_w) & (ki <= last))
    def _():
        mask = jnp.where(causal, qpos - kpos, WINDOW) < WINDOW
        for g in range(NSA_GROUPS):
            q = q_ref[g, :, 0].reshape(NSA_REP * tq, NSA_DH)
            _flash_update(q, kw_ref[g, 0], vw_ref[g, 0], mask,
                          m_ref.at[NSA_GROUPS + g], l_ref.at[NSA_GROUPS + g],
                          acc_ref.at[NSA_GROUPS + g])

    @pl.when(ki == last)
    def _():
        gates = gate_ref[...]
        ocmp = ocmp_ref[...]
        for g in range(NSA_GROUPS):
            o_s = acc_ref[g] / jnp.maximum(l_ref[g].reshape(NSA_REP * tq, 1), TINY)
            o_w = acc_ref[NSA_GROUPS + g] / jnp.maximum(
                l_ref[NSA_GROUPS + g].reshape(NSA_REP * tq, 1), TINY)
            for r in range(NSA_REP):
                hd = g * NSA_REP + r
                rows = slice(r * tq, (r + 1) * tq)
                cols = slice(hd * NSA_DH, (hd + 1) * NSA_DH)
                o_ref[:, cols] = (gates[:, 3 * hd:3 * hd + 1] * ocmp[:, cols]
                                  + gates[:, 3 * hd + 1:3 * hd + 2] * o_s[rows]
                                  + gates[:, 3 * hd + 2:3 * hd + 3] * o_w[rows])


def _nsa_sw_prompt(q, kvb, sel, o_cmp, gates, b, t, tq=256, tk=512):
    tq = min(tq, t)
    tk = min(tk, t)
    nq, nk = t // tq, t // tk
    q5 = q.reshape(NSA_GROUPS, NSA_REP, b, t, NSA_DH)
    kv4 = kvb.reshape(12, b, t, NSA_DH)

    def last_of(j):
        return (j * tq + tq - 1) // tk

    def slc_idx(blk):
        return lambda i, j, k: (blk, i, jnp.minimum(k, last_of(j)), 0)

    def win_idx(blk):
        def f(i, j, k):
            first = jnp.maximum(j * tq - (WINDOW - 1), 0) // tk
            return (blk, i, jnp.clip(k, first, last_of(j)), 0)
        return f

    kv_blk = (NSA_GROUPS, 1, tk, NSA_DH)
    row = lambda i, j, k: (i * nq + j, 0)
    return pl.pallas_call(
        functools.partial(_nsa_sw_prompt_kernel, tq=tq, tk=tk), grid=(b, nq, nk),
        in_specs=[pl.BlockSpec((NSA_GROUPS, NSA_REP, 1, tq, NSA_DH), lambda i, j, k: (0, 0, i, j, 0)),
                  pl.BlockSpec(kv_blk, slc_idx(2)), pl.BlockSpec(kv_blk, slc_idx(3)),
                  pl.BlockSpec(kv_blk, win_idx(4)), pl.BlockSpec(kv_blk, win_idx(5)),
                  pl.BlockSpec((tq, sel.shape[1]), row),
                  pl.BlockSpec((tq, o_cmp.shape[1]), row),
                  pl.BlockSpec((tq, gates.shape[1]), row)],
        out_specs=pl.BlockSpec((tq, NSA_HEADS * NSA_DH), row),
        out_shape=_sds((b * t, NSA_HEADS * NSA_DH)),
        scratch_shapes=[pltpu.VMEM((2 * NSA_GROUPS, NSA_REP, tq, 1), F32),
                        pltpu.VMEM((2 * NSA_GROUPS, NSA_REP, tq, 1), F32),
                        pltpu.VMEM((2 * NSA_GROUPS, NSA_REP * tq, NSA_DH), F32)],
        compiler_params=_params(("arbitrary", "arbitrary", "arbitrary"), 48),
        name="nsa_slc_win_prompt")(q5, kv4, kv4, kv4, kv4, sel, o_cmp, gates)


def _flash_update_t(q, k, v, mask_t, m_ref, acc_ref):
    s = _dot_t(k, q)
    if mask_t is not None:
        s = jnp.where(mask_t, s, NEG)
    m_prev = m_ref[...]
    m_new = jnp.maximum(m_prev, jnp.max(s, axis=0, keepdims=True))
    p = jnp.exp2(s - m_new)
    if mask_t is not None:
        p = jnp.where(mask_t, p, 0.0)
    alpha = jnp.exp2(m_prev - m_new)
    v1 = jnp.concatenate([v, jnp.ones_like(v)], axis=1)
    acc_ref[...] = alpha * acc_ref[...] + _tdot(v1, p.astype(BF16))
    m_ref[...] = m_new


def _flash_result_t(acc):
    dh = acc.shape[0] // 2
    return acc[:dh] / jnp.maximum(acc[dh:dh + 1], TINY)


def _causal_steps(nq, tq, tk):
    qi, ki = [], []
    for j in range(nq):
        for k in range((j * tq + tq - 1) // tk + 1):
            qi.append(j)
            ki.append(k)
    return jnp.asarray(qi, I32), jnp.asarray(ki, I32)


def _nsa_swt_prompt_kernel(qi_ref, ki_ref, q_ref, ks_ref, vs_ref, kw_ref, vw_ref, sel_ref, ocmp_ref,
                           gate_ref, o_ref, m_ref, acc_ref, *, tq, tk):
    n = pl.program_id(1)
    qi = qi_ref[n]
    ki = ki_ref[n]
    q0 = qi * tq
    last = (q0 + tq - 1) // tk
    first_w = jnp.maximum(q0 - (WINDOW - 1), 0) // tk
    n_sel = sel_ref.shape[1] // NSA_GROUPS
    n_q = NSA_REP * tq

    @pl.when(ki == 0)
    def _():
        m_ref[...] = jnp.full(m_ref.shape, NEG, F32)
        acc_ref[...] = jnp.zeros(acc_ref.shape, F32)

    kpos = ki * tk + lax.broadcasted_iota(I32, (tk, tq), 0)
    qpos = q0 + lax.broadcasted_iota(I32, (tk, tq), 1)
    causal = kpos <= qpos
    heads = lambda mask: jnp.concatenate([mask] * NSA_REP, axis=1)

    expand = (lax.broadcasted_iota(I32, (tk, n_sel), 1)
              == (ki * tk + lax.broadcasted_iota(I32, (tk, n_sel), 0)) // SEL_BLOCK)
    expand = jnp.where(expand, 1.0, 0.0).astype(BF16)
    for g in range(NSA_GROUPS):
        q = q_ref[g, :, 0].reshape(n_q, NSA_DH)
        chosen = _dot_t(expand, sel_ref[:, g * n_sel:(g + 1) * n_sel])
        mask = jnp.where(causal, chosen, 0.0) > 0.5
        _flash_update_t(q, ks_ref[g, 0], vs_ref[g, 0], heads(mask), m_ref.at[g], acc_ref.at[g])

    @pl.when(ki >= first_w)
    def _():
        mask = heads(jnp.where(causal, qpos - kpos, WINDOW) < WINDOW)
        for g in range(NSA_GROUPS):
            q = q_ref[g, :, 0].reshape(n_q, NSA_DH)
            _flash_update_t(q, kw_ref[g, 0], vw_ref[g, 0], mask, m_ref.at[NSA_GROUPS + g],
                            acc_ref.at[NSA_GROUPS + g])

    @pl.when(ki == last)
    def _():
        gates = gate_ref[...]
        ocmp = ocmp_ref[...]
        for g in range(NSA_GROUPS):
            o_s = _flash_result_t(acc_ref[g])
            o_w = _flash_result_t(acc_ref[NSA_GROUPS + g])
            for r in range(NSA_REP):
                hd = g * NSA_REP + r
                lanes = slice(r * tq, (r + 1) * tq)
                cols = slice(hd * NSA_DH, (hd + 1) * NSA_DH)
                o_ref[:, cols] = (gates[:, 3 * hd:3 * hd + 1] * ocmp[:, cols]
                                  + gates[:, 3 * hd + 1:3 * hd + 2] * o_s[:, lanes].T
                                  + gates[:, 3 * hd + 2:3 * hd + 3] * o_w[:, lanes].T)


def _nsa_swt_prompt(q, kvb, sel, o_cmp, gates, b, t, tq=256, tk=512):
    tq = min(tq, t)
    tk = min(tk, t)
    nq = t // tq
    qi, ki = _causal_steps(nq, tq, tk)
    q5 = q.reshape(NSA_GROUPS, NSA_REP, b, t, NSA_DH)
    kv4 = kvb.reshape(12, b, t, NSA_DH)

    def slc_idx(blk):
        return lambda i, n, qt, kt: (blk, i, kt[n], 0)

    def win_idx(blk):
        def f(i, n, qt, kt):
            first = jnp.maximum(qt[n] * tq - (WINDOW - 1), 0) // tk
            return (blk, i, jnp.maximum(kt[n], first), 0)
        return f

    kv_blk = (NSA_GROUPS, 1, tk, NSA_DH)
    row = lambda i, n, qt, kt: (i * nq + qt[n], 0)
    n_q = NSA_REP * tq
    grid_spec = pltpu.PrefetchScalarGridSpec(
        num_scalar_prefetch=2, grid=(b, qi.shape[0]),
        in_specs=[pl.BlockSpec((NSA_GROUPS, NSA_REP, 1, tq, NSA_DH),
                               lambda i, n, qt, kt: (0, 0, i, qt[n], 0)),
                  pl.BlockSpec(kv_blk, slc_idx(2)), pl.BlockSpec(kv_blk, slc_idx(3)),
                  pl.BlockSpec(kv_blk, win_idx(4)), pl.BlockSpec(kv_blk, win_idx(5)),
                  pl.BlockSpec((tq, sel.shape[1]), row),
                  pl.BlockSpec((tq, o_cmp.shape[1]), row),
                  pl.BlockSpec((tq, gates.shape[1]), row)],
        out_specs=pl.BlockSpec((tq, NSA_HEADS * NSA_DH), row),
        scratch_shapes=[pltpu.VMEM((2 * NSA_GROUPS, 1, n_q), F32),
                        pltpu.VMEM((2 * NSA_GROUPS, 2 * NSA_DH, n_q), F32)])
    return pl.pallas_call(
        functools.partial(_nsa_swt_prompt_kernel, tq=tq, tk=tk), grid_spec=grid_spec,
        out_shape=_sds((b * t, NSA_HEADS * NSA_DH)),
        compiler_params=_params(("arbitrary", "arbitrary"), 48),
        name="nsa_slc_win_prompt")(qi, ki, q5, kv4, kv4, kv4, kv4, sel, o_cmp, gates)


def _conv_prompt_kernel(u_ref, gb_ref, w_ref, o_ref, carry_ref):
    tm = u_ref.shape[0]

    @pl.when(pl.program_id(1) == 0)
    def _():
        carry_ref[...] = jnp.zeros(carry_ref.shape, F32)

    u = u_ref[...]
    prev = carry_ref[...]
    row = lax.broadcasted_iota(I32, u.shape, 0)
    u1 = jnp.where(row == 0, prev[7:8], pltpu.roll(u, 1, 0))
    u2 = jnp.where(row == 0, prev[6:7], jnp.where(row == 1, prev[7:8], pltpu.roll(u, 2, 0)))
    w = w_ref[...]
    o_ref[...] = gb_ref[...] * (w[0:1] * u2 + w[1:2] * u1 + w[2:3] * u)
    carry_ref[...] = u[tm - 8:]


def _conv_prompt(u, gb, conv_w, b, t, tm=1024):
    tm = min(tm, t)
    nt = t // tm
    c = u.shape[1]
    row = lambda i, j: (i * nt + j, 0)
    return pl.pallas_call(
        _conv_prompt_kernel, grid=(b, nt),
        in_specs=[pl.BlockSpec((tm, c), row), pl.BlockSpec((tm, c), row),
                  pl.BlockSpec(conv_w.shape, lambda i, j: (0, 0))],
        out_specs=pl.BlockSpec((tm, c), row), out_shape=_sds(u.shape),
        scratch_shapes=[pltpu.VMEM((8, c), F32)],
        compiler_params=_params(("arbitrary", "arbitrary"), 32), name="conv_prompt")(u, gb, conv_w)


def _matmul_res_body(n_a, rows, consts, outs):
    acc = rows[n_a][...]
    for k in range(n_a):
        acc = acc + _dot(rows[k][...].astype(BF16), consts[k][...])
    outs[0][...] = acc


def _matmul_res(acts, weights, res, name):
    r = res.shape[0]
    tm = _row_tile(r, 512)
    ws = [w.astype(BF16) for w in weights]
    return _rowwise(functools.partial(_matmul_res_body, len(acts)), list(acts) + [res], ws,
                    [_sds(res.shape)], tm, name)[0]


def _even_layer_prompt(x, b, t, g_mix, w_in, w_out, cmp_w, cmp_pe, conv_w):
    q, kvc, kvs, kvw, kvb, gates, gb, u = _even_proj(x, g_mix, w_in)
    cw, cpe = _compress_weights(cmp_w, cmp_pe)
    cmp = _compress_prompt(kvc, b, t, cw, cpe)
    o_cmp, sel = _nsa_cmp_prompt(q, cmp, b, t)
    o_nsa = _nsa_swt_prompt(q, kvb, sel, o_cmp, gates, b, t)
    o_conv = _conv_prompt(u, gb, conv_w, b, t)
    qc = NSA_HEADS * NSA_DH
    y = _matmul_res([o_nsa, o_conv], [w_out[:qc], w_out[qc:]], x, "even_out_proj")
    wb = min(WINDOW, t)
    caches = (kvc.reshape(b, t, 2, NSA_GROUPS, NSA_DH), kvs.reshape(b, t, 2, NSA_GROUPS, NSA_DH),
              kvw.reshape(b, t, 2, NSA_GROUPS, NSA_DH)[:, t - wb:],
              u.reshape(b, t, CONV_CH)[:, t - (CONV_W - 1):])
    return y, caches


def _channel_major(pool):
    nd = pool.ndim
    t = pool.transpose((0, 1) + tuple(range(3, nd)) + (2,))
    return t.reshape(pool.shape[0], pool.shape[1], -1, pool.shape[2])


def _page_copies(pool_ref, layer, pt_ref, batch, n_pages, buf_ref, slot, sem_ref):
    return [pltpu.make_async_copy(pool_ref.at[layer, pt_ref[batch * n_pages + p]],
                                  buf_ref.at[slot, :, pl.ds(p * PAGE_ROWS, PAGE_ROWS)],
                                  sem_ref.at[slot])
            for p in range(n_pages)]


def _paged_fetch(pool_ref, layer, pt_ref, n_pages, buf_ref, sem_ref):
    b = pl.program_id(0)
    slot = b % 2

    @pl.when(b == 0)
    def _():
        for c in _page_copies(pool_ref, layer, pt_ref, b, n_pages, buf_ref, slot, sem_ref):
            c.start()

    @pl.when(b + 1 < pl.num_programs(0))
    def _():
        for c in _page_copies(pool_ref, layer, pt_ref, b + 1, n_pages, buf_ref, 1 - slot, sem_ref):
            c.start()

    for c in _page_copies(pool_ref, layer, pt_ref, b, n_pages, buf_ref, slot, sem_ref):
        c.wait()
    return slot


def _online(state, s, mask, v, v_t=False):
    m_prev, l_prev, acc = state
    if mask is not None:
        s = jnp.where(mask, s, NEG)
    m_new = jnp.maximum(m_prev, jnp.max(s, axis=-1, keepdims=True))
    p = jnp.exp2(s - m_new)
    if mask is not None:
        p = jnp.where(mask, p, 0.0)
    alpha = jnp.exp2(m_prev - m_new)
    return (m_new, alpha * l_prev + jnp.sum(p, axis=-1, keepdims=True),
            alpha * acc + (_dot_t(p.astype(BF16), v) if v_t else _dot(p.astype(BF16), v)))


def _online_init(m, w):
    return (jnp.full((m, 1), NEG, F32), jnp.zeros((m, 1), F32), jnp.zeros((m, w), F32))


def _group_query_rows(q_ref, s):
    blocks = []
    for g in range(NSA_GROUPS):
        q = q_ref[g, :, 0].astype(F32).reshape(NSA_REP * s, NSA_DH)
        z = jnp.zeros_like(q)
        blocks.append(jnp.concatenate([q if j == g else z for j in range(2 * NSA_GROUPS)], axis=1))
    return jnp.concatenate(blocks, axis=0).astype(BF16)


def _group_rows(x):
    return jnp.concatenate([jnp.tile(xg, (NSA_REP, 1)) for xg in x], axis=0)


def _nsa_cmp_sample_kernel(pt_ref, pool_ref, q_ref, w_ref, pe_ref, ovl_ref, o_ref, sel_ref,
                           buf_ref, sem_ref, x_ref, bias_ref, *, layer, n_pages, s, topk):
    half = NSA_GROUPS * NSA_DH

    @pl.when(pl.program_id(0) == 0)
    def _():
        for kv in range(2):
            pb = None
            for jp in range(w_ref.shape[1]):
                for term in _split3(pe_ref[kv, jp]):
                    t = _dot(term, w_ref[kv, jp])
                    pb = t if pb is None else pb + t
            bias_ref[kv] = jnp.broadcast_to(pb[0:1, :half] + pb[1:2, half:], bias_ref.shape[1:])

    slot = _paged_fetch(pool_ref, layer, pt_ref, n_pages, buf_ref, sem_ref)
    past = n_pages * PAGE_ROWS
    n = past // CMP_STRIDE
    step = min(1024, past)
    for c0 in range(0, past, step):
        for kv in range(2):
            x_ref[kv, c0:c0 + step, :] = buf_ref[slot, kv * half:(kv + 1) * half, c0:c0 + step].T
    c = []
    for kv in range(2):
        p = None
        for jp in range(w_ref.shape[1]):
            x = jnp.concatenate(
                [x_ref.at[kv][pl.ds(2 * jp + jj, n, stride=CMP_STRIDE), :] for jj in range(2)],
                axis=1).astype(BF16)
            t = _dot(x, w_ref[kv, jp])
            p = t if p is None else p + t
        c.append(p[:, :half] + pltpu.roll(p[:, half:], n - 1, 0) + bias_ref[kv, 0:1])
    n_sel = ovl_ref.shape[1]
    qpos = n_pages * PAGE_ROWS + lax.broadcasted_iota(I32, (s, 1), 0)
    for g in range(NSA_GROUPS):
        q = q_ref[g, :, 0].astype(F32).reshape(NSA_REP * s, NSA_DH).astype(BF16)
        ck = c[0][:, g * NSA_DH:(g + 1) * NSA_DH].astype(BF16)
        cv = c[1][:, g * NSA_DH:(g + 1) * NSA_DH].astype(BF16)
        o, sel = _cmp_branch(q, ck, cv, qpos, ovl_ref[...], topk)
        for r in range(NSA_REP):
            hd = g * NSA_REP + r
            o_ref[0, :, hd * NSA_DH:(hd + 1) * NSA_DH] = o[r * s:(r + 1) * s]
        sel_ref[0, :, g * n_sel:(g + 1) * n_sel] = sel.astype(BF16)


PAGE_ROWS = 128
PAGE_CHUNKS = PAGE_ROWS // CMP_STRIDE


def _nsa_cmp_sample(q, pool, layer, page_table, w, pe, sb, s):
    n_pages = page_table.shape[1]
    past = n_pages * PAGE_ROWS
    assert past % SEL_BLOCK == 0 and s <= CMP_STRIDE and pool.shape[2] == PAGE_ROWS
    n = past // CMP_STRIDE
    n_sel = past // SEL_BLOCK
    ovl = _overlap_matrix(n, n_sel)
    pool4 = _channel_major(pool)
    q5 = q.reshape(NSA_GROUPS, NSA_REP, sb, s, NSA_DH)
    kern = functools.partial(_nsa_cmp_sample_kernel, layer=layer, n_pages=n_pages, s=s,
                             topk=SEL_TOPK - 1)
    grid_spec = pltpu.PrefetchScalarGridSpec(
        num_scalar_prefetch=1, grid=(sb,),
        in_specs=[pl.BlockSpec(memory_space=pl.ANY),
                  pl.BlockSpec((NSA_GROUPS, NSA_REP, 1, s, NSA_DH), lambda i, pt: (0, 0, i, 0, 0)),
                  pl.BlockSpec(w.shape, lambda i, pt: (0, 0, 0, 0)),
                  pl.BlockSpec(pe.shape, lambda i, pt: (0, 0, 0, 0)),
                  pl.BlockSpec(ovl.shape, lambda i, pt: (0, 0))],
        out_specs=[pl.BlockSpec((1, s, NSA_HEADS * NSA_DH), lambda i, pt: (i, 0, 0)),
                   pl.BlockSpec((1, s, NSA_GROUPS * n_sel), lambda i, pt: (i, 0, 0))],
        scratch_shapes=[pltpu.VMEM((2, NSA_KV_ROW, past), F32), pltpu.SemaphoreType.DMA((2,)),
                        pltpu.VMEM((2, past, NSA_GROUPS * NSA_DH), F32),
                        pltpu.VMEM((2, 8, NSA_GROUPS * NSA_DH), F32)])
    return pl.pallas_call(
        kern, grid_spec=grid_spec,
        out_shape=[_sds((sb, s, NSA_HEADS * NSA_DH)), _sds((sb, s, NSA_GROUPS * n_sel), BF16)],
        compiler_params=_params(("arbitrary",), 56), name="nsa_cmp_sample")(
            page_table.reshape(-1), pool4, q5, w, pe, ovl)


def _nsa_sw_sample_kernel(pt_ref, pool_ref, q_ref, knew_ref, wnew_ref, win_ref, sel_ref, ocmp_ref,
                          gate_ref, o_ref, buf_ref, sem_ref, *, layer, n_pages, s, chunk):
    slot = _paged_fetch(pool_ref, layer, pt_ref, n_pages, buf_ref, sem_ref)
    past = n_pages * PAGE_ROWS
    n_sel = sel_ref.shape[2] // NSA_GROUPS
    m = NSA_HEADS * s
    vrow = NSA_KV_ROW // 2
    qm = _group_query_rows(q_ref, s)
    sq = lax.broadcasted_iota(I32, (s, s), 0)
    sk = lax.broadcasted_iota(I32, (s, s), 1)
    new_mask = _group_rows([sk <= sq] * NSA_GROUPS)

    state = _online_init(m, vrow)
    for c0 in range(0, past, chunk):
        kv = buf_ref[slot, :, c0:c0 + chunk].astype(BF16)
        expand = (lax.broadcasted_iota(I32, (n_sel, chunk), 0)
                  == (c0 + lax.broadcasted_iota(I32, (n_sel, chunk), 1)) // SEL_BLOCK)
        expand = jnp.where(expand, 1.0, 0.0).astype(BF16)
        chosen = [_dot(sel_ref[0, :, g * n_sel:(g + 1) * n_sel], expand) > 0.5
                  for g in range(NSA_GROUPS)]
        state = _online(state, _dot(qm, kv), _group_rows(chosen), kv[vrow:], v_t=True)
    knew = knew_ref[0].astype(BF16)
    state = _online(state, _dot_t(qm, knew), new_mask, knew[:, vrow:])
    o_slc = state[2] / jnp.maximum(state[1], TINY)

    wb = win_ref.shape[2]
    win = win_ref[0].astype(BF16)
    qrow = lax.broadcasted_iota(I32, (s, wb), 0)
    dist = qrow + wb - lax.broadcasted_iota(I32, (s, wb), 1)
    old_mask = _group_rows([(dist >= 0) & (dist < WINDOW)] * NSA_GROUPS)
    state = _online(_online_init(m, vrow), _dot(qm, win), old_mask, win[vrow:], v_t=True)
    wnew = wnew_ref[0].astype(BF16)
    state = _online(state, _dot_t(qm, wnew), new_mask, wnew[:, vrow:])
    o_win = state[2] / jnp.maximum(state[1], TINY)

    gates = gate_ref[0]
    ocmp = ocmp_ref[0]
    for g in range(NSA_GROUPS):
        vcols = slice(g * NSA_DH, (g + 1) * NSA_DH)
        for r in range(NSA_REP):
            hd = g * NSA_REP + r
            rows = slice(hd * s, (hd + 1) * s)
            cols = slice(hd * NSA_DH, (hd + 1) * NSA_DH)
            o_ref[0, :, cols] = (gates[:, 3 * hd:3 * hd + 1] * ocmp[:, cols]
                                 + gates[:, 3 * hd + 1:3 * hd + 2] * o_slc[rows, vcols]
                                 + gates[:, 3 * hd + 2:3 * hd + 3] * o_win[rows, vcols])


def _nsa_sw_sample(q, pool, layer, page_table, kvs_new, kvw_new, win_buf, sel, o_cmp, gates, sb, s):
    n_pages = page_table.shape[1]
    past = n_pages * PAGE_ROWS
    wb = win_buf.shape[1]
    assert wb == WINDOW and s <= SEL_BLOCK and past % SEL_BLOCK == 0
    pool4 = _channel_major(pool)
    q5 = q.reshape(NSA_GROUPS, NSA_REP, sb, s, NSA_DH)
    per_b = lambda a: a.reshape(sb, s, a.shape[-1])
    blk = lambda a: pl.BlockSpec((1,) + a.shape[1:], lambda i, pt: (i, 0, 0))
    win_t = _channel_major(win_buf[None])[0]
    rows = [per_b(kvs_new), per_b(kvw_new), win_t, sel, o_cmp, per_b(gates)]
    kern = functools.partial(_nsa_sw_sample_kernel, layer=layer, n_pages=n_pages, s=s,
                             chunk=min(2048, past))
    grid_spec = pltpu.PrefetchScalarGridSpec(
        num_scalar_prefetch=1, grid=(sb,),
        in_specs=[pl.BlockSpec(memory_space=pl.ANY),
                  pl.BlockSpec((NSA_GROUPS, NSA_REP, 1, s, NSA_DH), lambda i, pt: (0, 0, i, 0, 0))]
        + [blk(a) for a in rows],
        out_specs=pl.BlockSpec((1, s, NSA_HEADS * NSA_DH), lambda i, pt: (i, 0, 0)),
        scratch_shapes=[pltpu.VMEM((2, NSA_KV_ROW, past), F32), pltpu.SemaphoreType.DMA((2,))])
    return pl.pallas_call(
        kern, grid_spec=grid_spec, out_shape=_sds((sb, s, NSA_HEADS * NSA_DH)),
        compiler_params=_params(("arbitrary",), 56), name="nsa_slc_win_sample")(
            page_table.reshape(-1), pool4, q5, *rows)


def _conv_sample_kernel(u_ref, gb_ref, cb_ref, w_ref, o_ref):
    u = u_ref[...]
    cb = cb_ref[...]
    row = lax.broadcasted_iota(I32, u.shape, 1)
    u1 = jnp.where(row == 0, cb[:, 1:2], pltpu.roll(u, 1, 1))
    u2 = jnp.where(row == 0, cb[:, 0:1], jnp.where(row == 1, cb[:, 1:2], pltpu.roll(u, 2, 1)))
    w = w_ref[...]
    o_ref[...] = gb_ref[...] * (w[0:1][None] * u2 + w[1:2][None] * u1 + w[2:3][None] * u)


def _conv_sample(u, gb, conv_buf, conv_w, sb, s):
    assert s >= CONV_W - 1
    c = u.shape[1]
    full = lambda shape: pl.BlockSpec(shape, lambda i, n=len(shape): (0,) * n)
    return pl.pallas_call(
        _conv_sample_kernel, grid=(1,),
        in_specs=[full((sb, s, c)), full((sb, s, c)), full(conv_buf.shape), full(conv_w.shape)],
        out_specs=full((sb, s, c)), out_shape=_sds((sb, s, c)),
        compiler_params=_params(("arbitrary",), 32), name="conv_sample")(
            u.reshape(sb, s, c), gb.reshape(sb, s, c), conv_buf, conv_w).reshape(sb * s, c)


def _even_layer_sample(x, sb, s, layer, pool_cmp, pool_slc, win_buf, conv_buf, page_table,
                       g_mix, w_in, w_out, cmp_w, cmp_pe, conv_w):
    q, kvc, kvs, kvw, _, gates, gb, u = _even_proj(x, g_mix, w_in)
    cw, cpe = _compress_pair_weights(cmp_w, cmp_pe)
    o_cmp, sel = _nsa_cmp_sample(q, pool_cmp, layer, page_table, cw, cpe, sb, s)
    o_nsa = _nsa_sw_sample(q, pool_slc, layer, page_table, kvs, kvw, win_buf, sel, o_cmp, gates,
                           sb, s)
    o_conv = _conv_sample(u, gb, conv_buf, conv_w, sb, s)
    qc = NSA_HEADS * NSA_DH
    y = _matmul_res([o_nsa.reshape(sb * s, qc), o_conv], [w_out[:qc], w_out[qc:]], x,
                    "even_out_proj")
    five = lambda a: a.reshape(sb, s, 2, NSA_GROUPS, NSA_DH)
    win_all = jnp.concatenate([win_buf, five(kvw)], axis=1)[:, s:]
    conv_all = jnp.concatenate([conv_buf, u.reshape(sb, s, CONV_CH)], axis=1)[:, s:]
    return y, (five(kvc), five(kvs), win_all, conv_all)


def _rot_half_cols(w):
    half = w.shape[-1] // 2
    return jnp.concatenate([-w[..., half:], w[..., :half]], axis=-1)


def _rope_tables(pos):
    half = ROPE // 2
    inv = ROPE_THETA ** (-jnp.arange(half, dtype=F32) / half)
    ang = pos.astype(F32)[:, None] * inv[None, :]
    cos, sin = jnp.cos(ang), jnp.sin(ang)
    cos32 = jnp.concatenate([cos, cos], axis=1)
    sin32 = jnp.concatenate([sin, sin], axis=1)
    n = pos.shape[0]
    pad = MLA_QK - NOPE - ROPE
    cosq = jnp.concatenate([jnp.ones((n, NOPE), F32), cos32, jnp.zeros((n, pad), F32)], axis=1)
    sinq = jnp.concatenate([jnp.zeros((n, NOPE), F32), sin32, jnp.zeros((n, pad), F32)], axis=1)
    return cos32, sin32, cosq, sinq


def _mla_weights(w_in, w_uq, w_uk):
    wcq = w_in[:, :Q_LORA]
    wckv = w_in[:, Q_LORA:Q_LORA + KV_LORA]
    wkpe = w_in[:, Q_LORA + KV_LORA:]
    uq = w_uq.reshape(Q_LORA, MLA_HEADS, NOPE + ROPE)
    pad = jnp.zeros((Q_LORA, MLA_HEADS, MLA_QK - NOPE - ROPE), F32)
    wq = jnp.concatenate([uq, pad], axis=-1).reshape(Q_LORA, MLA_HEADS * MLA_QK)
    wq_rot = jnp.concatenate([jnp.zeros((Q_LORA, MLA_HEADS, NOPE), F32),
                              _rot_half_cols(uq[..., NOPE:]), pad], axis=-1).reshape(wq.shape)
    kpad = jnp.zeros((KV_LORA, MLA_HEADS, MLA_QK - NOPE), F32)
    wk_lat = jnp.concatenate([w_uk, kpad], axis=-1).reshape(KV_LORA, MLA_HEADS * MLA_QK)
    eye = jnp.broadcast_to(jnp.eye(ROPE, dtype=F32)[:, None, :], (ROPE, MLA_HEADS, ROPE))
    wk_pe = jnp.concatenate([jnp.zeros((ROPE, MLA_HEADS, NOPE), F32), eye,
                             jnp.zeros((ROPE, MLA_HEADS, MLA_QK - NOPE - ROPE), F32)],
                            axis=-1).reshape(ROPE, MLA_HEADS * MLA_QK)
    bf = lambda a: a.astype(BF16)
    return dict(wcq=bf(wcq), wckv=bf(wckv), wkpe=bf(wkpe), wkpe_rot=bf(_rot_half_cols(wkpe)),
                wq=bf(wq), wq_rot=bf(wq_rot), wk_lat=bf(wk_lat), wk_pe=bf(wk_pe))


def _mla_latent(x_ref, cos_ref, sin_ref, g_ref, qn_ref, kvn_ref, wcq_ref, wckv_ref, wkpe_ref,
                wkrot_ref, lat_ref):
    h = _rms(x_ref[...], g_ref[...]).astype(BF16)
    cq = _rms(_dot(h, wcq_ref[...]), qn_ref[...]).astype(BF16)
    ckv = _rms(_dot(h, wckv_ref[...]), kvn_ref[...])
    kpe = _dot(h, wkpe_ref[...]) * cos_ref[...] + _dot(h, wkrot_ref[...]) * sin_ref[...]
    lat_ref[:, :KV_LORA] = ckv
    lat_ref[:, KV_LORA:] = kpe
    return cq, ckv.astype(BF16), kpe.astype(BF16)


def _mla_queries(cq, wq_ref, wqrot_ref, cosq, sinq, head):
    cols = slice(head * MLA_QK, (head + 1) * MLA_QK)
    q = _dot(cq, wq_ref[:, cols]) * cosq + _dot(cq, wqrot_ref[:, cols]) * sinq
    return q * ((NOPE + ROPE) ** -0.5 * LOG2E)


def _mla_proj_prompt_body(rows, consts, outs):
    x_ref, cos_ref, sin_ref, cosq_ref, sinq_ref = rows
    (g_ref, qn_ref, kvn_ref, wcq_ref, wckv_ref, wkpe_ref, wkrot_ref, wq_ref, wqrot_ref,
     wklat_ref, wkpe2_ref, wuv_ref) = consts
    lat_ref, q_ref, k_ref, v_ref = outs
    cq, ckv, kpe = _mla_latent(x_ref, cos_ref, sin_ref, g_ref, qn_ref, kvn_ref, wcq_ref,
                               wckv_ref, wkpe_ref, wkrot_ref, lat_ref)
    cosq, sinq = cosq_ref[...], sinq_ref[...]
    for hd in range(MLA_HEADS):
        cols = slice(hd * MLA_QK, (hd + 1) * MLA_QK)
        q_ref[:, cols] = _mla_queries(cq, wq_ref, wqrot_ref, cosq, sinq, hd).astype(BF16)
    k_ref[...] = (_dot(ckv, wklat_ref[...]) + _dot(kpe, wkpe2_ref[...])).astype(BF16)
    v_ref[...] = _dot(ckv, wuv_ref[...]).astype(BF16)


def _mla_proj_sample_body(rows, consts, outs):
    x_ref, cos_ref, sin_ref, cosq_ref, sinq_ref = rows
    (g_ref, qn_ref, kvn_ref, wcq_ref, wckv_ref, wkpe_ref, wkrot_ref, wq_ref, wqrot_ref,
     wukt_ref) = consts
    lat_ref, q_ref = outs
    cq, _, _ = _mla_latent(x_ref, cos_ref, sin_ref, g_ref, qn_ref, kvn_ref, wcq_ref, wckv_ref,
                           wkpe_ref, wkrot_ref, lat_ref)
    cosq, sinq = cosq_ref[...], sinq_ref[...]
    for hd in range(MLA_HEADS):
        q = _mla_queries(cq, wq_ref, wqrot_ref, cosq, sinq, hd)
        q_ref[hd, :, :KV_LORA] = _dot(q[:, :NOPE].astype(BF16), wukt_ref[hd])
        q_ref[hd, :, KV_LORA:] = q[:, NOPE:NOPE + ROPE]


def _mla_proj(x, pos, g_mix, q_norm, kv_norm, mw, extra, body, outs, name):
    r, d = x.shape
    tm = _row_tile(pos.shape[0], 512)
    cos32, sin32, cosq, sinq = _rope_tables(pos)
    n_per = pos.shape[0] // tm
    periods = {k: n_per for k in (1, 2, 3, 4)}
    consts = [g_mix.reshape(1, d), q_norm.reshape(1, -1), kv_norm.reshape(1, -1), mw["wcq"],
              mw["wckv"], mw["wkpe"], mw["wkpe_rot"], mw["wq"], mw["wq_rot"]] + extra
    return _rowwise(body, [x, cos32, sin32, cosq, sinq], consts, outs, tm, name, periods=periods)


def _mla_flash_kernel(qi_ref, ki_ref, q_ref, k_ref, v_ref, o_ref, m_ref, l_ref, acc_ref, *, tq, tk):
    n = pl.program_id(2)
    qi = qi_ref[n]
    ki = ki_ref[n]
    last = (qi * tq + tq - 1) // tk

    @pl.when(ki == 0)
    def _():
        m_ref[...] = jnp.full(m_ref.shape, NEG, F32)
        l_ref[...] = jnp.zeros(l_ref.shape, F32)
        acc_ref[...] = jnp.zeros(acc_ref.shape, F32)

    def step(masked):
        if masked:
            mask = (ki * tk + lax.broadcasted_iota(I32, (tq, tk), 1)
                    <= qi * tq + lax.broadcasted_iota(I32, (tq, tk), 0))
        for j in range(2):
            s = _dot_t(q_ref[:, j * MLA_QK:(j + 1) * MLA_QK], k_ref[:, j * MLA_QK:(j + 1) * MLA_QK])
            if masked:
                s = jnp.where(mask, s, NEG)
            m_prev = m_ref[j]
            m_new = jnp.maximum(m_prev, jnp.max(s, axis=-1, keepdims=True))
            p = jnp.exp(s - m_new)
            if masked:
                p = jnp.where(mask, p, 0.0)
            alpha = jnp.exp(m_prev - m_new)
            l_ref[j] = alpha * l_ref[j] + jnp.sum(p, axis=-1, keepdims=True)
            acc_ref[j] = alpha * acc_ref[j] + _dot(p.astype(BF16), v_ref[:, j * V_DIM:(j + 1) * V_DIM])
            m_ref[j] = m_new

    needs_mask = ki * tk + tk - 1 > qi * tq

    @pl.when(needs_mask)
    def _():
        step(True)

    @pl.when(jnp.logical_not(needs_mask))
    def _():
        step(False)

    @pl.when(ki == last)
    def _():
        for j in range(2):
            o_ref[:, j * V_DIM:(j + 1) * V_DIM] = acc_ref[j] / jnp.maximum(l_ref[j], TINY)


def _causal_steps(nq, tq, tk):
    qi, ki = [], []
    for j in range(nq):
        for k in range((j * tq + tq - 1) // tk + 1):
            qi.append(j)
            ki.append(k)
    return jnp.asarray(qi, I32), jnp.asarray(ki, I32)


def _mla_flash(q, k, v, b, t, tq=512, tk=512):
    tq, tk = min(tq, t), min(tk, t)
    nq, nk = t // tq, t // tk
    qi, ki = _causal_steps(nq, tq, tk)
    pairs = MLA_HEADS // 2
    grid_spec = pltpu.PrefetchScalarGridSpec(
        num_scalar_prefetch=2, grid=(b, pairs, qi.shape[0]),
        in_specs=[pl.BlockSpec((tq, 2 * MLA_QK), lambda i, h, n, qt, kt: (i * nq + qt[n], h)),
                  pl.BlockSpec((tk, 2 * MLA_QK), lambda i, h, n, qt, kt: (i * nk + kt[n], h)),
                  pl.BlockSpec((tk, 2 * V_DIM), lambda i, h, n, qt, kt: (i * nk + kt[n], h))],
        out_specs=pl.BlockSpec((tq, 2 * V_DIM), lambda i, h, n, qt, kt: (i * nq + qt[n], h)),
        scratch_shapes=[pltpu.VMEM((2, tq, 1), F32), pltpu.VMEM((2, tq, 1), F32),
                        pltpu.VMEM((2, tq, V_DIM), F32)])
    return pl.pallas_call(
        functools.partial(_mla_flash_kernel, tq=tq, tk=tk), grid_spec=grid_spec,
        out_shape=_sds((b * t, MLA_HEADS * V_DIM)),
        compiler_params=_params(("arbitrary", "arbitrary", "arbitrary"), 48),
        name="mla_flash_prompt")(qi, ki, q, k, v)


def _mla_flash_t_kernel(qi_ref, ki_ref, q_ref, k_ref, v_ref, o_ref, m_ref, acc_ref, *, tq, tk):
    n = pl.program_id(2)
    qi = qi_ref[n]
    ki = ki_ref[n]
    last = (qi * tq + tq - 1) // tk

    @pl.when(ki == 0)
    def _():
        m_ref[...] = jnp.full(m_ref.shape, NEG, F32)
        acc_ref[...] = jnp.zeros(acc_ref.shape, F32)

    def step(masked):
        mask = None
        if masked:
            mask = (ki * tk + lax.broadcasted_iota(I32, (tk, tq), 0)
                    <= qi * tq + lax.broadcasted_iota(I32, (tk, tq), 1))
        for j in range(2):
            qk = slice(j * MLA_QK, (j + 1) * MLA_QK)
            _flash_update_t(q_ref[:, qk], k_ref[:, qk], v_ref[:, j * V_DIM:(j + 1) * V_DIM], mask,
                            m_ref.at[j], acc_ref.at[j])

    needs_mask = ki * tk + tk - 1 > qi * tq

    @pl.when(needs_mask)
    def _():
        step(True)

    @pl.when(jnp.logical_not(needs_mask))
    def _():
        step(False)

    @pl.when(ki == last)
    def _():
        for j in range(2):
            o_ref[:, j * V_DIM:(j + 1) * V_DIM] = _flash_result_t(acc_ref[j]).T


def _mla_flash_t(q, k, v, b, t, tq=1024, tk=512):
    tq, tk = min(tq, t), min(tk, t)
    nq, nk = t // tq, t // tk
    qi, ki = _causal_steps(nq, tq, tk)
    pairs = MLA_HEADS // 2
    grid_spec = pltpu.PrefetchScalarGridSpec(
        num_scalar_prefetch=2, grid=(b, pairs, qi.shape[0]),
        in_specs=[pl.BlockSpec((tq, 2 * MLA_QK), lambda i, h, n, qt, kt: (i * nq + qt[n], h)),
                  pl.BlockSpec((tk, 2 * MLA_QK), lambda i, h, n, qt, kt: (i * nk + kt[n], h)),
                  pl.BlockSpec((tk, 2 * V_DIM), lambda i, h, n, qt, kt: (i * nk + kt[n], h))],
        out_specs=pl.BlockSpec((tq, 2 * V_DIM), lambda i, h, n, qt, kt: (i * nq + qt[n], h)),
        scratch_shapes=[pltpu.VMEM((2, 1, tq), F32), pltpu.VMEM((2, 2 * V_DIM, tq), F32)])
    return pl.pallas_call(
        functools.partial(_mla_flash_t_kernel, tq=tq, tk=tk), grid_spec=grid_spec,
        out_shape=_sds((b * t, MLA_HEADS * V_DIM)),
        compiler_params=_params(("arbitrary", "arbitrary", "arbitrary"), 48),
        name="mla_flash_prompt")(qi, ki, q, k, v)


def _odd_layer_prompt(x, b, t, g_mix, w_in, q_norm, w_uq, kv_norm, w_uk, w_uv, w_out):
    mw = _mla_weights(w_in, w_uq, w_uk)
    r = x.shape[0]
    outs = [_sds((r, LAT)), _sds((r, MLA_HEADS * MLA_QK), BF16), _sds((r, MLA_HEADS * MLA_QK), BF16),
            _sds((r, MLA_HEADS * V_DIM), BF16)]
    extra = [mw["wk_lat"], mw["wk_pe"], w_uv.reshape(KV_LORA, MLA_HEADS * V_DIM).astype(BF16)]
    lat, q, k, v = _mla_proj(x, jnp.arange(t, dtype=I32), g_mix, q_norm, kv_norm, mw, extra,
                             _mla_proj_prompt_body, outs, "mla_proj_prompt")
    o = _mla_flash_t(q, k, v, b, t)
    y = _matmul_res([o], [w_out], x, "mla_out_proj")
    return y, lat.reshape(b, t, LAT)


def _mla_sample_kernel(pt_ref, pool_ref, q_ref, new_ref, o_ref, buf_ref, sem_ref, *,
                       layer, n_pages, s, chunk):
    slot = _paged_fetch(pool_ref, layer, pt_ref, n_pages, buf_ref, sem_ref)
    past = n_pages * PAGE_ROWS
    m = MLA_HEADS * s
    q = q_ref[:, 0].reshape(m, LAT).astype(BF16)
    state = _online_init(m, KV_LORA)
    for c0 in range(0, past, chunk):
        kv = buf_ref[slot, :, c0:c0 + chunk].astype(BF16)
        state = _online(state, _dot(q, kv), None, kv[:KV_LORA], v_t=True)
    new = new_ref[0].astype(BF16)
    causal = (lax.broadcasted_iota(I32, (s, s), 1) <= lax.broadcasted_iota(I32, (s, s), 0))
    state = _online(state, _dot_t(q, new), jnp.tile(causal, (MLA_HEADS, 1)), new[:, :KV_LORA])
    o = state[2] / jnp.maximum(state[1], TINY)
    o_ref[:, 0] = o.reshape(MLA_HEADS, s, KV_LORA)


def _mla_sample_attend(q, pool, layer, page_table, lat_new, sb, s):
    n_pages = page_table.shape[1]
    past = n_pages * PAGE_ROWS
    q4 = q.reshape(MLA_HEADS, sb, s, LAT)
    kern = functools.partial(_mla_sample_kernel, layer=layer, n_pages=n_pages, s=s,
                             chunk=min(2048, past))
    grid_spec = pltpu.PrefetchScalarGridSpec(
        num_scalar_prefetch=1, grid=(sb,),
        in_specs=[pl.BlockSpec(memory_space=pl.ANY),
                  pl.BlockSpec((MLA_HEADS, 1, s, LAT), lambda i, pt: (0, i, 0, 0)),
                  pl.BlockSpec((1, s, LAT), lambda i, pt: (i, 0, 0))],
        out_specs=pl.BlockSpec((MLA_HEADS, 1, s, KV_LORA), lambda i, pt: (0, i, 0, 0)),
        scratch_shapes=[pltpu.VMEM((2, LAT, past), F32), pltpu.SemaphoreType.DMA((2,))])
    o = pl.pallas_call(
        kern, grid_spec=grid_spec, out_shape=_sds((MLA_HEADS, sb, s, KV_LORA)),
        compiler_params=_params(("arbitrary",), 56), name="mla_paged_sample")(
            page_table.reshape(-1), _channel_major(pool), q4, lat_new.reshape(sb, s, LAT))
    return o.reshape(MLA_HEADS, sb * s, KV_LORA)


def _mla_out_sample_body(rows, consts, outs):
    o_ref, res_ref = rows
    wuv_ref, wout_ref = consts
    acc = res_ref[...]
    for hd in range(MLA_HEADS):
        o_h = _dot(o_ref[hd].astype(BF16), wuv_ref[hd])
        acc = acc + _dot(o_h.astype(BF16), wout_ref[hd])
    outs[0][...] = acc


def _odd_layer_sample(x, sb, s, layer, pool, page_table, g_mix, w_in, q_norm, w_uq, kv_norm, w_uk,
                      w_uv, w_out):
    mw = _mla_weights(w_in, w_uq, w_uk)
    r = x.shape[0]
    past = page_table.shape[1] * PAGE_ROWS
    pos = past + jnp.arange(_row_tile(r, 512), dtype=I32) % s
    outs = [_sds((r, LAT)), _sds((MLA_HEADS, r, LAT))]
    wukt = w_uk.transpose(1, 2, 0).astype(BF16)
    lat, q = _mla_proj(x, pos, g_mix, q_norm, kv_norm, mw, [wukt],
                       _mla_proj_sample_body, outs, "mla_proj_sample")
    o_lat = _mla_sample_attend(q, pool, layer, page_table, lat, sb, s)
    wuv = w_uv.transpose(1, 0, 2).astype(BF16)
    wout = w_out.reshape(MLA_HEADS, V_DIM, -1).astype(BF16)
    y = _rowwise(_mla_out_sample_body, [o_lat, x], [wuv, wout], [_sds(x.shape)],
                 _row_tile(r, 512), "mla_out_sample")[0]
    return y, lat.reshape(sb, s, LAT)


def _plain_matmul_body(rows, consts, outs):
    outs[0][...] = _dot(rows[0][...].astype(BF16), consts[0][...])


def _mem_q_body(rows, consts, outs):
    g_ref, w_ref = consts
    h = _rms(rows[0][...], g_ref[...]).astype(BF16)
    dh = w_ref.shape[1] // MEM_HEADS
    outs[0][...] = (_dot(h, w_ref[...]) * (dh ** -0.5 * LOG2E)).astype(BF16)


def _mem_attn_kernel(q_ref, kv_ref, o_ref, *, split):
    d = q_ref.shape[2]
    dh = d // MEM_HEADS
    for hd in range(MEM_HEADS):
        if split:
            parts = dh // kv_ref.shape[5]
            rows = lambda kv: jnp.concatenate(
                [kv_ref[0, 0, :, kv, part * MEM_HEADS + hd, :] for part in range(parts)],
                axis=1).astype(BF16)
            k, v = rows(0), rows(1)
        else:
            k = kv_ref[0, :, hd * dh:(hd + 1) * dh].astype(BF16)
            v = kv_ref[0, :, d + hd * dh:d + (hd + 1) * dh].astype(BF16)
        s = _dot_t(q_ref[0, :, hd * dh:(hd + 1) * dh], k)
        p = jnp.exp2(s - jnp.max(s, axis=-1, keepdims=True))
        p = p / jnp.sum(p, axis=-1, keepdims=True)
        o_ref[0, :, hd * dh:(hd + 1) * dh] = _dot(p.astype(BF16), v).astype(BF16)


def _mem_layer(y, nb, kv, g_mem, w_q, w_o, layer=None):
    r, d = y.shape
    rows = r // nb
    q = _rowwise(_mem_q_body, [y], [g_mem.reshape(1, d), w_q.astype(BF16)], [_sds((r, d), BF16)],
                 _row_tile(r, 512), "mem_q_proj")[0]
    tq = _row_tile(rows, 512)
    nt = rows // tq
    if layer is None:
        kv_spec = pl.BlockSpec((1,) + kv.shape[1:], lambda i, j: (i, 0, 0))
    else:
        nl, _, mem, two, heads, dh = kv.shape
        parts = dh // V7X_LANES
        kv = kv.reshape(nl, nb, mem, two, heads, parts, V7X_LANES).transpose(0, 1, 2, 3, 5, 4, 6)
        kv = kv.reshape(nl, nb, mem, two, parts * heads, V7X_LANES)
        kv_spec = pl.BlockSpec((1, 1) + kv.shape[2:], lambda i, j: (layer, i, 0, 0, 0, 0))
    o = pl.pallas_call(
        functools.partial(_mem_attn_kernel, split=layer is not None), grid=(nb, nt),
        in_specs=[pl.BlockSpec((1, tq, d), lambda i, j: (i, j, 0)), kv_spec],
        out_specs=pl.BlockSpec((1, tq, d), lambda i, j: (i, j, 0)),
        out_shape=_sds((nb, rows, d), BF16),
        compiler_params=_params(("arbitrary", "arbitrary"), 48), name="mem_attention")(
            q.reshape(nb, rows, d), kv)
    return _matmul_res([o.reshape(r, d)], [w_o], y, "mem_out_proj")


def _mem_project(mem, w_kv):
    nb, m, d = mem.shape
    kv = _rowwise(_plain_matmul_body, [mem.reshape(nb * m, d)], [w_kv.astype(BF16)],
                  [_sds((nb * m, w_kv.shape[1]))], _row_tile(nb * m, 512), "mem_kv_proj")[0]
    return kv.reshape(nb, m, w_kv.shape[1])


def _route(x, wr_ref, br_ref):
    xh, xm, xl = _split3(x)
    w0, w1, w2 = wr_ref[0], wr_ref[1], wr_ref[2]
    logit = (_dot(xh, w0) + (_dot(xh, w1) + _dot(xm, w0))
             + (_dot(xh, w2) + _dot(xl, w0) + _dot(xm, w1))) + br_ref[...]
    lane = lax.broadcasted_iota(I32, logit.shape, 1).astype(F32)
    big = float(ROUTE_LANES)
    is_g = lane < N_GROUPS
    gl = jnp.where(is_g, logit, -jnp.inf)
    gmax = jnp.max(gl, axis=-1, keepdims=True)
    gidx = jnp.min(jnp.where(gl == gmax, lane, big), axis=-1, keepdims=True)
    g_gate = 1.0 / jnp.sum(jnp.where(is_g, jnp.exp(gl - gmax), 0.0), axis=-1, keepdims=True)
    lo = N_GROUPS + gidx * EXP_PER_GROUP
    in_grp = (lane >= lo) & (lane < lo + EXP_PER_GROUP)
    el = jnp.where(in_grp, logit, -jnp.inf)
    ee = jnp.where(in_grp, jnp.exp(el - jnp.max(el, axis=-1, keepdims=True)), 0.0)
    prob = ee / jnp.sum(ee, axis=-1, keepdims=True)
    cand = jnp.where(in_grp, prob, -1.0)
    p1 = jnp.max(cand, axis=-1, keepdims=True)
    i1 = jnp.min(jnp.where(cand == p1, lane, big), axis=-1, keepdims=True)
    cand = jnp.where(lane == i1, -1.0, cand)
    p2 = jnp.max(cand, axis=-1, keepdims=True)
    i2 = jnp.min(jnp.where(cand == p2, lane, big), axis=-1, keepdims=True)
    tot = p1 + p2
    return (jnp.where(lane == i1, p1 / tot * g_gate, 0.0)
            + jnp.where(lane == i2, p2 / tot * g_gate, 0.0))


def _moe_kernel(y_ref, g_ref, wr_ref, br_ref, w13_ref, w2_ref, gf_ref, o_ref, h_ref, comb_ref,
                acc_ref, *, final_norm):
    e = pl.program_id(1)

    @pl.when(e == 0)
    def _():
        x = _rms(y_ref[...], g_ref[...])
        h_ref[...] = x.astype(BF16)
        comb_ref[...] = _route(x, wr_ref, br_ref)
        acc_ref[...] = jnp.zeros(acc_ref.shape, F32)

    hc = _dot(h_ref[...], w13_ref[0])
    ff = hc.shape[1] // 2
    a, bb = hc[:, :ff], hc[:, ff:]
    comb = comb_ref[...]
    lane = lax.broadcasted_iota(I32, comb.shape, 1)
    ce = jnp.sum(jnp.where(lane == e + N_GROUPS, comb, 0.0), axis=-1, keepdims=True)
    act = a * (1.0 / (1.0 + jnp.exp(-a))) * bb * ce
    acc_ref[...] += _dot(act.astype(BF16), w2_ref[0])

    @pl.when(e == pl.num_programs(1) - 1)
    def _():
        out = y_ref[...] + acc_ref[...]
        if final_norm:
            out = _rms(out, gf_ref[...])
        o_ref[...] = out


def _moe_layer(y, g_ffn, w_group, b_group, w_expert, b_expert, w13, w2, g_final=None):
    r, d = y.shape
    tm = _row_tile(r, 1024)
    n_route = N_GROUPS + N_EXPERTS
    wr = jnp.concatenate([w_group, w_expert, jnp.zeros((d, ROUTE_LANES - n_route), F32)], axis=1)
    wr = jnp.stack(_split3(wr))
    br = jnp.concatenate([b_group, b_expert, jnp.zeros((ROUTE_LANES - n_route,), F32)]).reshape(1, -1)
    gf = (g_ffn if g_final is None else g_final).reshape(1, d)
    ff2 = w13.shape[2]
    return pl.pallas_call(
        functools.partial(_moe_kernel, final_norm=g_final is not None), grid=(r // tm, N_EXPERTS),
        in_specs=[pl.BlockSpec((tm, d), lambda i, e: (i, 0)),
                  pl.BlockSpec((1, d), lambda i, e: (0, 0)),
                  pl.BlockSpec(wr.shape, lambda i, e: (0, 0, 0)),
                  pl.BlockSpec(br.shape, lambda i, e: (0, 0)),
                  pl.BlockSpec((1, d, ff2), lambda i, e: (e, 0, 0)),
                  pl.BlockSpec((1, ff2 // 2, d), lambda i, e: (e, 0, 0)),
                  pl.BlockSpec((1, d), lambda i, e: (0, 0))],
        out_specs=pl.BlockSpec((tm, d), lambda i, e: (i, 0)),
        out_shape=_sds((r, d)),
        scratch_shapes=[pltpu.VMEM((tm, d), BF16), pltpu.VMEM((tm, ROUTE_LANES), F32),
                        pltpu.VMEM((tm, d), F32)],
        compiler_params=_params(("arbitrary", "arbitrary"), 56), name="hier_moe")(
            y, g_ffn.reshape(1, d), wr, br, w13.astype(BF16), w2.astype(BF16), gf)


def kernel(x_prompt, x_sample, mem_prompt, cache_nsa_cmp_kv, cache_nsa_slc_kv, state_nsa_win_kv, state_conv, cache_mla_latent, cache_mem_kv, page_table, norm_mix, norm_mem, norm_ffn, norm_final, e_w_in, e_w_out, e_cmp_w, e_cmp_pe, e_conv_w, o_w_in, o_q_norm, o_w_uq, o_kv_norm, o_w_uk, o_w_uv, o_w_out, mem_w_q, mem_w_kv, mem_w_o, moe_w_group, moe_b_group, moe_w_expert, moe_b_expert, moe_w13, moe_w2):
    b, t, d = x_prompt.shape
    sb, s, _ = x_sample.shape
    depth = norm_mix.shape[0]
    yp = x_prompt.reshape(b * t, d)
    ys = x_sample.reshape(sb * s, d)
    p_even, s_even, p_mla, s_mla, p_mem = [], [], [], [], []
    for l in range(depth):
        if l % 2 == 0:
            e = l // 2
            w = (norm_mix[l], e_w_in[e], e_w_out[e], e_cmp_w[e], e_cmp_pe[e], e_conv_w[e])
            yp, cp = _even_layer_prompt(yp, b, t, *w)
            ys, cs = _even_layer_sample(ys, sb, s, e, cache_nsa_cmp_kv, cache_nsa_slc_kv,
                                        state_nsa_win_kv[e], state_conv[e], page_table, *w)
            p_even.append(cp)
            s_even.append(cs)
        else:
            o = l // 2
            w = (norm_mix[l], o_w_in[o], o_q_norm[o], o_w_uq[o], o_kv_norm[o], o_w_uk[o], o_w_uv[o],
                 o_w_out[o])
            yp, lp = _odd_layer_prompt(yp, b, t, *w)
            ys, ls = _odd_layer_sample(ys, sb, s, o, cache_mla_latent, page_table, *w)
            p_mla.append(lp)
            s_mla.append(ls)
        kv_p = _mem_project(mem_prompt, mem_w_kv[l])
        p_mem.append(kv_p.reshape(b, kv_p.shape[1], 2, MEM_HEADS, d // MEM_HEADS))
        yp = _mem_layer(yp, b, kv_p, norm_mem[l], mem_w_q[l], mem_w_o[l])
        ys = _mem_layer(ys, sb, cache_mem_kv, norm_mem[l], mem_w_q[l], mem_w_o[l], layer=l)
        g_final = norm_final if l == depth - 1 else None
        moe_w = (norm_ffn[l], moe_w_group[l], moe_b_group[l], moe_w_expert[l], moe_b_expert[l],
                 moe_w13[l], moe_w2[l])
        yp = _moe_layer(yp, *moe_w, g_final=g_final)
        ys = _moe_layer(ys, *moe_w, g_final=g_final)
    stack = lambda caches, k: jnp.stack([c[k] for c in caches])
    return (yp.reshape(b, t, d), ys.reshape(sb, s, d),
            stack(p_even, 0), stack(p_even, 1), stack(p_even, 2), stack(p_even, 3),
            jnp.stack(p_mla), jnp.stack(p_mem),
            stack(s_even, 0), stack(s_even, 1), stack(s_even, 2), stack(s_even, 3),
            jnp.stack(s_mla))
```

```python
import functools
import math

import jax
import jax.numpy as jnp
from jax import lax
from jax.experimental import pallas as pl
from jax.experimental.pallas import tpu as pltpu

F32 = jnp.float32
BF16 = jnp.bfloat16
I32 = jnp.int32

EPS = 1e-6
LOG2E = math.log2(math.e)
NEG = -1e30
TINY = 1e-30
FORCE_BONUS = 1e4

NSA_HEADS = 8
NSA_GROUPS = 2
NSA_REP = NSA_HEADS // NSA_GROUPS
NSA_DH = 64
NSA_KV_ROW = 2 * NSA_GROUPS * NSA_DH
CMP_STRIDE = 16
CMP_BLOCK = 32
SEL_BLOCK = 64
SEL_TOPK = 16
WINDOW = 512
CONV_CH = 512
CONV_W = 3
MLA_HEADS = 16
Q_LORA = 384
KV_LORA = 256
NOPE = 64
ROPE = 32
V_DIM = 64
ROPE_THETA = 10000.0
LAT = KV_LORA + ROPE
MLA_QK = 128
MEM_HEADS = 4
N_GROUPS = 4
EXP_PER_GROUP = 4
N_EXPERTS = 16
EXPERT_FF = 256
ROUTE_LANES = 128

V7X_VMEM_BYTES = 64 * 2**20
V7X_LANES = 128


def _params(sem, vmem_mb):
    assert vmem_mb * 2**20 < V7X_VMEM_BYTES
    return pltpu.CompilerParams(dimension_semantics=sem, vmem_limit_bytes=vmem_mb * 2**20)


def _dot(a, b):
    return jnp.dot(a, b, preferred_element_type=F32)


def _dot_t(a, b):
    return lax.dot_general(a, b, (((1,), (1,)), ((), ())), preferred_element_type=F32)


def _tdot(a, b):
    return lax.dot_general(a, b, (((0,), (0,)), ((), ())), preferred_element_type=F32)


def _split3(x):
    hi = x.astype(BF16)
    r1 = x - hi.astype(F32)
    mid = r1.astype(BF16)
    lo = (r1 - mid.astype(F32)).astype(BF16)
    return hi, mid, lo


def _rms(x, g):
    return x * lax.rsqrt(jnp.mean(x * x, axis=-1, keepdims=True) + EPS) * g


def _row_spec(shape, tm, period=None):
    lead = len(shape) - 2
    blk = tuple(shape[:lead]) + (tm, shape[-1])
    if period is None:
        return pl.BlockSpec(blk, lambda i, lead=lead: (0,) * lead + (i, 0))
    return pl.BlockSpec(blk, lambda i, lead=lead, p=period: (0,) * lead + (i % p, 0))


def _const_spec(shape):
    return pl.BlockSpec(tuple(shape), lambda i, n=len(shape): (0,) * n)


def _rowwise(body, rows, consts, outs, tm, name, vmem_mb=48, periods=None):
    n_rows = rows[0].shape[-2]
    assert n_rows % tm == 0
    periods = periods or {}
    in_specs = [_row_spec(a.shape, tm, periods.get(k)) for k, a in enumerate(rows)]
    in_specs += [_const_spec(c.shape) for c in consts]
    out_specs = [_row_spec(o.shape, tm) for o in outs]
    nr, nc = len(rows), len(consts)

    def kern(*refs):
        body(refs[:nr], refs[nr:nr + nc], refs[nr + nc:])

    return pl.pallas_call(
        kern, grid=(n_rows // tm,), in_specs=in_specs, out_specs=out_specs, out_shape=outs,
        compiler_params=_params(("arbitrary",), vmem_mb), name=name)(*rows, *consts)


def _sds(shape, dtype=F32):
    return jax.ShapeDtypeStruct(tuple(shape), dtype)


def _row_tile(n_rows, want):
    tm = min(want, n_rows)
    assert n_rows % tm == 0
    return tm


def _even_proj_body(rows, consts, outs):
    (x_ref,) = rows
    g_ref, wq_ref, wkv_ref, wg_ref, wc_ref = consts
    q_ref, kvc_ref, kvs_ref, kvw_ref, kvb_ref, gate_ref, gb_ref, u_ref = outs
    h = _rms(x_ref[...], g_ref[...]).astype(BF16)
    q = _dot(h, wq_ref[...]) * (NSA_DH ** -0.5 * LOG2E)
    for hd in range(NSA_HEADS):
        q_ref[hd // NSA_REP, hd % NSA_REP] = q[:, hd * NSA_DH:(hd + 1) * NSA_DH].astype(BF16)
    kv = _dot(h, wkv_ref[...])
    kvc_ref[...] = kv[:, :NSA_KV_ROW]
    kvs_ref[...] = kv[:, NSA_KV_ROW:2 * NSA_KV_ROW]
    kvw_ref[...] = kv[:, 2 * NSA_KV_ROW:]
    for j in range(3 * 2 * NSA_GROUPS):
        kvb_ref[j] = kv[:, j * NSA_DH:(j + 1) * NSA_DH].astype(BF16)
    gate_ref[...] = 1.0 / (1.0 + jnp.exp(-_dot(h, wg_ref[...])))
    c = _dot(h, wc_ref[...])
    gb_ref[...] = c[:, :CONV_CH]
    u_ref[...] = c[:, CONV_CH:2 * CONV_CH] * c[:, 2 * CONV_CH:]


def _even_proj(x, g, w_in):
    r, d = x.shape
    qc = NSA_HEADS * NSA_DH
    kc = 3 * NSA_KV_ROW
    gc = NSA_HEADS * 3
    wq = w_in[:, :qc].astype(BF16)
    wkv = w_in[:, qc:qc + kc].astype(BF16)
    wg = w_in[:, qc + kc:qc + kc + gc].astype(BF16)
    wc = w_in[:, qc + kc + gc:].astype(BF16)
    tm = _row_tile(r, 512)
    outs = [
        _sds((NSA_GROUPS, NSA_REP, r, NSA_DH), BF16),
        _sds((r, NSA_KV_ROW)), _sds((r, NSA_KV_ROW)), _sds((r, NSA_KV_ROW)),
        _sds((12, r, NSA_DH), BF16),
        _sds((r, gc)), _sds((r, CONV_CH)), _sds((r, CONV_CH)),
    ]
    return _rowwise(_even_proj_body, [x], [g.reshape(1, d), wq, wkv, wg, wc], outs, tm, "even_proj")


def _compress_weights(cmp_w, cmp_pe):
    eye_g = jnp.eye(NSA_GROUPS, dtype=F32)
    eye_k = jnp.eye(2, dtype=F32)

    def big(wh):
        return jnp.einsum("kjde,kl,gh->jkgdlhe", wh, eye_k, eye_g).reshape(
            CMP_STRIDE * NSA_KV_ROW, NSA_KV_ROW)

    w = jnp.concatenate([big(cmp_w[:, :CMP_STRIDE]), big(cmp_w[:, CMP_STRIDE:])], axis=1)

    def pe_row(peh):
        return jnp.broadcast_to(peh.transpose(1, 0, 2)[:, :, None, :],
                                (CMP_STRIDE, 2, NSA_GROUPS, NSA_DH)).reshape(1, -1)

    pe = jnp.concatenate([pe_row(cmp_pe[:, :CMP_STRIDE]), pe_row(cmp_pe[:, CMP_STRIDE:]),
                          jnp.zeros((6, CMP_STRIDE * NSA_KV_ROW), F32)], axis=0)
    return w.astype(BF16), pe


def _compress_pair_weights(cmp_w, cmp_pe):
    pairs = CMP_STRIDE // 2
    w = cmp_w.reshape(2, 2, pairs, 2, NSA_DH, NSA_DH)
    w = jnp.einsum("khpjde,gG->kpjgdhGe", w, jnp.eye(NSA_GROUPS, dtype=F32))
    w = w.reshape(2, pairs, 2 * NSA_GROUPS * NSA_DH, 2 * NSA_GROUPS * NSA_DH)
    pe = cmp_pe.reshape(2, 2, pairs, 2, 1, NSA_DH)
    pe = jnp.broadcast_to(pe, (2, 2, pairs, 2, NSA_GROUPS, NSA_DH)).transpose(0, 2, 1, 3, 4, 5)
    pe = pe.reshape(2, pairs, 2, 2 * NSA_GROUPS * NSA_DH)
    pe = jnp.concatenate([pe, jnp.zeros((2, pairs, 6, pe.shape[-1]), F32)], axis=2)
    return w.astype(BF16), pe


def _compress_core(x, w_ref, pe_ref):
    return _compress_finish(_dot(x, w_ref[...]), w_ref, pe_ref)


def _compress_finish(p, w_ref, pe_ref):
    n = p.shape[0]
    w = w_ref[...]
    hi, mid, lo = _split3(pe_ref[...])
    pb = _dot(hi, w) + _dot(mid, w) + _dot(lo, w)
    bias = pb[0:1, :NSA_KV_ROW] + pb[1:2, NSA_KV_ROW:]
    return p[:, :NSA_KV_ROW] + pltpu.roll(p[:, NSA_KV_ROW:], n - 1, 0) + bias


def _compress_prompt_kernel(x_ref, w_ref, pe_ref, o_ref):
    c = _compress_core(x_ref[0].astype(BF16), w_ref, pe_ref)
    for j in range(2 * NSA_GROUPS):
        o_ref[0, j] = c[:, j * NSA_DH:(j + 1) * NSA_DH].astype(BF16)


def _compress_prompt(kvc, b, t, w, pe):
    n = t // CMP_STRIDE
    x = kvc.reshape(b, n, CMP_STRIDE * NSA_KV_ROW)
    return pl.pallas_call(
        _compress_prompt_kernel, grid=(b,),
        in_specs=[pl.BlockSpec((1, n, CMP_STRIDE * NSA_KV_ROW), lambda i: (i, 0, 0)),
                  pl.BlockSpec(w.shape, lambda i: (0, 0)), pl.BlockSpec(pe.shape, lambda i: (0, 0))],
        out_specs=pl.BlockSpec((1, 4, n, NSA_DH), lambda i: (i, 0, 0, 0)),
        out_shape=_sds((b, 4, n, NSA_DH), BF16),
        compiler_params=_params(("arbitrary",), 48), name="nsa_compress_prompt")(x, w, pe)


def _overlap_matrix(n_cmp, n_sel):
    i = jnp.arange(n_cmp)[:, None]
    j = jnp.arange(n_sel)[None, :]
    per = SEL_BLOCK // CMP_STRIDE
    return ((i // per == j).astype(F32) + ((i + 1) // per == j).astype(F32)).astype(BF16)


def _topk_mask(score, k):
    n = score.shape[-1]
    lane = lax.broadcasted_iota(I32, score.shape, score.ndim - 1).astype(F32)
    sel = jnp.zeros_like(score)
    for _ in range(k):
        m = jnp.max(score, axis=-1, keepdims=True)
        first = jnp.min(jnp.where(score == m, lane, float(n)), axis=-1, keepdims=True)
        pick = lane == first
        sel = jnp.where(pick, 1.0, sel)
        score = jnp.where(pick, -jnp.inf, score)
    return sel


def _masked_softmax(s, mask):
    s = jnp.where(mask, s, NEG)
    m = jnp.max(s, axis=-1, keepdims=True)
    p = jnp.where(mask, jnp.exp2(s - m), 0.0)
    return p / jnp.maximum(jnp.sum(p, axis=-1, keepdims=True), TINY)


def _cmp_branch(q, ck, cv, qpos, ovl, topk):
    tq = qpos.shape[0]
    n = ck.shape[0]
    n_sel = ovl.shape[1]
    cmp_end = CMP_STRIDE * lax.broadcasted_iota(I32, (tq, n), 1) + (CMP_BLOCK - 1)
    cmask = (cmp_end <= qpos)[None]
    s = _dot_t(q, ck).reshape(NSA_REP, tq, n)
    p = _masked_softmax(s, cmask)
    o = _dot(p.reshape(NSA_REP * tq, n).astype(BF16), cv)
    imp = jnp.sum(p, axis=0)
    hi, mid, lo = _split3(imp)
    sel_imp = _dot(hi, ovl) + _dot(mid, ovl) + _dot(lo, ovl)
    blk = lax.broadcasted_iota(I32, (tq, n_sel), 1)
    qblk = qpos // SEL_BLOCK
    forced = (blk == 0) | (blk == qblk) | (blk == qblk - 1)
    score = jnp.where(forced, sel_imp + FORCE_BONUS, jnp.where(blk <= qblk, sel_imp, NEG))
    return o, _topk_mask(score, topk)


def _nsa_cmp_prompt_kernel(q_ref, c_ref, ovl_ref, o_ref, sel_ref, *, tq):
    qpos = pl.program_id(1) * tq + lax.broadcasted_iota(I32, (tq, 1), 0)
    n_sel = ovl_ref.shape[1]
    for g in range(NSA_GROUPS):
        q = q_ref[g, :, 0].reshape(NSA_REP * tq, NSA_DH)
        o, sel = _cmp_branch(q, c_ref[0, g], c_ref[0, NSA_GROUPS + g], qpos, ovl_ref[...],
                             min(SEL_TOPK, n_sel))
        for r in range(NSA_REP):
            hd = g * NSA_REP + r
            o_ref[:, hd * NSA_DH:(hd + 1) * NSA_DH] = o[r * tq:(r + 1) * tq]
        sel_ref[:, g * n_sel:(g + 1) * n_sel] = sel.astype(BF16)


def _nsa_cmp_prompt(q, cmp, b, t, tq=256):
    n = t // CMP_STRIDE
    n_sel = t // SEL_BLOCK
    tq = min(tq, t)
    nq = t // tq
    ovl = _overlap_matrix(n, n_sel)
    q5 = q.reshape(NSA_GROUPS, NSA_REP, b, t, NSA_DH)
    return pl.pallas_call(
        functools.partial(_nsa_cmp_prompt_kernel, tq=tq), grid=(b, nq),
        in_specs=[pl.BlockSpec((NSA_GROUPS, NSA_REP, 1, tq, NSA_DH), lambda i, j: (0, 0, i, j, 0)),
                  pl.BlockSpec((1, 4, n, NSA_DH), lambda i, j: (i, 0, 0, 0)),
                  pl.BlockSpec(ovl.shape, lambda i, j: (0, 0))],
        out_specs=[pl.BlockSpec((tq, NSA_HEADS * NSA_DH), lambda i, j, nq=nq: (i * nq + j, 0)),
                   pl.BlockSpec((tq, NSA_GROUPS * n_sel), lambda i, j, nq=nq: (i * nq + j, 0))],
        out_shape=[_sds((b * t, NSA_HEADS * NSA_DH)), _sds((b * t, NSA_GROUPS * n_sel), BF16)],
        compiler_params=_params(("arbitrary", "arbitrary"), 48), name="nsa_cmp_prompt")(q5, cmp, ovl)


def _flash_update(q, k, v, mask, m_ref, l_ref, acc_ref):
    tq, tk = mask.shape
    rep = q.shape[0] // tq
    s = _dot_t(q, k).reshape(rep, tq, tk)
    s = jnp.where(mask[None], s, NEG)
    m_prev = m_ref[...]
    m_new = jnp.maximum(m_prev, jnp.max(s, axis=-1, keepdims=True))
    p = jnp.where(mask[None], jnp.exp(s - m_new), 0.0)
    alpha = jnp.exp(m_prev - m_new)
    l_ref[...] = alpha * l_ref[...] + jnp.sum(p, axis=-1, keepdims=True)
    pv = _dot(p.reshape(rep * tq, tk).astype(BF16), v)
    acc_ref[...] = alpha.reshape(rep * tq, 1) * acc_ref[...] + pv
    m_ref[...] = m_new


def _nsa_sw_prompt_kernel(q_ref, ks_ref, vs_ref, kw_ref, vw_ref, sel_ref, ocmp_ref, gate_ref,
                          o_ref, m_ref, l_ref, acc_ref, *, tq, tk):
    qi = pl.program_id(1)
    ki = pl.program_id(2)
    q0 = qi * tq
    last = (q0 + tq - 1) // tk
    first_w = jnp.maximum(q0 - (WINDOW - 1), 0) // tk
    n_sel = sel_ref.shape[1] // NSA_GROUPS
    per = tk // SEL_BLOCK

    @pl.when(ki == 0)
    def _():
        m_ref[...] = jnp.full(m_ref.shape, NEG, F32)
        l_ref[...] = jnp.zeros(l_ref.shape, F32)
        acc_ref[...] = jnp.zeros(acc_ref.shape, F32)

    qpos = q0 + lax.broadcasted_iota(I32, (tq, tk), 0)
    kpos = ki * tk + lax.broadcasted_iota(I32, (tq, tk), 1)
    causal = kpos <= qpos

    @pl.when(ki <= last)
    def _():
        expand = (lax.broadcasted_iota(I32, (n_sel, tk), 0)
                  == ki * per + lax.broadcasted_iota(I32, (n_sel, tk), 1) // SEL_BLOCK)
        expand = jnp.where(expand, 1.0, 0.0).astype(BF16)
        for g in range(NSA_GROUPS):
            q = q_ref[g, :, 0].reshape(NSA_REP * tq, NSA_DH)
            chosen = _dot(sel_ref[:, g * n_sel:(g + 1) * n_sel], expand)
            mask = jnp.where(causal, chosen, 0.0) > 0.5
            _flash_update(q, ks_ref[g, 0], vs_ref[g, 0], mask,
                          m_ref.at[g], l_ref.at[g], acc_ref.at[g])

    @pl.when((ki >= first_w) & (ki <= last))
    def _():
        mask = jnp.where(causal, qpos - kpos, WINDOW) < WINDOW
        for g in range(NSA_GROUPS):
            q = q_ref[g, :, 0].reshape(NSA_REP * tq, NSA_DH)
            _flash_update(q, kw_ref[g, 0], vw_ref[g, 0], mask,
                          m_ref.at[NSA_GROUPS + g], l_ref.at[NSA_GROUPS + g],
                          acc_ref.at[NSA_GROUPS + g])

    @pl.when(ki == last)
    def _():
        gates = gate_ref[...]
        ocmp = ocmp_ref[...]
        for g in range(NSA_GROUPS):
            o_s = acc_ref[g] / jnp.maximum(l_ref[g].reshape(NSA_REP * tq, 1), TINY)
            o_w = acc_ref[NSA_GROUPS + g] / jnp.maximum(
                l_ref[NSA_GROUPS + g].reshape(NSA_REP * tq, 1), TINY)
            for r in range(NSA_REP):
                hd = g * NSA_REP + r
                rows = slice(r * tq, (r + 1) * tq)
                cols = slice(hd * NSA_DH, (hd + 1) * NSA_DH)
                o_ref[:, cols] = (gates[:, 3 * hd:3 * hd + 1] * ocmp[:, cols]
                                  + gates[:, 3 * hd + 1:3 * hd + 2] * o_s[rows]
                                  + gates[:, 3 * hd + 2:3 * hd + 3] * o_w[rows])


def _nsa_sw_prompt(q, kvb, sel, o_cmp, gates, b, t, tq=256, tk=512):
    tq = min(tq, t)
    tk = min(tk, t)
    nq, nk = t // tq, t // tk
    q5 = q.reshape(NSA_GROUPS, NSA_REP, b, t, NSA_DH)
    kv4 = kvb.reshape(12, b, t, NSA_DH)

    def last_of(j):
        return (j * tq + tq - 1) // tk

    def slc_idx(blk):
        return lambda i, j, k: (blk, i, jnp.minimum(k, last_of(j)), 0)

    def win_idx(blk):
        def f(i, j, k):
            first = jnp.maximum(j * tq - (WINDOW - 1), 0) // tk
            return (blk, i, jnp.clip(k, first, last_of(j)), 0)
        return f

    kv_blk = (NSA_GROUPS, 1, tk, NSA_DH)
    row = lambda i, j, k: (i * nq + j, 0)
    return pl.pallas_call(
        functools.partial(_nsa_sw_prompt_kernel, tq=tq, tk=tk), grid=(b, nq, nk),
        in_specs=[pl.BlockSpec((NSA_GROUPS, NSA_REP, 1, tq, NSA_DH), lambda i, j, k: (0, 0, i, j, 0)),
                  pl.BlockSpec(kv_blk, slc_idx(2)), pl.BlockSpec(kv_blk, slc_idx(3)),
                  pl.BlockSpec(kv_blk, win_idx(4)), pl.BlockSpec(kv_blk, win_idx(5)),
                  pl.BlockSpec((tq, sel.shape[1]), row),
                  pl.BlockSpec((tq, o_cmp.shape[1]), row),
                  pl.BlockSpec((tq, gates.shape[1]), row)],
        out_specs=pl.BlockSpec((tq, NSA_HEADS * NSA_DH), row),
        out_shape=_sds((b * t, NSA_HEADS * NSA_DH)),
        scratch_shapes=[pltpu.VMEM((2 * NSA_GROUPS, NSA_REP, tq, 1), F32),
                        pltpu.VMEM((2 * NSA_GROUPS, NSA_REP, tq, 1), F32),
                        pltpu.VMEM((2 * NSA_GROUPS, NSA_REP * tq, NSA_DH), F32)],
        compiler_params=_params(("arbitrary", "arbitrary", "arbitrary"), 48),
        name="nsa_slc_win_prompt")(q5, kv4, kv4, kv4, kv4, sel, o_cmp, gates)


def _flash_update_t(q, k, v, mask_t, m_ref, acc_ref):
    s = _dot_t(k, q)
    if mask_t is not None:
        s = jnp.where(mask_t, s, NEG)
    m_prev = m_ref[...]
    m_new = jnp.maximum(m_prev, jnp.max(s, axis=0, keepdims=True))
    p = jnp.exp2(s - jnp.maximum(m_new, 0.1 * NEG))
    alpha = jnp.exp2(m_prev - m_new)
    v1 = jnp.concatenate([v, jnp.ones_like(v)], axis=1)
    acc_ref[...] = alpha * acc_ref[...] + _tdot(v1, p.astype(BF16))
    m_ref[...] = m_new


def _flash_result_t(acc):
    dh = acc.shape[0] // 2
    return acc[:dh] / jnp.maximum(acc[dh:dh + 1], TINY)


def _causal_steps(nq, tq, tk):
    qi, ki = [], []
    for j in range(nq):
        for k in range((j * tq + tq - 1) // tk + 1):
            qi.append(j)
            ki.append(k)
    return jnp.asarray(qi, I32), jnp.asarray(ki, I32)


def _nsa_swt_prompt_kernel(qi_ref, ki_ref, q_ref, ks_ref, vs_ref, kw_ref, vw_ref, sel_ref, ocmp_ref,
                           gate_ref, o_ref, m_ref, acc_ref, *, tq, tk):
    n = pl.program_id(1)
    qi = qi_ref[n]
    ki = ki_ref[n]
    q0 = qi * tq
    last = (q0 + tq - 1) // tk
    first_w = jnp.maximum(q0 - (WINDOW - 1), 0) // tk
    n_sel = sel_ref.shape[1] // NSA_GROUPS
    n_q = NSA_REP * tq

    @pl.when(ki == 0)
    def _():
        m_ref[...] = jnp.full(m_ref.shape, NEG, F32)
        acc_ref[...] = jnp.zeros(acc_ref.shape, F32)

    kpos = ki * tk + lax.broadcasted_iota(I32, (tk, tq), 0)
    qpos = q0 + lax.broadcasted_iota(I32, (tk, tq), 1)
    causal = kpos <= qpos
    heads = lambda mask: jnp.concatenate([mask] * NSA_REP, axis=1)

    expand = (lax.broadcasted_iota(I32, (tk, n_sel), 1)
              == (ki * tk + lax.broadcasted_iota(I32, (tk, n_sel), 0)) // SEL_BLOCK)
    expand = jnp.where(expand, 1.0, 0.0).astype(BF16)
    for g in range(NSA_GROUPS):
        q = q_ref[g, :, 0].reshape(n_q, NSA_DH)
        chosen = _dot_t(expand, sel_ref[:, g * n_sel:(g + 1) * n_sel])
        mask = jnp.where(causal, chosen, 0.0) > 0.5
        _flash_update_t(q, ks_ref[g, 0], vs_ref[g, 0], heads(mask), m_ref.at[g], acc_ref.at[g])

    @pl.when(ki >= first_w)
    def _():
        mask = heads(jnp.where(causal, qpos - kpos, WINDOW) < WINDOW)
        for g in range(NSA_GROUPS):
            q = q_ref[g, :, 0].reshape(n_q, NSA_DH)
            _flash_update_t(q, kw_ref[g, 0], vw_ref[g, 0], mask, m_ref.at[NSA_GROUPS + g],
                            acc_ref.at[NSA_GROUPS + g])

    @pl.when(ki == last)
    def _():
        gates = gate_ref[...]
        ocmp = ocmp_ref[...]
        for g in range(NSA_GROUPS):
            o_s = _flash_result_t(acc_ref[g])
            o_w = _flash_result_t(acc_ref[NSA_GROUPS + g])
            for r in range(NSA_REP):
                hd = g * NSA_REP + r
                lanes = slice(r * tq, (r + 1) * tq)
                cols = slice(hd * NSA_DH, (hd + 1) * NSA_DH)
                o_ref[:, cols] = (gates[:, 3 * hd:3 * hd + 1] * ocmp[:, cols]
                                  + gates[:, 3 * hd + 1:3 * hd + 2] * o_s[:, lanes].T
                                  + gates[:, 3 * hd + 2:3 * hd + 3] * o_w[:, lanes].T)


def _nsa_swt_prompt(q, kvb, sel, o_cmp, gates, b, t, tq=256, tk=1024):
    tq = min(tq, t)
    tk = min(tk, t)
    nq = t // tq
    qi, ki = _causal_steps(nq, tq, tk)
    q5 = q.reshape(NSA_GROUPS, NSA_REP, b, t, NSA_DH)
    kv4 = kvb.reshape(12, b, t, NSA_DH)

    def slc_idx(blk):
        return lambda i, n, qt, kt: (blk, i, kt[n], 0)

    def win_idx(blk):
        def f(i, n, qt, kt):
            first = jnp.maximum(qt[n] * tq - (WINDOW - 1), 0) // tk
            return (blk, i, jnp.maximum(kt[n], first), 0)
        return f

    kv_blk = (NSA_GROUPS, 1, tk, NSA_DH)
    row = lambda i, n, qt, kt: (i * nq + qt[n], 0)
    n_q = NSA_REP * tq
    grid_spec = pltpu.PrefetchScalarGridSpec(
        num_scalar_prefetch=2, grid=(b, qi.shape[0]),
        in_specs=[pl.BlockSpec((NSA_GROUPS, NSA_REP, 1, tq, NSA_DH),
                               lambda i, n, qt, kt: (0, 0, i, qt[n], 0)),
                  pl.BlockSpec(kv_blk, slc_idx(2)), pl.BlockSpec(kv_blk, slc_idx(3)),
                  pl.BlockSpec(kv_blk, win_idx(4)), pl.BlockSpec(kv_blk, win_idx(5)),
                  pl.BlockSpec((tq, sel.shape[1]), row),
                  pl.BlockSpec((tq, o_cmp.shape[1]), row),
                  pl.BlockSpec((tq, gates.shape[1]), row)],
        out_specs=pl.BlockSpec((tq, NSA_HEADS * NSA_DH), row),
        scratch_shapes=[pltpu.VMEM((2 * NSA_GROUPS, 1, n_q), F32),
                        pltpu.VMEM((2 * NSA_GROUPS, 2 * NSA_DH, n_q), F32)])
    return pl.pallas_call(
        functools.partial(_nsa_swt_prompt_kernel, tq=tq, tk=tk), grid_spec=grid_spec,
        out_shape=_sds((b * t, NSA_HEADS * NSA_DH)),
        compiler_params=_params(("arbitrary", "arbitrary"), 48),
        name="nsa_slc_win_prompt")(qi, ki, q5, kv4, kv4, kv4, kv4, sel, o_cmp, gates)


def _conv_prompt_kernel(u_ref, gb_ref, w_ref, o_ref, carry_ref):
    tm = u_ref.shape[0]

    @pl.when(pl.program_id(1) == 0)
    def _():
        carry_ref[...] = jnp.zeros(carry_ref.shape, F32)

    u = u_ref[...]
    prev = carry_ref[...]
    row = lax.broadcasted_iota(I32, u.shape, 0)
    u1 = jnp.where(row == 0, prev[7:8], pltpu.roll(u, 1, 0))
    u2 = jnp.where(row == 0, prev[6:7], jnp.where(row == 1, prev[7:8], pltpu.roll(u, 2, 0)))
    w = w_ref[...]
    o_ref[...] = gb_ref[...] * (w[0:1] * u2 + w[1:2] * u1 + w[2:3] * u)
    carry_ref[...] = u[tm - 8:]


def _conv_prompt(u, gb, conv_w, b, t, tm=1024):
    tm = min(tm, t)
    nt = t // tm
    c = u.shape[1]
    row = lambda i, j: (i * nt + j, 0)
    return pl.pallas_call(
        _conv_prompt_kernel, grid=(b, nt),
        in_specs=[pl.BlockSpec((tm, c), row), pl.BlockSpec((tm, c), row),
                  pl.BlockSpec(conv_w.shape, lambda i, j: (0, 0))],
        out_specs=pl.BlockSpec((tm, c), row), out_shape=_sds(u.shape),
        scratch_shapes=[pltpu.VMEM((8, c), F32)],
        compiler_params=_params(("arbitrary", "arbitrary"), 32), name="conv_prompt")(u, gb, conv_w)


def _matmul_res_body(n_a, rows, consts, outs):
    acc = rows[n_a][...]
    for k in range(n_a):
        acc = acc + _dot(rows[k][...].astype(BF16), consts[k][...])
    outs[0][...] = acc


def _matmul_res(acts, weights, res, name):
    r = res.shape[0]
    tm = _row_tile(r, 512)
    ws = [w.astype(BF16) for w in weights]
    return _rowwise(functools.partial(_matmul_res_body, len(acts)), list(acts) + [res], ws,
                    [_sds(res.shape)], tm, name)[0]


def _even_layer_prompt(x, b, t, g_mix, w_in, w_out, cmp_w, cmp_pe, conv_w):
    q, kvc, kvs, kvw, kvb, gates, gb, u = _even_proj(x, g_mix, w_in)
    cw, cpe = _compress_weights(cmp_w, cmp_pe)
    cmp = _compress_prompt(kvc, b, t, cw, cpe)
    o_cmp, sel = _nsa_cmp_prompt(q, cmp, b, t)
    o_nsa = _nsa_swt_prompt(q, kvb, sel, o_cmp, gates, b, t)
    o_conv = _conv_prompt(u, gb, conv_w, b, t)
    qc = NSA_HEADS * NSA_DH
    y = _matmul_res([o_nsa, o_conv], [w_out[:qc], w_out[qc:]], x, "even_out_proj")
    wb = min(WINDOW, t)
    caches = (kvc.reshape(b, t, 2, NSA_GROUPS, NSA_DH), kvs.reshape(b, t, 2, NSA_GROUPS, NSA_DH),
              kvw.reshape(b, t, 2, NSA_GROUPS, NSA_DH)[:, t - wb:],
              u.reshape(b, t, CONV_CH)[:, t - (CONV_W - 1):])
    return y, caches


def _channel_major(pool):
    nd = pool.ndim
    t = pool.transpose((0, 1) + tuple(range(3, nd)) + (2,))
    return t.reshape(pool.shape[0], pool.shape[1], -1, pool.shape[2])


def _page_copies(pool_ref, layer, pt_ref, batch, n_pages, buf_ref, slot, sem_ref):
    return [pltpu.make_async_copy(pool_ref.at[layer, pt_ref[batch * n_pages + p]],
                                  buf_ref.at[slot, :, pl.ds(p * PAGE_ROWS, PAGE_ROWS)],
                                  sem_ref.at[slot])
            for p in range(n_pages)]


def _paged_fetch(pool_ref, layer, pt_ref, n_pages, buf_ref, sem_ref):
    b = pl.program_id(0)
    slot = b % 2

    @pl.when(b == 0)
    def _():
        for c in _page_copies(pool_ref, layer, pt_ref, b, n_pages, buf_ref, slot, sem_ref):
            c.start()

    @pl.when(b + 1 < pl.num_programs(0))
    def _():
        for c in _page_copies(pool_ref, layer, pt_ref, b + 1, n_pages, buf_ref, 1 - slot, sem_ref):
            c.start()

    for c in _page_copies(pool_ref, layer, pt_ref, b, n_pages, buf_ref, slot, sem_ref):
        c.wait()
    return slot


def _online(state, s, mask, v, v_t=False):
    m_prev, l_prev, acc = state
    if mask is not None:
        s = jnp.where(mask, s, NEG)
    m_new = jnp.maximum(m_prev, jnp.max(s, axis=-1, keepdims=True))
    p = jnp.exp2(s - m_new)
    if mask is not None:
        p = jnp.where(mask, p, 0.0)
    alpha = jnp.exp2(m_prev - m_new)
    return (m_new, alpha * l_prev + jnp.sum(p, axis=-1, keepdims=True),
            alpha * acc + (_dot_t(p.astype(BF16), v) if v_t else _dot(p.astype(BF16), v)))


def _online_init(m, w):
    return (jnp.full((m, 1), NEG, F32), jnp.zeros((m, 1), F32), jnp.zeros((m, w), F32))


def _group_query_rows(q_ref, s):
    blocks = []
    for g in range(NSA_GROUPS):
        q = q_ref[g, :, 0].astype(F32).reshape(NSA_REP * s, NSA_DH)
        z = jnp.zeros_like(q)
        blocks.append(jnp.concatenate([q if j == g else z for j in range(2 * NSA_GROUPS)], axis=1))
    return jnp.concatenate(blocks, axis=0).astype(BF16)


def _group_rows(x):
    return jnp.concatenate([jnp.tile(xg, (NSA_REP, 1)) for xg in x], axis=0)


def _nsa_cmp_sample_kernel(pt_ref, pool_ref, q_ref, w_ref, pe_ref, ovl_ref, o_ref, sel_ref,
                           buf_ref, sem_ref, x_ref, bias_ref, *, layer, n_pages, s, topk):
    half = NSA_GROUPS * NSA_DH

    @pl.when(pl.program_id(0) == 0)
    def _():
        for kv in range(2):
            pb = None
            for jp in range(w_ref.shape[1]):
                for term in _split3(pe_ref[kv, jp]):
                    t = _dot(term, w_ref[kv, jp])
                    pb = t if pb is None else pb + t
            bias_ref[kv] = jnp.broadcast_to(pb[0:1, :half] + pb[1:2, half:], bias_ref.shape[1:])

    slot = _paged_fetch(pool_ref, layer, pt_ref, n_pages, buf_ref, sem_ref)
    past = n_pages * PAGE_ROWS
    n = past // CMP_STRIDE
    step = min(1024, past)
    for c0 in range(0, past, step):
        for kv in range(2):
            x_ref[kv, c0:c0 + step, :] = buf_ref[slot, kv * half:(kv + 1) * half, c0:c0 + step].T
    c = []
    for kv in range(2):
        p = None
        for jp in range(w_ref.shape[1]):
            x = jnp.concatenate(
                [x_ref.at[kv][pl.ds(2 * jp + jj, n, stride=CMP_STRIDE), :] for jj in range(2)],
                axis=1).astype(BF16)
            t = _dot(x, w_ref[kv, jp])
            p = t if p is None else p + t
        c.append(p[:, :half] + pltpu.roll(p[:, half:], n - 1, 0) + bias_ref[kv, 0:1])
    n_sel = ovl_ref.shape[1]
    qpos = n_pages * PAGE_ROWS + lax.broadcasted_iota(I32, (s, 1), 0)
    for g in range(NSA_GROUPS):
        q = q_ref[g, :, 0].astype(F32).reshape(NSA_REP * s, NSA_DH).astype(BF16)
        ck = c[0][:, g * NSA_DH:(g + 1) * NSA_DH].astype(BF16)
        cv = c[1][:, g * NSA_DH:(g + 1) * NSA_DH].astype(BF16)
        o, sel = _cmp_branch(q, ck, cv, qpos, ovl_ref[...], topk)
        for r in range(NSA_REP):
            hd = g * NSA_REP + r
            o_ref[0, :, hd * NSA_DH:(hd + 1) * NSA_DH] = o[r * s:(r + 1) * s]
        sel_ref[0, :, g * n_sel:(g + 1) * n_sel] = sel.astype(BF16)


PAGE_ROWS = 128
PAGE_CHUNKS = PAGE_ROWS // CMP_STRIDE


def _nsa_cmp_sample(q, pool, layer, page_table, w, pe, sb, s):
    n_pages = page_table.shape[1]
    past = n_pages * PAGE_ROWS
    assert past % SEL_BLOCK == 0 and s <= CMP_STRIDE and pool.shape[2] == PAGE_ROWS
    n = past // CMP_STRIDE
    n_sel = past // SEL_BLOCK
    ovl = _overlap_matrix(n, n_sel)
    pool4 = _channel_major(pool)
    q5 = q.reshape(NSA_GROUPS, NSA_REP, sb, s, NSA_DH)
    kern = functools.partial(_nsa_cmp_sample_kernel, layer=layer, n_pages=n_pages, s=s,
                             topk=SEL_TOPK - 1)
    grid_spec = pltpu.PrefetchScalarGridSpec(
        num_scalar_prefetch=1, grid=(sb,),
        in_specs=[pl.BlockSpec(memory_space=pl.ANY),
                  pl.BlockSpec((NSA_GROUPS, NSA_REP, 1, s, NSA_DH), lambda i, pt: (0, 0, i, 0, 0)),
                  pl.BlockSpec(w.shape, lambda i, pt: (0, 0, 0, 0)),
                  pl.BlockSpec(pe.shape, lambda i, pt: (0, 0, 0, 0)),
                  pl.BlockSpec(ovl.shape, lambda i, pt: (0, 0))],
        out_specs=[pl.BlockSpec((1, s, NSA_HEADS * NSA_DH), lambda i, pt: (i, 0, 0)),
                   pl.BlockSpec((1, s, NSA_GROUPS * n_sel), lambda i, pt: (i, 0, 0))],
        scratch_shapes=[pltpu.VMEM((2, NSA_KV_ROW, past), F32), pltpu.SemaphoreType.DMA((2,)),
                        pltpu.VMEM((2, past, NSA_GROUPS * NSA_DH), F32),
                        pltpu.VMEM((2, 8, NSA_GROUPS * NSA_DH), F32)])
    return pl.pallas_call(
        kern, grid_spec=grid_spec,
        out_shape=[_sds((sb, s, NSA_HEADS * NSA_DH)), _sds((sb, s, NSA_GROUPS * n_sel), BF16)],
        compiler_params=_params(("arbitrary",), 56), name="nsa_cmp_sample")(
            page_table.reshape(-1), pool4, q5, w, pe, ovl)


def _nsa_sw_sample_kernel(pt_ref, pool_ref, q_ref, knew_ref, wnew_ref, win_ref, sel_ref, ocmp_ref,
                          gate_ref, o_ref, buf_ref, sem_ref, *, layer, n_pages, s, chunk):
    slot = _paged_fetch(pool_ref, layer, pt_ref, n_pages, buf_ref, sem_ref)
    past = n_pages * PAGE_ROWS
    n_sel = sel_ref.shape[2] // NSA_GROUPS
    m = NSA_HEADS * s
    vrow = NSA_KV_ROW // 2
    qm = _group_query_rows(q_ref, s)
    sq = lax.broadcasted_iota(I32, (s, s), 0)
    sk = lax.broadcasted_iota(I32, (s, s), 1)
    new_mask = _group_rows([sk <= sq] * NSA_GROUPS)

    state = _online_init(m, vrow)
    for c0 in range(0, past, chunk):
        kv = buf_ref[slot, :, c0:c0 + chunk].astype(BF16)
        expand = (lax.broadcasted_iota(I32, (n_sel, chunk), 0)
                  == (c0 + lax.broadcasted_iota(I32, (n_sel, chunk), 1)) // SEL_BLOCK)
        expand = jnp.where(expand, 1.0, 0.0).astype(BF16)
        chosen = [_dot(sel_ref[0, :, g * n_sel:(g + 1) * n_sel], expand) > 0.5
                  for g in range(NSA_GROUPS)]
        state = _online(state, _dot(qm, kv), _group_rows(chosen), kv[vrow:], v_t=True)
    knew = knew_ref[0].astype(BF16)
    state = _online(state, _dot_t(qm, knew), new_mask, knew[:, vrow:])
    o_slc = state[2] / jnp.maximum(state[1], TINY)

    wb = win_ref.shape[2]
    win = win_ref[0].astype(BF16)
    qrow = lax.broadcasted_iota(I32, (s, wb), 0)
    dist = qrow + wb - lax.broadcasted_iota(I32, (s, wb), 1)
    old_mask = _group_rows([(dist >= 0) & (dist < WINDOW)] * NSA_GROUPS)
    state = _online(_online_init(m, vrow), _dot(qm, win), old_mask, win[vrow:], v_t=True)
    wnew = wnew_ref[0].astype(BF16)
    state = _online(state, _dot_t(qm, wnew), new_mask, wnew[:, vrow:])
    o_win = state[2] / jnp.maximum(state[1], TINY)

    gates = gate_ref[0]
    ocmp = ocmp_ref[0]
    for g in range(NSA_GROUPS):
        vcols = slice(g * NSA_DH, (g + 1) * NSA_DH)
        for r in range(NSA_REP):
            hd = g * NSA_REP + r
            rows = slice(hd * s, (hd + 1) * s)
            cols = slice(hd * NSA_DH, (hd + 1) * NSA_DH)
            o_ref[0, :, cols] = (gates[:, 3 * hd:3 * hd + 1] * ocmp[:, cols]
                                 + gates[:, 3 * hd + 1:3 * hd + 2] * o_slc[rows, vcols]
                                 + gates[:, 3 * hd + 2:3 * hd + 3] * o_win[rows, vcols])


def _nsa_sw_sample(q, pool, layer, page_table, kvs_new, kvw_new, win_buf, sel, o_cmp, gates, sb, s):
    n_pages = page_table.shape[1]
    past = n_pages * PAGE_ROWS
    wb = win_buf.shape[1]
    assert wb == WINDOW and s <= SEL_BLOCK and past % SEL_BLOCK == 0
    pool4 = _channel_major(pool)
    q5 = q.reshape(NSA_GROUPS, NSA_REP, sb, s, NSA_DH)
    per_b = lambda a: a.reshape(sb, s, a.shape[-1])
    blk = lambda a: pl.BlockSpec((1,) + a.shape[1:], lambda i, pt: (i, 0, 0))
    win_t = _channel_major(win_buf[None])[0]
    rows = [per_b(kvs_new), per_b(kvw_new), win_t, sel, o_cmp, per_b(gates)]
    kern = functools.partial(_nsa_sw_sample_kernel, layer=layer, n_pages=n_pages, s=s,
                             chunk=min(2048, past))
    grid_spec = pltpu.PrefetchScalarGridSpec(
        num_scalar_prefetch=1, grid=(sb,),
        in_specs=[pl.BlockSpec(memory_space=pl.ANY),
                  pl.BlockSpec((NSA_GROUPS, NSA_REP, 1, s, NSA_DH), lambda i, pt: (0, 0, i, 0, 0))]
        + [blk(a) for a in rows],
        out_specs=pl.BlockSpec((1, s, NSA_HEADS * NSA_DH), lambda i, pt: (i, 0, 0)),
        scratch_shapes=[pltpu.VMEM((2, NSA_KV_ROW, past), F32), pltpu.SemaphoreType.DMA((2,))])
    return pl.pallas_call(
        kern, grid_spec=grid_spec, out_shape=_sds((sb, s, NSA_HEADS * NSA_DH)),
        compiler_params=_params(("arbitrary",), 56), name="nsa_slc_win_sample")(
            page_table.reshape(-1), pool4, q5, *rows)


def _conv_sample_kernel(u_ref, gb_ref, cb_ref, w_ref, o_ref):
    u = u_ref[...]
    cb = cb_ref[...]
    row = lax.broadcasted_iota(I32, u.shape, 1)
    u1 = jnp.where(row == 0, cb[:, 1:2], pltpu.roll(u, 1, 1))
    u2 = jnp.where(row == 0, cb[:, 0:1], jnp.where(row == 1, cb[:, 1:2], pltpu.roll(u, 2, 1)))
    w = w_ref[...]
    o_ref[...] = gb_ref[...] * (w[0:1][None] * u2 + w[1:2][None] * u1 + w[2:3][None] * u)


def _conv_sample(u, gb, conv_buf, conv_w, sb, s):
    assert s >= CONV_W - 1
    c = u.shape[1]
    full = lambda shape: pl.BlockSpec(shape, lambda i, n=len(shape): (0,) * n)
    return pl.pallas_call(
        _conv_sample_kernel, grid=(1,),
        in_specs=[full((sb, s, c)), full((sb, s, c)), full(conv_buf.shape), full(conv_w.shape)],
        out_specs=full((sb, s, c)), out_shape=_sds((sb, s, c)),
        compiler_params=_params(("arbitrary",), 32), name="conv_sample")(
            u.reshape(sb, s, c), gb.reshape(sb, s, c), conv_buf, conv_w).reshape(sb * s, c)


def _even_layer_sample(x, sb, s, layer, pool_cmp, pool_slc, win_buf, conv_buf, page_table,
                       g_mix, w_in, w_out, cmp_w, cmp_pe, conv_w):
    q, kvc, kvs, kvw, _, gates, gb, u = _even_proj(x, g_mix, w_in)
    cw, cpe = _compress_pair_weights(cmp_w, cmp_pe)
    o_cmp, sel = _nsa_cmp_sample(q, pool_cmp, layer, page_table, cw, cpe, sb, s)
    o_nsa = _nsa_sw_sample(q, pool_slc, layer, page_table, kvs, kvw, win_buf, sel, o_cmp, gates,
                           sb, s)
    o_conv = _conv_sample(u, gb, conv_buf, conv_w, sb, s)
    qc = NSA_HEADS * NSA_DH
    y = _matmul_res([o_nsa.reshape(sb * s, qc), o_conv], [w_out[:qc], w_out[qc:]], x,
                    "even_out_proj")
    five = lambda a: a.reshape(sb, s, 2, NSA_GROUPS, NSA_DH)
    win_all = jnp.concatenate([win_buf, five(kvw)], axis=1)[:, s:]
    conv_all = jnp.concatenate([conv_buf, u.reshape(sb, s, CONV_CH)], axis=1)[:, s:]
    return y, (five(kvc), five(kvs), win_all, conv_all)


def _rot_half_cols(w):
    half = w.shape[-1] // 2
    return jnp.concatenate([-w[..., half:], w[..., :half]], axis=-1)


def _rope_tables(pos):
    half = ROPE // 2
    inv = ROPE_THETA ** (-jnp.arange(half, dtype=F32) / half)
    ang = pos.astype(F32)[:, None] * inv[None, :]
    cos, sin = jnp.cos(ang), jnp.sin(ang)
    cos32 = jnp.concatenate([cos, cos], axis=1)
    sin32 = jnp.concatenate([sin, sin], axis=1)
    n = pos.shape[0]
    pad = MLA_QK - NOPE - ROPE
    cosq = jnp.concatenate([jnp.ones((n, NOPE), F32), cos32, jnp.zeros((n, pad), F32)], axis=1)
    sinq = jnp.concatenate([jnp.zeros((n, NOPE), F32), sin32, jnp.zeros((n, pad), F32)], axis=1)
    return cos32, sin32, cosq, sinq


def _mla_weights(w_in, w_uq, w_uk):
    wcq = w_in[:, :Q_LORA]
    wckv = w_in[:, Q_LORA:Q_LORA + KV_LORA]
    wkpe = w_in[:, Q_LORA + KV_LORA:]
    uq = w_uq.reshape(Q_LORA, MLA_HEADS, NOPE + ROPE)
    pad = jnp.zeros((Q_LORA, MLA_HEADS, MLA_QK - NOPE - ROPE), F32)
    wq = jnp.concatenate([uq, pad], axis=-1).reshape(Q_LORA, MLA_HEADS * MLA_QK)
    wq_rot = jnp.concatenate([jnp.zeros((Q_LORA, MLA_HEADS, NOPE), F32),
                              _rot_half_cols(uq[..., NOPE:]), pad], axis=-1).reshape(wq.shape)
    kpad = jnp.zeros((KV_LORA, MLA_HEADS, MLA_QK - NOPE), F32)
    wk_lat = jnp.concatenate([w_uk, kpad], axis=-1).reshape(KV_LORA, MLA_HEADS * MLA_QK)
    eye = jnp.broadcast_to(jnp.eye(ROPE, dtype=F32)[:, None, :], (ROPE, MLA_HEADS, ROPE))
    wk_pe = jnp.concatenate([jnp.zeros((ROPE, MLA_HEADS, NOPE), F32), eye,
                             jnp.zeros((ROPE, MLA_HEADS, MLA_QK - NOPE - ROPE), F32)],
                            axis=-1).reshape(ROPE, MLA_HEADS * MLA_QK)
    bf = lambda a: a.astype(BF16)
    return dict(wcq=bf(wcq), wckv=bf(wckv), wkpe=bf(wkpe), wkpe_rot=bf(_rot_half_cols(wkpe)),
                wq=bf(wq), wq_rot=bf(wq_rot), wk_lat=bf(wk_lat), wk_pe=bf(wk_pe))


def _mla_latent(x_ref, cos_ref, sin_ref, g_ref, qn_ref, kvn_ref, wcq_ref, wckv_ref, wkpe_ref,
                wkrot_ref, lat_ref):
    h = _rms(x_ref[...], g_ref[...]).astype(BF16)
    cq = _rms(_dot(h, wcq_ref[...]), qn_ref[...]).astype(BF16)
    ckv = _rms(_dot(h, wckv_ref[...]), kvn_ref[...])
    kpe = _dot(h, wkpe_ref[...]) * cos_ref[...] + _dot(h, wkrot_ref[...]) * sin_ref[...]
    lat_ref[:, :KV_LORA] = ckv
    lat_ref[:, KV_LORA:] = kpe
    return cq, ckv.astype(BF16), kpe.astype(BF16)


def _mla_queries(cq, wq_ref, wqrot_ref, cosq, sinq, head):
    cols = slice(head * MLA_QK, (head + 1) * MLA_QK)
    q = _dot(cq, wq_ref[:, cols]) * cosq + _dot(cq, wqrot_ref[:, cols]) * sinq
    return q * ((NOPE + ROPE) ** -0.5 * LOG2E)


def _mla_proj_prompt_body(rows, consts, outs):
    x_ref, cos_ref, sin_ref, cosq_ref, sinq_ref = rows
    (g_ref, qn_ref, kvn_ref, wcq_ref, wckv_ref, wkpe_ref, wkrot_ref, wq_ref, wqrot_ref,
     wklat_ref, wkpe2_ref, wuv_ref) = consts
    lat_ref, q_ref, k_ref, v_ref = outs
    cq, ckv, kpe = _mla_latent(x_ref, cos_ref, sin_ref, g_ref, qn_ref, kvn_ref, wcq_ref,
                               wckv_ref, wkpe_ref, wkrot_ref, lat_ref)
    cosq, sinq = cosq_ref[...], sinq_ref[...]
    for hd in range(MLA_HEADS):
        cols = slice(hd * MLA_QK, (hd + 1) * MLA_QK)
        q_ref[:, cols] = _mla_queries(cq, wq_ref, wqrot_ref, cosq, sinq, hd).astype(BF16)
    k_ref[...] = (_dot(ckv, wklat_ref[...]) + _dot(kpe, wkpe2_ref[...])).astype(BF16)
    v_ref[...] = _dot(ckv, wuv_ref[...]).astype(BF16)


def _mla_proj_sample_body(rows, consts, outs):
    x_ref, cos_ref, sin_ref, cosq_ref, sinq_ref = rows
    (g_ref, qn_ref, kvn_ref, wcq_ref, wckv_ref, wkpe_ref, wkrot_ref, wq_ref, wqrot_ref,
     wukt_ref) = consts
    lat_ref, q_ref = outs
    cq, _, _ = _mla_latent(x_ref, cos_ref, sin_ref, g_ref, qn_ref, kvn_ref, wcq_ref, wckv_ref,
                           wkpe_ref, wkrot_ref, lat_ref)
    cosq, sinq = cosq_ref[...], sinq_ref[...]
    for hd in range(MLA_HEADS):
        q = _mla_queries(cq, wq_ref, wqrot_ref, cosq, sinq, hd)
        q_ref[hd, :, :KV_LORA] = _dot(q[:, :NOPE].astype(BF16), wukt_ref[hd])
        q_ref[hd, :, KV_LORA:] = q[:, NOPE:NOPE + ROPE]


def _mla_proj(x, pos, g_mix, q_norm, kv_norm, mw, extra, body, outs, name):
    r, d = x.shape
    tm = _row_tile(pos.shape[0], 512)
    cos32, sin32, cosq, sinq = _rope_tables(pos)
    n_per = pos.shape[0] // tm
    periods = {k: n_per for k in (1, 2, 3, 4)}
    consts = [g_mix.reshape(1, d), q_norm.reshape(1, -1), kv_norm.reshape(1, -1), mw["wcq"],
              mw["wckv"], mw["wkpe"], mw["wkpe_rot"], mw["wq"], mw["wq_rot"]] + extra
    return _rowwise(body, [x, cos32, sin32, cosq, sinq], consts, outs, tm, name, periods=periods)


def _mla_flash_kernel(qi_ref, ki_ref, q_ref, k_ref, v_ref, o_ref, m_ref, l_ref, acc_ref, *, tq, tk):
    n = pl.program_id(2)
    qi = qi_ref[n]
    ki = ki_ref[n]
    last = (qi * tq + tq - 1) // tk

    @pl.when(ki == 0)
    def _():
        m_ref[...] = jnp.full(m_ref.shape, NEG, F32)
        l_ref[...] = jnp.zeros(l_ref.shape, F32)
        acc_ref[...] = jnp.zeros(acc_ref.shape, F32)

    def step(masked):
        if masked:
            mask = (ki * tk + lax.broadcasted_iota(I32, (tq, tk), 1)
                    <= qi * tq + lax.broadcasted_iota(I32, (tq, tk), 0))
        for j in range(2):
            s = _dot_t(q_ref[:, j * MLA_QK:(j + 1) * MLA_QK], k_ref[:, j * MLA_QK:(j + 1) * MLA_QK])
            if masked:
                s = jnp.where(mask, s, NEG)
            m_prev = m_ref[j]
            m_new = jnp.maximum(m_prev, jnp.max(s, axis=-1, keepdims=True))
            p = jnp.exp(s - m_new)
            if masked:
                p = jnp.where(mask, p, 0.0)
            alpha = jnp.exp(m_prev - m_new)
            l_ref[j] = alpha * l_ref[j] + jnp.sum(p, axis=-1, keepdims=True)
            acc_ref[j] = alpha * acc_ref[j] + _dot(p.astype(BF16), v_ref[:, j * V_DIM:(j + 1) * V_DIM])
            m_ref[j] = m_new

    needs_mask = ki * tk + tk - 1 > qi * tq

    @pl.when(needs_mask)
    def _():
        step(True)

    @pl.when(jnp.logical_not(needs_mask))
    def _():
        step(False)

    @pl.when(ki == last)
    def _():
        for j in range(2):
            o_ref[:, j * V_DIM:(j + 1) * V_DIM] = acc_ref[j] / jnp.maximum(l_ref[j], TINY)


def _causal_steps(nq, tq, tk):
    qi, ki = [], []
    for j in range(nq):
        for k in range((j * tq + tq - 1) // tk + 1):
            qi.append(j)
            ki.append(k)
    return jnp.asarray(qi, I32), jnp.asarray(ki, I32)


def _mla_flash(q, k, v, b, t, tq=512, tk=512):
    tq, tk = min(tq, t), min(tk, t)
    nq, nk = t // tq, t // tk
    qi, ki = _causal_steps(nq, tq, tk)
    pairs = MLA_HEADS // 2
    grid_spec = pltpu.PrefetchScalarGridSpec(
        num_scalar_prefetch=2, grid=(b, pairs, qi.shape[0]),
        in_specs=[pl.BlockSpec((tq, 2 * MLA_QK), lambda i, h, n, qt, kt: (i * nq + qt[n], h)),
                  pl.BlockSpec((tk, 2 * MLA_QK), lambda i, h, n, qt, kt: (i * nk + kt[n], h)),
                  pl.BlockSpec((tk, 2 * V_DIM), lambda i, h, n, qt, kt: (i * nk + kt[n], h))],
        out_specs=pl.BlockSpec((tq, 2 * V_DIM), lambda i, h, n, qt, kt: (i * nq + qt[n], h)),
        scratch_shapes=[pltpu.VMEM((2, tq, 1), F32), pltpu.VMEM((2, tq, 1), F32),
                        pltpu.VMEM((2, tq, V_DIM), F32)])
    return pl.pallas_call(
        functools.partial(_mla_flash_kernel, tq=tq, tk=tk), grid_spec=grid_spec,
        out_shape=_sds((b * t, MLA_HEADS * V_DIM)),
        compiler_params=_params(("arbitrary", "arbitrary", "arbitrary"), 48),
        name="mla_flash_prompt")(qi, ki, q, k, v)


def _mla_flash_t_kernel(qi_ref, ki_ref, q_ref, k_ref, v_ref, o_ref, m_ref, acc_ref, *, tq, tk):
    n = pl.program_id(2)
    qi = qi_ref[n]
    ki = ki_ref[n]
    last = (qi * tq + tq - 1) // tk

    @pl.when(ki == 0)
    def _():
        m_ref[...] = jnp.full(m_ref.shape, NEG, F32)
        acc_ref[...] = jnp.zeros(acc_ref.shape, F32)

    def step(masked):
        mask = None
        if masked:
            mask = (ki * tk + lax.broadcasted_iota(I32, (tk, tq), 0)
                    <= qi * tq + lax.broadcasted_iota(I32, (tk, tq), 1))
        for j in range(2):
            qk = slice(j * MLA_QK, (j + 1) * MLA_QK)
            _flash_update_t(q_ref[:, qk], k_ref[:, qk], v_ref[:, j * V_DIM:(j + 1) * V_DIM], mask,
                            m_ref.at[j], acc_ref.at[j])

    needs_mask = ki * tk + tk - 1 > qi * tq

    @pl.when(needs_mask)
    def _():
        step(True)

    @pl.when(jnp.logical_not(needs_mask))
    def _():
        step(False)

    @pl.when(ki == last)
    def _():
        for j in range(2):
            o_ref[:, j * V_DIM:(j + 1) * V_DIM] = _flash_result_t(acc_ref[j]).T


def _mla_flash_t(q, k, v, b, t, tq=1024, tk=1024):
    tq, tk = min(tq, t), min(tk, t)
    nq, nk = t // tq, t // tk
    qi, ki = _causal_steps(nq, tq, tk)
    pairs = MLA_HEADS // 2
    grid_spec = pltpu.PrefetchScalarGridSpec(
        num_scalar_prefetch=2, grid=(b, pairs, qi.shape[0]),
        in_specs=[pl.BlockSpec((tq, 2 * MLA_QK), lambda i, h, n, qt, kt: (i * nq + qt[n], h)),
                  pl.BlockSpec((tk, 2 * MLA_QK), lambda i, h, n, qt, kt: (i * nk + kt[n], h)),
                  pl.BlockSpec((tk, 2 * V_DIM), lambda i, h, n, qt, kt: (i * nk + kt[n], h))],
        out_specs=pl.BlockSpec((tq, 2 * V_DIM), lambda i, h, n, qt, kt: (i * nq + qt[n], h)),
        scratch_shapes=[pltpu.VMEM((2, 1, tq), F32), pltpu.VMEM((2, 2 * V_DIM, tq), F32)])
    return pl.pallas_call(
        functools.partial(_mla_flash_t_kernel, tq=tq, tk=tk), grid_spec=grid_spec,
        out_shape=_sds((b * t, MLA_HEADS * V_DIM)),
        compiler_params=_params(("arbitrary", "arbitrary", "arbitrary"), 48),
        name="mla_flash_prompt")(qi, ki, q, k, v)


def _odd_layer_prompt(x, b, t, g_mix, w_in, q_norm, w_uq, kv_norm, w_uk, w_uv, w_out):
    mw = _mla_weights(w_in, w_uq, w_uk)
    r = x.shape[0]
    outs = [_sds((r, LAT)), _sds((r, MLA_HEADS * MLA_QK), BF16), _sds((r, MLA_HEADS * MLA_QK), BF16),
            _sds((r, MLA_HEADS * V_DIM), BF16)]
    extra = [mw["wk_lat"], mw["wk_pe"], w_uv.reshape(KV_LORA, MLA_HEADS * V_DIM).astype(BF16)]
    lat, q, k, v = _mla_proj(x, jnp.arange(t, dtype=I32), g_mix, q_norm, kv_norm, mw, extra,
                             _mla_proj_prompt_body, outs, "mla_proj_prompt")
    o = _mla_flash_t(q, k, v, b, t)
    y = _matmul_res([o], [w_out], x, "mla_out_proj")
    return y, lat.reshape(b, t, LAT)


def _mla_sample_kernel(pt_ref, pool_ref, q_ref, new_ref, o_ref, buf_ref, sem_ref, *,
                       layer, n_pages, s, chunk):
    slot = _paged_fetch(pool_ref, layer, pt_ref, n_pages, buf_ref, sem_ref)
    past = n_pages * PAGE_ROWS
    m = MLA_HEADS * s
    q = q_ref[:, 0].reshape(m, LAT).astype(BF16)
    state = _online_init(m, KV_LORA)
    for c0 in range(0, past, chunk):
        kv = buf_ref[slot, :, c0:c0 + chunk].astype(BF16)
        state = _online(state, _dot(q, kv), None, kv[:KV_LORA], v_t=True)
    new = new_ref[0].astype(BF16)
    causal = (lax.broadcasted_iota(I32, (s, s), 1) <= lax.broadcasted_iota(I32, (s, s), 0))
    state = _online(state, _dot_t(q, new), jnp.tile(causal, (MLA_HEADS, 1)), new[:, :KV_LORA])
    o = state[2] / jnp.maximum(state[1], TINY)
    o_ref[:, 0] = o.reshape(MLA_HEADS, s, KV_LORA)


def _mla_sample_attend(q, pool, layer, page_table, lat_new, sb, s):
    n_pages = page_table.shape[1]
    past = n_pages * PAGE_ROWS
    q4 = q.reshape(MLA_HEADS, sb, s, LAT)
    kern = functools.partial(_mla_sample_kernel, layer=layer, n_pages=n_pages, s=s,
                             chunk=min(2048, past))
    grid_spec = pltpu.PrefetchScalarGridSpec(
        num_scalar_prefetch=1, grid=(sb,),
        in_specs=[pl.BlockSpec(memory_space=pl.ANY),
                  pl.BlockSpec((MLA_HEADS, 1, s, LAT), lambda i, pt: (0, i, 0, 0)),
                  pl.BlockSpec((1, s, LAT), lambda i, pt: (i, 0, 0))],
        out_specs=pl.BlockSpec((MLA_HEADS, 1, s, KV_LORA), lambda i, pt: (0, i, 0, 0)),
        scratch_shapes=[pltpu.VMEM((2, LAT, past), F32), pltpu.SemaphoreType.DMA((2,))])
    o = pl.pallas_call(
        kern, grid_spec=grid_spec, out_shape=_sds((MLA_HEADS, sb, s, KV_LORA)),
        compiler_params=_params(("arbitrary",), 56), name="mla_paged_sample")(
            page_table.reshape(-1), _channel_major(pool), q4, lat_new.reshape(sb, s, LAT))
    return o.reshape(MLA_HEADS, sb * s, KV_LORA)


def _mla_out_sample_body(rows, consts, outs):
    o_ref, res_ref = rows
    wuv_ref, wout_ref = consts
    acc = res_ref[...]
    for hd in range(MLA_HEADS):
        o_h = _dot(o_ref[hd].astype(BF16), wuv_ref[hd])
        acc = acc + _dot(o_h.astype(BF16), wout_ref[hd])
    outs[0][...] = acc


def _odd_layer_sample(x, sb, s, layer, pool, page_table, g_mix, w_in, q_norm, w_uq, kv_norm, w_uk,
                      w_uv, w_out):
    mw = _mla_weights(w_in, w_uq, w_uk)
    r = x.shape[0]
    past = page_table.shape[1] * PAGE_ROWS
    pos = past + jnp.arange(_row_tile(r, 512), dtype=I32) % s
    outs = [_sds((r, LAT)), _sds((MLA_HEADS, r, LAT))]
    wukt = w_uk.transpose(1, 2, 0).astype(BF16)
    lat, q = _mla_proj(x, pos, g_mix, q_norm, kv_norm, mw, [wukt],
                       _mla_proj_sample_body, outs, "mla_proj_sample")
    o_lat = _mla_sample_attend(q, pool, layer, page_table, lat, sb, s)
    wuv = w_uv.transpose(1, 0, 2).astype(BF16)
    wout = w_out.reshape(MLA_HEADS, V_DIM, -1).astype(BF16)
    y = _rowwise(_mla_out_sample_body, [o_lat, x], [wuv, wout], [_sds(x.shape)],
                 _row_tile(r, 512), "mla_out_sample")[0]
    return y, lat.reshape(sb, s, LAT)


def _plain_matmul_body(rows, consts, outs):
    outs[0][...] = _dot(rows[0][...].astype(BF16), consts[0][...])


def _mem_q_body(rows, consts, outs):
    g_ref, w_ref = consts
    h = _rms(rows[0][...], g_ref[...]).astype(BF16)
    dh = w_ref.shape[1] // MEM_HEADS
    outs[0][...] = (_dot(h, w_ref[...]) * (dh ** -0.5 * LOG2E)).astype(BF16)


def _mem_attn_kernel(q_ref, kv_ref, o_ref, *, split):
    d = q_ref.shape[2]
    dh = d // MEM_HEADS
    for hd in range(MEM_HEADS):
        if split:
            parts = dh // kv_ref.shape[5]
            rows = lambda kv: jnp.concatenate(
                [kv_ref[0, 0, :, kv, part * MEM_HEADS + hd, :] for part in range(parts)],
                axis=1).astype(BF16)
            k, v = rows(0), rows(1)
        else:
            k = kv_ref[0, :, hd * dh:(hd + 1) * dh].astype(BF16)
            v = kv_ref[0, :, d + hd * dh:d + (hd + 1) * dh].astype(BF16)
        s = _dot_t(q_ref[0, :, hd * dh:(hd + 1) * dh], k)
        p = jnp.exp2(s - jnp.max(s, axis=-1, keepdims=True))
        p = p / jnp.sum(p, axis=-1, keepdims=True)
        o_ref[0, :, hd * dh:(hd + 1) * dh] = _dot(p.astype(BF16), v).astype(BF16)


def _mem_layer(y, nb, kv, g_mem, w_q, w_o, layer=None):
    r, d = y.shape
    rows = r // nb
    q = _rowwise(_mem_q_body, [y], [g_mem.reshape(1, d), w_q.astype(BF16)], [_sds((r, d), BF16)],
                 _row_tile(r, 512), "mem_q_proj")[0]
    tq = _row_tile(rows, 512)
    nt = rows // tq
    if layer is None:
        kv_spec = pl.BlockSpec((1,) + kv.shape[1:], lambda i, j: (i, 0, 0))
    else:
        nl, _, mem, two, heads, dh = kv.shape
        parts = dh // V7X_LANES
        kv = kv.reshape(nl, nb, mem, two, heads, parts, V7X_LANES).transpose(0, 1, 2, 3, 5, 4, 6)
        kv = kv.reshape(nl, nb, mem, two, parts * heads, V7X_LANES)
        kv_spec = pl.BlockSpec((1, 1) + kv.shape[2:], lambda i, j: (layer, i, 0, 0, 0, 0))
    o = pl.pallas_call(
        functools.partial(_mem_attn_kernel, split=layer is not None), grid=(nb, nt),
        in_specs=[pl.BlockSpec((1, tq, d), lambda i, j: (i, j, 0)), kv_spec],
        out_specs=pl.BlockSpec((1, tq, d), lambda i, j: (i, j, 0)),
        out_shape=_sds((nb, rows, d), BF16),
        compiler_params=_params(("arbitrary", "arbitrary"), 48), name="mem_attention")(
            q.reshape(nb, rows, d), kv)
    return _matmul_res([o.reshape(r, d)], [w_o], y, "mem_out_proj")


def _mem_project(mem, w_kv):
    nb, m, d = mem.shape
    kv = _rowwise(_plain_matmul_body, [mem.reshape(nb * m, d)], [w_kv.astype(BF16)],
                  [_sds((nb * m, w_kv.shape[1]))], _row_tile(nb * m, 512), "mem_kv_proj")[0]
    return kv.reshape(nb, m, w_kv.shape[1])


def _route(x, wr_ref, br_ref):
    xh, xm, _ = _split3(x)
    w0, w1 = wr_ref[0], wr_ref[1]
    logit = _dot(xh, w0) + (_dot(xh, w1) + _dot(xm, w0)) + br_ref[...]
    lane = lax.broadcasted_iota(I32, logit.shape, 1).astype(F32)
    big = float(ROUTE_LANES)
    is_g = lane < N_GROUPS
    gl = jnp.where(is_g, logit, -jnp.inf)
    gmax = jnp.max(gl, axis=-1, keepdims=True)
    gidx = jnp.min(jnp.where(gl == gmax, lane, big), axis=-1, keepdims=True)
    g_gate = 1.0 / jnp.sum(jnp.where(is_g, jnp.exp(gl - gmax), 0.0), axis=-1, keepdims=True)
    lo = N_GROUPS + gidx * EXP_PER_GROUP
    in_grp = (lane >= lo) & (lane < lo + EXP_PER_GROUP)
    el = jnp.where(in_grp, logit, -jnp.inf)
    ee = jnp.where(in_grp, jnp.exp(el - jnp.max(el, axis=-1, keepdims=True)), 0.0)
    prob = ee / jnp.sum(ee, axis=-1, keepdims=True)
    cand = jnp.where(in_grp, prob, -1.0)
    p1 = jnp.max(cand, axis=-1, keepdims=True)
    i1 = jnp.min(jnp.where(cand == p1, lane, big), axis=-1, keepdims=True)
    cand = jnp.where(lane == i1, -1.0, cand)
    p2 = jnp.max(cand, axis=-1, keepdims=True)
    i2 = jnp.min(jnp.where(cand == p2, lane, big), axis=-1, keepdims=True)
    tot = p1 + p2
    return (jnp.where(lane == i1, p1 / tot * g_gate, 0.0)
            + jnp.where(lane == i2, p2 / tot * g_gate, 0.0))


def _moe_kernel(y_ref, g_ref, wr_ref, br_ref, w13_ref, w2_ref, gf_ref, o_ref, h_ref, comb_ref,
                acc_ref, *, final_norm):
    grp = pl.program_id(1)

    @pl.when(grp == 0)
    def _():
        x = _rms(y_ref[...], g_ref[...])
        h_ref[...] = x.astype(BF16)
        comb_ref[...] = _route(x, wr_ref, br_ref)
        acc_ref[...] = jnp.zeros(acc_ref.shape, F32)

    hc = _dot(h_ref[...], w13_ref[0])
    half = hc.shape[1] // 2
    ff = half // EXP_PER_GROUP
    comb = comb_ref[...]
    lane = lax.broadcasted_iota(I32, comb.shape, 1)
    acts = []
    for j in range(EXP_PER_GROUP):
        e_lane = N_GROUPS + grp * EXP_PER_GROUP + j
        ce = jnp.sum(jnp.where(lane == e_lane, comb, 0.0), axis=-1, keepdims=True)
        a = hc[:, j * ff:(j + 1) * ff]
        bb = hc[:, half + j * ff:half + (j + 1) * ff]
        acts.append((a * (1.0 / (1.0 + jnp.exp(-a))) * bb * ce).astype(BF16))
    acc_ref[...] += _dot(jnp.concatenate(acts, axis=1), w2_ref[0])

    @pl.when(grp == pl.num_programs(1) - 1)
    def _():
        out = y_ref[...] + acc_ref[...]
        if final_norm:
            out = _rms(out, gf_ref[...])
        o_ref[...] = out


def _moe_layer(y, g_ffn, w_group, b_group, w_expert, b_expert, w13, w2, g_final=None):
    r, d = y.shape
    tm = _row_tile(r, 1024)
    n_route = N_GROUPS + N_EXPERTS
    wr = jnp.concatenate([w_group, w_expert, jnp.zeros((d, ROUTE_LANES - n_route), F32)], axis=1)
    wr = jnp.stack(_split3(wr)[:2])
    br = jnp.concatenate([b_group, b_expert, jnp.zeros((ROUTE_LANES - n_route,), F32)]).reshape(1, -1)
    gf = (g_ffn if g_final is None else g_final).reshape(1, d)
    ff = w13.shape[2] // 2
    per = EXP_PER_GROUP
    w13g = w13.reshape(N_GROUPS, per, d, 2, ff).transpose(0, 2, 3, 1, 4).reshape(
        N_GROUPS, d, 2 * per * ff).astype(BF16)
    w2g = w2.reshape(N_GROUPS, per * ff, d).astype(BF16)
    return pl.pallas_call(
        functools.partial(_moe_kernel, final_norm=g_final is not None), grid=(r // tm, N_GROUPS),
        in_specs=[pl.BlockSpec((tm, d), lambda i, e: (i, 0)),
                  pl.BlockSpec((1, d), lambda i, e: (0, 0)),
                  pl.BlockSpec(wr.shape, lambda i, e: (0, 0, 0)),
                  pl.BlockSpec(br.shape, lambda i, e: (0, 0)),
                  pl.BlockSpec((1, d, 2 * per * ff), lambda i, e: (e, 0, 0)),
                  pl.BlockSpec((1, per * ff, d), lambda i, e: (e, 0, 0)),
                  pl.BlockSpec((1, d), lambda i, e: (0, 0))],
        out_specs=pl.BlockSpec((tm, d), lambda i, e: (i, 0)),
        out_shape=_sds((r, d)),
        scratch_shapes=[pltpu.VMEM((tm, d), BF16), pltpu.VMEM((tm, ROUTE_LANES), F32),
                        pltpu.VMEM((tm, d), F32)],
        compiler_params=_params(("arbitrary", "arbitrary"), 56), name="hier_moe")(
            y, g_ffn.reshape(1, d), wr, br, w13g, w2g, gf)


def kernel(x_prompt, x_sample, mem_prompt, cache_nsa_cmp_kv, cache_nsa_slc_kv, state_nsa_win_kv, state_conv, cache_mla_latent, cache_mem_kv, page_table, norm_mix, norm_mem, norm_ffn, norm_final, e_w_in, e_w_out, e_cmp_w, e_cmp_pe, e_conv_w, o_w_in, o_q_norm, o_w_uq, o_kv_norm, o_w_uk, o_w_uv, o_w_out, mem_w_q, mem_w_kv, mem_w_o, moe_w_group, moe_b_group, moe_w_expert, moe_b_expert, moe_w13, moe_w2):
    b, t, d = x_prompt.shape
    sb, s, _ = x_sample.shape
    depth = norm_mix.shape[0]
    yp = x_prompt.reshape(b * t, d)
    ys = x_sample.reshape(sb * s, d)
    p_even, s_even, p_mla, s_mla, p_mem = [], [], [], [], []
    for l in range(depth):
        if l % 2 == 0:
            e = l // 2
            w = (norm_mix[l], e_w_in[e], e_w_out[e], e_cmp_w[e], e_cmp_pe[e], e_conv_w[e])
            yp, cp = _even_layer_prompt(yp, b, t, *w)
            ys, cs = _even_layer_sample(ys, sb, s, e, cache_nsa_cmp_kv, cache_nsa_slc_kv,
                                        state_nsa_win_kv[e], state_conv[e], page_table, *w)
            p_even.append(cp)
            s_even.append(cs)
        else:
            o = l // 2
            w = (norm_mix[l], o_w_in[o], o_q_norm[o], o_w_uq[o], o_kv_norm[o], o_w_uk[o], o_w_uv[o],
                 o_w_out[o])
            yp, lp = _odd_layer_prompt(yp, b, t, *w)
            ys, ls = _odd_layer_sample(ys, sb, s, o, cache_mla_latent, page_table, *w)
            p_mla.append(lp)
            s_mla.append(ls)
        kv_p = _mem_project(mem_prompt, mem_w_kv[l])
        p_mem.append(kv_p.reshape(b, kv_p.shape[1], 2, MEM_HEADS, d // MEM_HEADS))
        yp = _mem_layer(yp, b, kv_p, norm_mem[l], mem_w_q[l], mem_w_o[l])
        ys = _mem_layer(ys, sb, cache_mem_kv, norm_mem[l], mem_w_q[l], mem_w_o[l], layer=l)
        g_final = norm_final if l == depth - 1 else None
        moe_w = (norm_ffn[l], moe_w_group[l], moe_b_group[l], moe_w_expert[l], moe_b_expert[l],
                 moe_w13[l], moe_w2[l])
        yp = _moe_layer(yp, *moe_w, g_final=g_final)
        ys = _moe_layer(ys, *moe_w, g_final=g_final)
    stack = lambda caches, k: jnp.stack([c[k] for c in caches])
    return (yp.reshape(b, t, d), ys.reshape(sb, s, d),
            stack(p_even, 0), stack(p_even, 1), stack(p_even, 2), stack(p_even, 3),
            jnp.stack(p_mla), jnp.stack(p_mem),
            stack(s_even, 0), stack(s_even, 1), stack(s_even, 2), stack(s_even, 3),
            jnp.stack(s_mla))
```

```python
import functools
import math

import jax
import jax.numpy as jnp
from jax import lax
from jax.experimental import pallas as pl
from jax.experimental.pallas import tpu as pltpu

F32 = jnp.float32
BF16 = jnp.bfloat16
I32 = jnp.int32

EPS = 1e-6
LOG2E = math.log2(math.e)
NEG = -1e30
TINY = 1e-30
FORCE_BONUS = 1e4

NSA_HEADS = 8
NSA_GROUPS = 2
NSA_REP = NSA_HEADS // NSA_GROUPS
NSA_DH = 64
NSA_KV_ROW = 2 * NSA_GROUPS * NSA_DH
CMP_STRIDE = 16
CMP_BLOCK = 32
SEL_BLOCK = 64
SEL_TOPK = 16
WINDOW = 512
CONV_CH = 512
CONV_W = 3
MLA_HEADS = 16
Q_LORA = 384
KV_LORA = 256
NOPE = 64
ROPE = 32
V_DIM = 64
ROPE_THETA = 10000.0
LAT = KV_LORA + ROPE
MLA_QK = 128
MEM_HEADS = 4
N_GROUPS = 4
EXP_PER_GROUP = 4
N_EXPERTS = 16
EXPERT_FF = 256
ROUTE_LANES = 128

V7X_VMEM_BYTES = 64 * 2**20
V7X_LANES = 128


def _params(sem, vmem_mb):
    assert vmem_mb * 2**20 < V7X_VMEM_BYTES
    return pltpu.CompilerParams(dimension_semantics=sem, vmem_limit_bytes=vmem_mb * 2**20)


def _dot(a, b):
    return jnp.dot(a, b, preferred_element_type=F32)


def _dot_t(a, b):
    return lax.dot_general(a, b, (((1,), (1,)), ((), ())), preferred_element_type=F32)


def _tdot(a, b):
    return lax.dot_general(a, b, (((0,), (0,)), ((), ())), preferred_element_type=F32)


def _split3(x):
    hi = x.astype(BF16)
    r1 = x - hi.astype(F32)
    mid = r1.astype(BF16)
    lo = (r1 - mid.astype(F32)).astype(BF16)
    return hi, mid, lo


def _rms(x, g):
    return x * lax.rsqrt(jnp.mean(x * x, axis=-1, keepdims=True) + EPS) * g


def _row_spec(shape, tm, period=None):
    lead = len(shape) - 2
    blk = tuple(shape[:lead]) + (tm, shape[-1])
    if period is None:
        return pl.BlockSpec(blk, lambda i, lead=lead: (0,) * lead + (i, 0))
    return pl.BlockSpec(blk, lambda i, lead=lead, p=period: (0,) * lead + (i % p, 0))


def _const_spec(shape):
    return pl.BlockSpec(tuple(shape), lambda i, n=len(shape): (0,) * n)


def _seq_col_spec(shape, tm):
    lead = len(shape) - 3
    nt = shape[-1] // tm
    blk = tuple(shape[:lead]) + (1, shape[-2], tm)
    return pl.BlockSpec(blk, lambda i, lead=lead, nt=nt: (0,) * lead + (i // nt, 0, i % nt))


def _rowwise(body, rows, consts, outs, tm, name, vmem_mb=48, periods=None, seq_cols=()):
    n_rows = rows[0].shape[-2]
    assert n_rows % tm == 0
    periods = periods or {}
    in_specs = [_row_spec(a.shape, tm, periods.get(k)) for k, a in enumerate(rows)]
    in_specs += [_const_spec(c.shape) for c in consts]
    out_specs = [_seq_col_spec(o.shape, tm) if k in seq_cols else _row_spec(o.shape, tm)
                 for k, o in enumerate(outs)]
    nr, nc = len(rows), len(consts)

    def kern(*refs):
        body(refs[:nr], refs[nr:nr + nc], refs[nr + nc:])

    return pl.pallas_call(
        kern, grid=(n_rows // tm,), in_specs=in_specs, out_specs=out_specs, out_shape=outs,
        compiler_params=_params(("arbitrary",), vmem_mb), name=name)(*rows, *consts)


def _sds(shape, dtype=F32):
    return jax.ShapeDtypeStruct(tuple(shape), dtype)


def _row_tile(n_rows, want):
    tm = min(want, n_rows)
    assert n_rows % tm == 0
    return tm


def _even_proj_body(channel_major, rows, consts, outs):
    (x_ref,) = rows
    g_ref, wq_ref, wkv_ref, wg_ref, wc_ref = consts[:5]
    if channel_major:
        q_ref, kvc_ref, kvt_ref, kvb_ref, gate_ref, gb_ref, u_ref = outs
    else:
        q_ref, kvc_ref, kvs_ref, kvw_ref, kvb_ref, gate_ref, gb_ref, u_ref = outs
    h = _rms(x_ref[...], g_ref[...]).astype(BF16)
    q = _dot(h, wq_ref[...]) * (NSA_DH ** -0.5 * LOG2E)
    for hd in range(NSA_HEADS):
        q_ref[hd // NSA_REP, hd % NSA_REP] = q[:, hd * NSA_DH:(hd + 1) * NSA_DH].astype(BF16)
    kv = _dot(h, wkv_ref[...])
    kvc_ref[...] = kv[:, :NSA_KV_ROW]
    if channel_major:
        kvt = _dot_t(consts[5][...], h)
        for j in range(3):
            kvt_ref[j, 0] = kvt[j * NSA_KV_ROW:(j + 1) * NSA_KV_ROW]
    else:
        kvs_ref[...] = kv[:, NSA_KV_ROW:2 * NSA_KV_ROW]
        kvw_ref[...] = kv[:, 2 * NSA_KV_ROW:]
    for j in range(3 * 2 * NSA_GROUPS):
        kvb_ref[j] = kv[:, j * NSA_DH:(j + 1) * NSA_DH].astype(BF16)
    gate_ref[...] = 1.0 / (1.0 + jnp.exp(-_dot(h, wg_ref[...])))
    c = _dot(h, wc_ref[...])
    gb_ref[...] = c[:, :CONV_CH]
    u_ref[...] = c[:, CONV_CH:2 * CONV_CH] * c[:, 2 * CONV_CH:]


def _even_proj(x, g, w_in, seq_len=None):
    r, d = x.shape
    qc = NSA_HEADS * NSA_DH
    kc = 3 * NSA_KV_ROW
    gc = NSA_HEADS * 3
    wq = w_in[:, :qc].astype(BF16)
    wkv = w_in[:, qc:qc + kc].astype(BF16)
    wg = w_in[:, qc + kc:qc + kc + gc].astype(BF16)
    wc = w_in[:, qc + kc + gc:].astype(BF16)
    consts = [g.reshape(1, d), wq, wkv, wg, wc]
    tail = [_sds((12, r, NSA_DH), BF16), _sds((r, gc)), _sds((r, CONV_CH)), _sds((r, CONV_CH))]
    head = [_sds((NSA_GROUPS, NSA_REP, r, NSA_DH), BF16), _sds((r, NSA_KV_ROW))]
    if seq_len is None:
        tm = _row_tile(r, 512)
        outs = head + [_sds((r, NSA_KV_ROW)), _sds((r, NSA_KV_ROW))] + tail
        seq_cols = ()
    else:
        tm = _row_tile(seq_len, 512)
        outs = head + [_sds((3, r // seq_len, NSA_KV_ROW, seq_len))] + tail
        seq_cols = (2,)
        consts.append(wkv.T)
    return _rowwise(functools.partial(_even_proj_body, seq_len is not None), [x], consts, outs, tm,
                    "even_proj", seq_cols=seq_cols)


def _compress_weights(cmp_w, cmp_pe):
    eye_g = jnp.eye(NSA_GROUPS, dtype=F32)
    eye_k = jnp.eye(2, dtype=F32)

    def big(wh):
        return jnp.einsum("kjde,kl,gh->jkgdlhe", wh, eye_k, eye_g).reshape(
            CMP_STRIDE * NSA_KV_ROW, NSA_KV_ROW)

    w = jnp.concatenate([big(cmp_w[:, :CMP_STRIDE]), big(cmp_w[:, CMP_STRIDE:])], axis=1)

    def pe_row(peh):
        return jnp.broadcast_to(peh.transpose(1, 0, 2)[:, :, None, :],
                                (CMP_STRIDE, 2, NSA_GROUPS, NSA_DH)).reshape(1, -1)

    pe = jnp.concatenate([pe_row(cmp_pe[:, :CMP_STRIDE]), pe_row(cmp_pe[:, CMP_STRIDE:]),
                          jnp.zeros((6, CMP_STRIDE * NSA_KV_ROW), F32)], axis=0)
    return w.astype(BF16), pe


def _compress_pair_weights(cmp_w, cmp_pe):
    pairs = CMP_STRIDE // 2
    w = cmp_w.reshape(2, 2, pairs, 2, NSA_DH, NSA_DH)
    w = jnp.einsum("khpjde,gG->kpjgdhGe", w, jnp.eye(NSA_GROUPS, dtype=F32))
    w = w.reshape(2, pairs, 2 * NSA_GROUPS * NSA_DH, 2 * NSA_GROUPS * NSA_DH)
    pe = cmp_pe.reshape(2, 2, pairs, 2, 1, NSA_DH)
    pe = jnp.broadcast_to(pe, (2, 2, pairs, 2, NSA_GROUPS, NSA_DH)).transpose(0, 2, 1, 3, 4, 5)
    pe = pe.reshape(2, pairs, 2, 2 * NSA_GROUPS * NSA_DH)
    pe = jnp.concatenate([pe, jnp.zeros((2, pairs, 6, pe.shape[-1]), F32)], axis=2)
    return w.astype(BF16), pe


def _compress_core(x, w_ref, pe_ref):
    return _compress_finish(_dot(x, w_ref[...]), w_ref, pe_ref)


def _compress_finish(p, w_ref, pe_ref):
    n = p.shape[0]
    w = w_ref[...]
    hi, mid, lo = _split3(pe_ref[...])
    pb = _dot(hi, w) + _dot(mid, w) + _dot(lo, w)
    bias = pb[0:1, :NSA_KV_ROW] + pb[1:2, NSA_KV_ROW:]
    return p[:, :NSA_KV_ROW] + pltpu.roll(p[:, NSA_KV_ROW:], n - 1, 0) + bias


def _compress_prompt_kernel(x_ref, w_ref, pe_ref, o_ref):
    c = _compress_core(x_ref[0].astype(BF16), w_ref, pe_ref)
    for j in range(2 * NSA_GROUPS):
        o_ref[0, j] = c[:, j * NSA_DH:(j + 1) * NSA_DH].astype(BF16)


def _compress_prompt(kvc, b, t, w, pe):
    n = t // CMP_STRIDE
    x = kvc.reshape(b, n, CMP_STRIDE * NSA_KV_ROW)
    return pl.pallas_call(
        _compress_prompt_kernel, grid=(b,),
        in_specs=[pl.BlockSpec((1, n, CMP_STRIDE * NSA_KV_ROW), lambda i: (i, 0, 0)),
                  pl.BlockSpec(w.shape, lambda i: (0, 0)), pl.BlockSpec(pe.shape, lambda i: (0, 0))],
        out_specs=pl.BlockSpec((1, 4, n, NSA_DH), lambda i: (i, 0, 0, 0)),
        out_shape=_sds((b, 4, n, NSA_DH), BF16),
        compiler_params=_params(("arbitrary",), 48), name="nsa_compress_prompt")(x, w, pe)


def _overlap_matrix(n_cmp, n_sel):
    i = jnp.arange(n_cmp)[:, None]
    j = jnp.arange(n_sel)[None, :]
    per = SEL_BLOCK // CMP_STRIDE
    return ((i // per == j).astype(F32) + ((i + 1) // per == j).astype(F32)).astype(BF16)


def _topk_mask(score, k):
    n = score.shape[-1]
    lane = lax.broadcasted_iota(I32, score.shape, score.ndim - 1).astype(F32)
    sel = jnp.zeros_like(score)
    for _ in range(k):
        m = jnp.max(score, axis=-1, keepdims=True)
        first = jnp.min(jnp.where(score == m, lane, float(n)), axis=-1, keepdims=True)
        pick = lane == first
        sel = jnp.where(pick, 1.0, sel)
        score = jnp.where(pick, -jnp.inf, score)
    return sel


def _masked_softmax(s, mask):
    s = jnp.where(mask, s, NEG)
    m = jnp.max(s, axis=-1, keepdims=True)
    p = jnp.where(mask, jnp.exp2(s - m), 0.0)
    return p / jnp.maximum(jnp.sum(p, axis=-1, keepdims=True), TINY)


def _cmp_branch(q, ck, cv, qpos, ovl):
    tq = qpos.shape[0]
    n = ck.shape[0]
    n_sel = ovl.shape[1]
    cmp_end = CMP_STRIDE * lax.broadcasted_iota(I32, (tq, n), 1) + (CMP_BLOCK - 1)
    cmask = (cmp_end <= qpos)[None]
    s = _dot_t(q, ck).reshape(NSA_REP, tq, n)
    p = _masked_softmax(s, cmask)
    o = _dot(p.reshape(NSA_REP * tq, n).astype(BF16), cv)
    imp = jnp.sum(p, axis=0)
    hi, mid, lo = _split3(imp)
    sel_imp = _dot(hi, ovl) + _dot(mid, ovl) + _dot(lo, ovl)
    blk = lax.broadcasted_iota(I32, (tq, n_sel), 1)
    qblk = qpos // SEL_BLOCK
    forced = (blk == 0) | (blk == qblk) | (blk == qblk - 1)
    return o, jnp.where(forced, sel_imp + FORCE_BONUS, jnp.where(blk <= qblk, sel_imp, NEG))


def _select_blocks(scores, topk, sel_ref, lead):
    tq, n_sel = scores[0].shape
    sel = _topk_mask(jnp.concatenate(scores, axis=0), topk).astype(BF16)
    for g in range(NSA_GROUPS):
        sel_ref[lead + (slice(None), slice(g * n_sel, (g + 1) * n_sel))] = sel[g * tq:(g + 1) * tq]


def _nsa_cmp_prompt_kernel(q_ref, c_ref, ovl_ref, o_ref, sel_ref, *, tq):
    qpos = pl.program_id(1) * tq + lax.broadcasted_iota(I32, (tq, 1), 0)
    scores = []
    for g in range(NSA_GROUPS):
        q = q_ref[g, :, 0].reshape(NSA_REP * tq, NSA_DH)
        o, score = _cmp_branch(q, c_ref[0, g], c_ref[0, NSA_GROUPS + g], qpos, ovl_ref[...])
        scores.append(score)
        for r in range(NSA_REP):
            hd = g * NSA_REP + r
            o_ref[:, hd * NSA_DH:(hd + 1) * NSA_DH] = o[r * tq:(r + 1) * tq]
    _select_blocks(scores, min(SEL_TOPK, ovl_ref.shape[1]), sel_ref, ())


def _nsa_cmp_prompt(q, cmp, b, t, tq=256):
    n = t // CMP_STRIDE
    n_sel = t // SEL_BLOCK
    tq = min(tq, t)
    nq = t // tq
    ovl = _overlap_matrix(n, n_sel)
    q5 = q.reshape(NSA_GROUPS, NSA_REP, b, t, NSA_DH)
    return pl.pallas_call(
        functools.partial(_nsa_cmp_prompt_kernel, tq=tq), grid=(b, nq),
        in_specs=[pl.BlockSpec((NSA_GROUPS, NSA_REP, 1, tq, NSA_DH), lambda i, j: (0, 0, i, j, 0)),
                  pl.BlockSpec((1, 4, n, NSA_DH), lambda i, j: (i, 0, 0, 0)),
                  pl.BlockSpec(ovl.shape, lambda i, j: (0, 0))],
        out_specs=[pl.BlockSpec((tq, NSA_HEADS * NSA_DH), lambda i, j, nq=nq: (i * nq + j, 0)),
                   pl.BlockSpec((tq, NSA_GROUPS * n_sel), lambda i, j, nq=nq: (i * nq + j, 0))],
        out_shape=[_sds((b * t, NSA_HEADS * NSA_DH)), _sds((b * t, NSA_GROUPS * n_sel), BF16)],
        compiler_params=_params(("arbitrary", "arbitrary"), 48), name="nsa_cmp_prompt")(q5, cmp, ovl)


def _flash_update_t(q, k, v, mask_t, m_ref, acc_ref):
    s = _dot_t(k, q)
    if mask_t is not None:
        s = jnp.where(mask_t, s, NEG)
    m_prev = m_ref[...]
    m_new = jnp.maximum(m_prev, jnp.max(s, axis=0, keepdims=True))
    p = jnp.exp2(s - jnp.maximum(m_new, 0.1 * NEG))
    alpha = jnp.exp2(m_prev - m_new)
    v1 = jnp.concatenate([v, jnp.ones_like(v)], axis=1)
    acc_ref[...] = alpha * acc_ref[...] + _tdot(v1, p.astype(BF16))
    m_ref[...] = m_new


def _flash_result_t(acc):
    dh = acc.shape[0] // 2
    return acc[:dh] / jnp.maximum(acc[dh:dh + 1], TINY)


def _causal_steps(nq, tq, tk):
    qi, ki = [], []
    for j in range(nq):
        for k in range((j * tq + tq - 1) // tk + 1):
            qi.append(j)
            ki.append(k)
    return jnp.asarray(qi, I32), jnp.asarray(ki, I32)


def _nsa_swt_prompt_kernel(qi_ref, ki_ref, q_ref, ks_ref, vs_ref, kw_ref, vw_ref, sel_ref, ocmp_ref,
                           gate_ref, o_ref, m_ref, acc_ref, *, tq, tk):
    n = pl.program_id(1)
    qi = qi_ref[n]
    ki = ki_ref[n]
    q0 = qi * tq
    last = (q0 + tq - 1) // tk
    first_w = jnp.maximum(q0 - (WINDOW - 1), 0) // tk
    n_sel = sel_ref.shape[1] // NSA_GROUPS
    n_q = NSA_REP * tq

    @pl.when(ki == 0)
    def _():
        m_ref[...] = jnp.full(m_ref.shape, NEG, F32)
        acc_ref[...] = jnp.zeros(acc_ref.shape, F32)

    kpos = ki * tk + lax.broadcasted_iota(I32, (tk, tq), 0)
    qpos = q0 + lax.broadcasted_iota(I32, (tk, tq), 1)
    causal = kpos <= qpos
    heads = lambda mask: jnp.concatenate([mask] * NSA_REP, axis=1)

    expand = (lax.broadcasted_iota(I32, (tk, n_sel), 1)
              == (ki * tk + lax.broadcasted_iota(I32, (tk, n_sel), 0)) // SEL_BLOCK)
    expand = jnp.where(expand, 1.0, 0.0).astype(BF16)
    for g in range(NSA_GROUPS):
        q = q_ref[g, :, 0].reshape(n_q, NSA_DH)
        chosen = _dot_t(expand, sel_ref[:, g * n_sel:(g + 1) * n_sel])
        mask = jnp.where(causal, chosen, 0.0) > 0.5
        _flash_update_t(q, ks_ref[g, 0], vs_ref[g, 0], heads(mask), m_ref.at[g], acc_ref.at[g])

    @pl.when(ki >= first_w)
    def _():
        mask = heads(jnp.where(causal, qpos - kpos, WINDOW) < WINDOW)
        for g in range(NSA_GROUPS):
            q = q_ref[g, :, 0].reshape(n_q, NSA_DH)
            _flash_update_t(q, kw_ref[g, 0], vw_ref[g, 0], mask, m_ref.at[NSA_GROUPS + g],
                            acc_ref.at[NSA_GROUPS + g])

    @pl.when(ki == last)
    def _():
        gates = gate_ref[...]
        ocmp = ocmp_ref[...]
        for g in range(NSA_GROUPS):
            o_s = _flash_result_t(acc_ref[g])
            o_w = _flash_result_t(acc_ref[NSA_GROUPS + g])
            for r in range(NSA_REP):
                hd = g * NSA_REP + r
                lanes = slice(r * tq, (r + 1) * tq)
                cols = slice(hd * NSA_DH, (hd + 1) * NSA_DH)
                o_ref[:, cols] = (gates[:, 3 * hd:3 * hd + 1] * ocmp[:, cols]
                                  + gates[:, 3 * hd + 1:3 * hd + 2] * o_s[:, lanes].T
                                  + gates[:, 3 * hd + 2:3 * hd + 3] * o_w[:, lanes].T)


def _nsa_swt_prompt(q, kvb, sel, o_cmp, gates, b, t, tq=256, tk=1024):
    tq = min(tq, t)
    tk = min(tk, t)
    nq = t // tq
    qi, ki = _causal_steps(nq, tq, tk)
    q5 = q.reshape(NSA_GROUPS, NSA_REP, b, t, NSA_DH)
    kv4 = kvb.reshape(12, b, t, NSA_DH)

    def slc_idx(blk):
        return lambda i, n, qt, kt: (blk, i, kt[n], 0)

    def win_idx(blk):
        def f(i, n, qt, kt):
            first = jnp.maximum(qt[n] * tq - (WINDOW - 1), 0) // tk
            return (blk, i, jnp.maximum(kt[n], first), 0)
        return f

    kv_blk = (NSA_GROUPS, 1, tk, NSA_DH)
    row = lambda i, n, qt, kt: (i * nq + qt[n], 0)
    n_q = NSA_REP * tq
    grid_spec = pltpu.PrefetchScalarGridSpec(
        num_scalar_prefetch=2, grid=(b, qi.shape[0]),
        in_specs=[pl.BlockSpec((NSA_GROUPS, NSA_REP, 1, tq, NSA_DH),
                               lambda i, n, qt, kt: (0, 0, i, qt[n], 0)),
                  pl.BlockSpec(kv_blk, slc_idx(2)), pl.BlockSpec(kv_blk, slc_idx(3)),
                  pl.BlockSpec(kv_blk, win_idx(4)), pl.BlockSpec(kv_blk, win_idx(5)),
                  pl.BlockSpec((tq, sel.shape[1]), row),
                  pl.BlockSpec((tq, o_cmp.shape[1]), row),
                  pl.BlockSpec((tq, gates.shape[1]), row)],
        out_specs=pl.BlockSpec((tq, NSA_HEADS * NSA_DH), row),
        scratch_shapes=[pltpu.VMEM((2 * NSA_GROUPS, 1, n_q), F32),
                        pltpu.VMEM((2 * NSA_GROUPS, 2 * NSA_DH, n_q), F32)])
    return pl.pallas_call(
        functools.partial(_nsa_swt_prompt_kernel, tq=tq, tk=tk), grid_spec=grid_spec,
        out_shape=_sds((b * t, NSA_HEADS * NSA_DH)),
        compiler_params=_params(("arbitrary", "arbitrary"), 48),
        name="nsa_slc_win_prompt")(qi, ki, q5, kv4, kv4, kv4, kv4, sel, o_cmp, gates)


def _conv_prompt_kernel(u_ref, gb_ref, w_ref, o_ref, carry_ref):
    tm = u_ref.shape[0]

    @pl.when(pl.program_id(1) == 0)
    def _():
        carry_ref[...] = jnp.zeros(carry_ref.shape, F32)

    u = u_ref[...]
    prev = carry_ref[...]
    row = lax.broadcasted_iota(I32, u.shape, 0)
    u1 = jnp.where(row == 0, prev[7:8], pltpu.roll(u, 1, 0))
    u2 = jnp.where(row == 0, prev[6:7], jnp.where(row == 1, prev[7:8], pltpu.roll(u, 2, 0)))
    w = w_ref[...]
    o_ref[...] = gb_ref[...] * (w[0:1] * u2 + w[1:2] * u1 + w[2:3] * u)
    carry_ref[...] = u[tm - 8:]


def _conv_prompt(u, gb, conv_w, b, t, tm=1024):
    tm = min(tm, t)
    nt = t // tm
    c = u.shape[1]
    row = lambda i, j: (i * nt + j, 0)
    return pl.pallas_call(
        _conv_prompt_kernel, grid=(b, nt),
        in_specs=[pl.BlockSpec((tm, c), row), pl.BlockSpec((tm, c), row),
                  pl.BlockSpec(conv_w.shape, lambda i, j: (0, 0))],
        out_specs=pl.BlockSpec((tm, c), row), out_shape=_sds(u.shape),
        scratch_shapes=[pltpu.VMEM((8, c), F32)],
        compiler_params=_params(("arbitrary", "arbitrary"), 32), name="conv_prompt")(u, gb, conv_w)


def _matmul_res_body(n_a, rows, consts, outs):
    acc = rows[n_a][...]
    for k in range(n_a):
        acc = acc + _dot(rows[k][...].astype(BF16), consts[k][...])
    outs[0][...] = acc


def _matmul_res(acts, weights, res, name):
    r = res.shape[0]
    tm = _row_tile(r, 512)
    ws = [w.astype(BF16) for w in weights]
    return _rowwise(functools.partial(_matmul_res_body, len(acts)), list(acts) + [res], ws,
                    [_sds(res.shape)], tm, name)[0]


def _even_layer_prompt(x, b, t, g_mix, w_in, w_out, cmp_w, cmp_pe, conv_w):
    q, kvc, kvt, kvb, gates, gb, u = _even_proj(x, g_mix, w_in, seq_len=t)
    cw, cpe = _compress_weights(cmp_w, cmp_pe)
    cmp = _compress_prompt(kvc, b, t, cw, cpe)
    o_cmp, sel = _nsa_cmp_prompt(q, cmp, b, t)
    o_nsa = _nsa_swt_prompt(q, kvb, sel, o_cmp, gates, b, t)
    o_conv = _conv_prompt(u, gb, conv_w, b, t)
    qc = NSA_HEADS * NSA_DH
    y = _matmul_res([o_nsa, o_conv], [w_out[:qc], w_out[qc:]], x, "even_out_proj")
    wb = min(WINDOW, t)

    def rows(a):
        return a.reshape(b, 2, NSA_GROUPS, NSA_DH, a.shape[-1]).transpose(0, 4, 1, 2, 3)

    caches = (rows(kvt[0]), rows(kvt[1]), rows(kvt[2][:, :, t - wb:]),
              u.reshape(b, t, CONV_CH)[:, t - (CONV_W - 1):])
    return y, caches


def _channel_major(pool):
    nd = pool.ndim
    t = pool.transpose((0, 1) + tuple(range(3, nd)) + (2,))
    return t.reshape(pool.shape[0], pool.shape[1], -1, pool.shape[2])


def _page_copies(pool_ref, layer, pt_ref, batch, n_pages, buf_ref, slot, sem_ref):
    return [pltpu.make_async_copy(pool_ref.at[layer, pt_ref[batch * n_pages + p]],
                                  buf_ref.at[slot, :, pl.ds(p * PAGE_ROWS, PAGE_ROWS)],
                                  sem_ref.at[slot])
            for p in range(n_pages)]


def _paged_fetch(pool_ref, layer, pt_ref, n_pages, buf_ref, sem_ref):
    b = pl.program_id(0)
    slot = b % 2

    @pl.when(b == 0)
    def _():
        for c in _page_copies(pool_ref, layer, pt_ref, b, n_pages, buf_ref, slot, sem_ref):
            c.start()

    @pl.when(b + 1 < pl.num_programs(0))
    def _():
        for c in _page_copies(pool_ref, layer, pt_ref, b + 1, n_pages, buf_ref, 1 - slot, sem_ref):
            c.start()

    for c in _page_copies(pool_ref, layer, pt_ref, b, n_pages, buf_ref, slot, sem_ref):
        c.wait()
    return slot


def _online(state, s, mask, v, v_t=False):
    m_prev, l_prev, acc = state
    if mask is not None:
        s = jnp.where(mask, s, NEG)
    m_new = jnp.maximum(m_prev, jnp.max(s, axis=-1, keepdims=True))
    p = jnp.exp2(s - m_new)
    if mask is not None:
        p = jnp.where(mask, p, 0.0)
    alpha = jnp.exp2(m_prev - m_new)
    return (m_new, alpha * l_prev + jnp.sum(p, axis=-1, keepdims=True),
            alpha * acc + (_dot_t(p.astype(BF16), v) if v_t else _dot(p.astype(BF16), v)))


def _online_init(m, w):
    return (jnp.full((m, 1), NEG, F32), jnp.zeros((m, 1), F32), jnp.zeros((m, w), F32))


def _group_query_rows(q_ref, s):
    blocks = []
    for g in range(NSA_GROUPS):
        q = q_ref[g, :, 0].astype(F32).reshape(NSA_REP * s, NSA_DH)
        z = jnp.zeros_like(q)
        blocks.append(jnp.concatenate([q if j == g else z for j in range(2 * NSA_GROUPS)], axis=1))
    return jnp.concatenate(blocks, axis=0).astype(BF16)


def _group_rows(x):
    return jnp.concatenate([jnp.tile(xg, (NSA_REP, 1)) for xg in x], axis=0)


def _nsa_cmp_sample_kernel(pt_ref, pool_ref, q_ref, w_ref, pe_ref, ovl_ref, o_ref, sel_ref,
                           buf_ref, sem_ref, x_ref, bias_ref, *, layer, n_pages, s, topk):
    half = NSA_GROUPS * NSA_DH

    @pl.when(pl.program_id(0) == 0)
    def _():
        for kv in range(2):
            pb = None
            for jp in range(w_ref.shape[1]):
                for term in _split3(pe_ref[kv, jp]):
                    t = _dot(term, w_ref[kv, jp])
                    pb = t if pb is None else pb + t
            bias_ref[kv] = jnp.broadcast_to(pb[0:1, :half] + pb[1:2, half:], bias_ref.shape[1:])

    slot = _paged_fetch(pool_ref, layer, pt_ref, n_pages, buf_ref, sem_ref)
    past = n_pages * PAGE_ROWS
    n = past // CMP_STRIDE
    step = min(1024, past)
    for c0 in range(0, past, step):
        for kv in range(2):
            x_ref[kv, c0:c0 + step, :] = buf_ref[slot, kv * half:(kv + 1) * half, c0:c0 + step].T
    c = []
    for kv in range(2):
        p = None
        for jp in range(w_ref.shape[1]):
            x = jnp.concatenate(
                [x_ref.at[kv][pl.ds(2 * jp + jj, n, stride=CMP_STRIDE), :] for jj in range(2)],
                axis=1).astype(BF16)
            t = _dot(x, w_ref[kv, jp])
            p = t if p is None else p + t
        c.append(p[:, :half] + pltpu.roll(p[:, half:], n - 1, 0) + bias_ref[kv, 0:1])
    qpos = n_pages * PAGE_ROWS + lax.broadcasted_iota(I32, (s, 1), 0)
    scores = []
    for g in range(NSA_GROUPS):
        q = q_ref[g, :, 0].astype(F32).reshape(NSA_REP * s, NSA_DH).astype(BF16)
        ck = c[0][:, g * NSA_DH:(g + 1) * NSA_DH].astype(BF16)
        cv = c[1][:, g * NSA_DH:(g + 1) * NSA_DH].astype(BF16)
        o, score = _cmp_branch(q, ck, cv, qpos, ovl_ref[...])
        scores.append(score)
        for r in range(NSA_REP):
            hd = g * NSA_REP + r
            o_ref[0, :, hd * NSA_DH:(hd + 1) * NSA_DH] = o[r * s:(r + 1) * s]
    _select_blocks(scores, topk, sel_ref, (0,))


PAGE_ROWS = 128
PAGE_CHUNKS = PAGE_ROWS // CMP_STRIDE


def _nsa_cmp_sample(q, pool, layer, page_table, w, pe, sb, s):
    n_pages = page_table.shape[1]
    past = n_pages * PAGE_ROWS
    assert past % SEL_BLOCK == 0 and s <= CMP_STRIDE and pool.shape[2] == PAGE_ROWS
    n = past // CMP_STRIDE
    n_sel = past // SEL_BLOCK
    ovl = _overlap_matrix(n, n_sel)
    pool4 = _channel_major(pool)
    q5 = q.reshape(NSA_GROUPS, NSA_REP, sb, s, NSA_DH)
    kern = functools.partial(_nsa_cmp_sample_kernel, layer=layer, n_pages=n_pages, s=s,
                             topk=SEL_TOPK - 1)
    grid_spec = pltpu.PrefetchScalarGridSpec(
        num_scalar_prefetch=1, grid=(sb,),
        in_specs=[pl.BlockSpec(memory_space=pl.ANY),
                  pl.BlockSpec((NSA_GROUPS, NSA_REP, 1, s, NSA_DH), lambda i, pt: (0, 0, i, 0, 0)),
                  pl.BlockSpec(w.shape, lambda i, pt: (0, 0, 0, 0)),
                  pl.BlockSpec(pe.shape, lambda i, pt: (0, 0, 0, 0)),
                  pl.BlockSpec(ovl.shape, lambda i, pt: (0, 0))],
        out_specs=[pl.BlockSpec((1, s, NSA_HEADS * NSA_DH), lambda i, pt: (i, 0, 0)),
                   pl.BlockSpec((1, s, NSA_GROUPS * n_sel), lambda i, pt: (i, 0, 0))],
        scratch_shapes=[pltpu.VMEM((2, NSA_KV_ROW, past), F32), pltpu.SemaphoreType.DMA((2,)),
                        pltpu.VMEM((2, past, NSA_GROUPS * NSA_DH), F32),
                        pltpu.VMEM((2, 8, NSA_GROUPS * NSA_DH), F32)])
    return pl.pallas_call(
        kern, grid_spec=grid_spec,
        out_shape=[_sds((sb, s, NSA_HEADS * NSA_DH)), _sds((sb, s, NSA_GROUPS * n_sel), BF16)],
        compiler_params=_params(("arbitrary",), 56), name="nsa_cmp_sample")(
            page_table.reshape(-1), pool4, q5, w, pe, ovl)


def _nsa_sw_sample_kernel(pt_ref, pool_ref, q_ref, knew_ref, wnew_ref, win_ref, sel_ref, ocmp_ref,
                          gate_ref, o_ref, buf_ref, sem_ref, *, layer, n_pages, s, chunk):
    slot = _paged_fetch(pool_ref, layer, pt_ref, n_pages, buf_ref, sem_ref)
    past = n_pages * PAGE_ROWS
    n_sel = sel_ref.shape[2] // NSA_GROUPS
    m = NSA_HEADS * s
    vrow = NSA_KV_ROW // 2
    qm = _group_query_rows(q_ref, s)
    sq = lax.broadcasted_iota(I32, (s, s), 0)
    sk = lax.broadcasted_iota(I32, (s, s), 1)
    new_mask = _group_rows([sk <= sq] * NSA_GROUPS)

    state = _online_init(m, vrow)
    for c0 in range(0, past, chunk):
        kv = buf_ref[slot, :, c0:c0 + chunk].astype(BF16)
        expand = (lax.broadcasted_iota(I32, (n_sel, chunk), 0)
                  == (c0 + lax.broadcasted_iota(I32, (n_sel, chunk), 1)) // SEL_BLOCK)
        expand = jnp.where(expand, 1.0, 0.0).astype(BF16)
        chosen = [_dot(sel_ref[0, :, g * n_sel:(g + 1) * n_sel], expand) > 0.5
                  for g in range(NSA_GROUPS)]
        state = _online(state, _dot(qm, kv), _group_rows(chosen), kv[vrow:], v_t=True)
    knew = knew_ref[0].astype(BF16)
    state = _online(state, _dot_t(qm, knew), new_mask, knew[:, vrow:])
    o_slc = state[2] / jnp.maximum(state[1], TINY)

    wb = win_ref.shape[2]
    win = win_ref[0].astype(BF16)
    qrow = lax.broadcasted_iota(I32, (s, wb), 0)
    dist = qrow + wb - lax.broadcasted_iota(I32, (s, wb), 1)
    old_mask = _group_rows([(dist >= 0) & (dist < WINDOW)] * NSA_GROUPS)
    state = _online(_online_init(m, vrow), _dot(qm, win), old_mask, win[vrow:], v_t=True)
    wnew = wnew_ref[0].astype(BF16)
    state = _online(state, _dot_t(qm, wnew), new_mask, wnew[:, vrow:])
    o_win = state[2] / jnp.maximum(state[1], TINY)

    gates = gate_ref[0]
    ocmp = ocmp_ref[0]
    for g in range(NSA_GROUPS):
        vcols = slice(g * NSA_DH, (g + 1) * NSA_DH)
        for r in range(NSA_REP):
            hd = g * NSA_REP + r
            rows = slice(hd * s, (hd + 1) * s)
            cols = slice(hd * NSA_DH, (hd + 1) * NSA_DH)
            o_ref[0, :, cols] = (gates[:, 3 * hd:3 * hd + 1] * ocmp[:, cols]
                                 + gates[:, 3 * hd + 1:3 * hd + 2] * o_slc[rows, vcols]
                                 + gates[:, 3 * hd + 2:3 * hd + 3] * o_win[rows, vcols])


def _nsa_sw_sample(q, pool, layer, page_table, kvs_new, kvw_new, win_buf, sel, o_cmp, gates, sb, s):
    n_pages = page_table.shape[1]
    past = n_pages * PAGE_ROWS
    wb = win_buf.shape[1]
    assert wb == WINDOW and s <= SEL_BLOCK and past % SEL_BLOCK == 0
    pool4 = _channel_major(pool)
    q5 = q.reshape(NSA_GROUPS, NSA_REP, sb, s, NSA_DH)
    per_b = lambda a: a.reshape(sb, s, a.shape[-1])
    blk = lambda a: pl.BlockSpec((1,) + a.shape[1:], lambda i, pt: (i, 0, 0))
    win_t = _channel_major(win_buf[None])[0]
    rows = [per_b(kvs_new), per_b(kvw_new), win_t, sel, o_cmp, per_b(gates)]
    kern = functools.partial(_nsa_sw_sample_kernel, layer=layer, n_pages=n_pages, s=s,
                             chunk=min(2048, past))
    grid_spec = pltpu.PrefetchScalarGridSpec(
        num_scalar_prefetch=1, grid=(sb,),
        in_specs=[pl.BlockSpec(memory_space=pl.ANY),
                  pl.BlockSpec((NSA_GROUPS, NSA_REP, 1, s, NSA_DH), lambda i, pt: (0, 0, i, 0, 0))]
        + [blk(a) for a in rows],
        out_specs=pl.BlockSpec((1, s, NSA_HEADS * NSA_DH), lambda i, pt: (i, 0, 0)),
        scratch_shapes=[pltpu.VMEM((2, NSA_KV_ROW, past), F32), pltpu.SemaphoreType.DMA((2,))])
    return pl.pallas_call(
        kern, grid_spec=grid_spec, out_shape=_sds((sb, s, NSA_HEADS * NSA_DH)),
        compiler_params=_params(("arbitrary",), 56), name="nsa_slc_win_sample")(
            page_table.reshape(-1), pool4, q5, *rows)


def _conv_sample_kernel(u_ref, gb_ref, cb_ref, w_ref, o_ref):
    u = u_ref[...]
    cb = cb_ref[...]
    row = lax.broadcasted_iota(I32, u.shape, 1)
    u1 = jnp.where(row == 0, cb[:, 1:2], pltpu.roll(u, 1, 1))
    u2 = jnp.where(row == 0, cb[:, 0:1], jnp.where(row == 1, cb[:, 1:2], pltpu.roll(u, 2, 1)))
    w = w_ref[...]
    o_ref[...] = gb_ref[...] * (w[0:1][None] * u2 + w[1:2][None] * u1 + w[2:3][None] * u)


def _conv_sample(u, gb, conv_buf, conv_w, sb, s):
    assert s >= CONV_W - 1
    c = u.shape[1]
    full = lambda shape: pl.BlockSpec(shape, lambda i, n=len(shape): (0,) * n)
    return pl.pallas_call(
        _conv_sample_kernel, grid=(1,),
        in_specs=[full((sb, s, c)), full((sb, s, c)), full(conv_buf.shape), full(conv_w.shape)],
        out_specs=full((sb, s, c)), out_shape=_sds((sb, s, c)),
        compiler_params=_params(("arbitrary",), 32), name="conv_sample")(
            u.reshape(sb, s, c), gb.reshape(sb, s, c), conv_buf, conv_w).reshape(sb * s, c)


def _even_layer_sample(x, sb, s, layer, pool_cmp, pool_slc, win_buf, conv_buf, page_table,
                       g_mix, w_in, w_out, cmp_w, cmp_pe, conv_w):
    q, kvc, kvs, kvw, _, gates, gb, u = _even_proj(x, g_mix, w_in)
    cw, cpe = _compress_pair_weights(cmp_w, cmp_pe)
    o_cmp, sel = _nsa_cmp_sample(q, pool_cmp, layer, page_table, cw, cpe, sb, s)
    o_nsa = _nsa_sw_sample(q, pool_slc, layer, page_table, kvs, kvw, win_buf, sel, o_cmp, gates,
                           sb, s)
    o_conv = _conv_sample(u, gb, conv_buf, conv_w, sb, s)
    qc = NSA_HEADS * NSA_DH
    y = _matmul_res([o_nsa.reshape(sb * s, qc), o_conv], [w_out[:qc], w_out[qc:]], x,
                    "even_out_proj")
    five = lambda a: a.reshape(sb, s, 2, NSA_GROUPS, NSA_DH)
    win_all = jnp.concatenate([win_buf, five(kvw)], axis=1)[:, s:]
    conv_all = jnp.concatenate([conv_buf, u.reshape(sb, s, CONV_CH)], axis=1)[:, s:]
    return y, (five(kvc), five(kvs), win_all, conv_all)


def _rot_half_cols(w):
    half = w.shape[-1] // 2
    return jnp.concatenate([-w[..., half:], w[..., :half]], axis=-1)


def _rope_tables(pos):
    half = ROPE // 2
    inv = ROPE_THETA ** (-jnp.arange(half, dtype=F32) / half)
    ang = pos.astype(F32)[:, None] * inv[None, :]
    cos, sin = jnp.cos(ang), jnp.sin(ang)
    cos32 = jnp.concatenate([cos, cos], axis=1)
    sin32 = jnp.concatenate([sin, sin], axis=1)
    n = pos.shape[0]
    pad = MLA_QK - NOPE - ROPE
    cosq = jnp.concatenate([jnp.ones((n, NOPE), F32), cos32, jnp.zeros((n, pad), F32)], axis=1)
    sinq = jnp.concatenate([jnp.zeros((n, NOPE), F32), sin32, jnp.zeros((n, pad), F32)], axis=1)
    return cos32, sin32, cosq, sinq


def _mla_weights(w_in, w_uq, w_uk):
    wcq = w_in[:, :Q_LORA]
    wckv = w_in[:, Q_LORA:Q_LORA + KV_LORA]
    wkpe = w_in[:, Q_LORA + KV_LORA:]
    uq = w_uq.reshape(Q_LORA, MLA_HEADS, NOPE + ROPE)
    pad = jnp.zeros((Q_LORA, MLA_HEADS, MLA_QK - NOPE - ROPE), F32)
    wq = jnp.concatenate([uq, pad], axis=-1).reshape(Q_LORA, MLA_HEADS * MLA_QK)
    wq_rot = jnp.concatenate([jnp.zeros((Q_LORA, MLA_HEADS, NOPE), F32),
                              _rot_half_cols(uq[..., NOPE:]), pad], axis=-1).reshape(wq.shape)
    kpad = jnp.zeros((KV_LORA, MLA_HEADS, MLA_QK - NOPE), F32)
    wk_lat = jnp.concatenate([w_uk, kpad], axis=-1).reshape(KV_LORA, MLA_HEADS * MLA_QK)
    eye = jnp.broadcast_to(jnp.eye(ROPE, dtype=F32)[:, None, :], (ROPE, MLA_HEADS, ROPE))
    wk_pe = jnp.concatenate([jnp.zeros((ROPE, MLA_HEADS, NOPE), F32), eye,
                             jnp.zeros((ROPE, MLA_HEADS, MLA_QK - NOPE - ROPE), F32)],
                            axis=-1).reshape(ROPE, MLA_HEADS * MLA_QK)
    bf = lambda a: a.astype(BF16)
    return dict(wcq=bf(wcq), wckv=bf(wckv), wkpe=bf(wkpe), wkpe_rot=bf(_rot_half_cols(wkpe)),
                wq=bf(wq), wq_rot=bf(wq_rot), wk_lat=bf(wk_lat), wk_pe=bf(wk_pe))


def _mla_latent(x_ref, cos_ref, sin_ref, g_ref, qn_ref, kvn_ref, wcq_ref, wckv_ref, wkpe_ref,
                wkrot_ref, lat_ref):
    h = _rms(x_ref[...], g_ref[...]).astype(BF16)
    cq = _rms(_dot(h, wcq_ref[...]), qn_ref[...]).astype(BF16)
    ckv = _rms(_dot(h, wckv_ref[...]), kvn_ref[...])
    kpe = _dot(h, wkpe_ref[...]) * cos_ref[...] + _dot(h, wkrot_ref[...]) * sin_ref[...]
    lat_ref[:, :KV_LORA] = ckv
    lat_ref[:, KV_LORA:] = kpe
    return cq, ckv.astype(BF16), kpe.astype(BF16)


def _mla_queries(cq, wq_ref, wqrot_ref, cosq, sinq, head):
    cols = slice(head * MLA_QK, (head + 1) * MLA_QK)
    q = _dot(cq, wq_ref[:, cols]) * cosq + _dot(cq, wqrot_ref[:, cols]) * sinq
    return q * ((NOPE + ROPE) ** -0.5 * LOG2E)


def _mla_proj_prompt_body(rows, consts, outs):
    x_ref, cos_ref, sin_ref, cosq_ref, sinq_ref = rows
    (g_ref, qn_ref, kvn_ref, wcq_ref, wckv_ref, wkpe_ref, wkrot_ref, wq_ref, wqrot_ref,
     wklat_ref, wkpe2_ref, wuv_ref) = consts
    lat_ref, q_ref, k_ref, v_ref = outs
    cq, ckv, kpe = _mla_latent(x_ref, cos_ref, sin_ref, g_ref, qn_ref, kvn_ref, wcq_ref,
                               wckv_ref, wkpe_ref, wkrot_ref, lat_ref)
    cosq, sinq = cosq_ref[...], sinq_ref[...]
    for hd in range(MLA_HEADS):
        cols = slice(hd * MLA_QK, (hd + 1) * MLA_QK)
        q_ref[:, cols] = _mla_queries(cq, wq_ref, wqrot_ref, cosq, sinq, hd).astype(BF16)
    k_ref[...] = (_dot(ckv, wklat_ref[...]) + _dot(kpe, wkpe2_ref[...])).astype(BF16)
    v_ref[...] = _dot(ckv, wuv_ref[...]).astype(BF16)


def _mla_proj_sample_body(rows, consts, outs):
    x_ref, cos_ref, sin_ref, cosq_ref, sinq_ref = rows
    (g_ref, qn_ref, kvn_ref, wcq_ref, wckv_ref, wkpe_ref, wkrot_ref, wq_ref, wqrot_ref,
     wukt_ref) = consts
    lat_ref, q_ref = outs
    cq, _, _ = _mla_latent(x_ref, cos_ref, sin_ref, g_ref, qn_ref, kvn_ref, wcq_ref, wckv_ref,
                           wkpe_ref, wkrot_ref, lat_ref)
    cosq, sinq = cosq_ref[...], sinq_ref[...]
    for hd in range(MLA_HEADS):
        q = _mla_queries(cq, wq_ref, wqrot_ref, cosq, sinq, hd)
        q_ref[hd, :, :KV_LORA] = _dot(q[:, :NOPE].astype(BF16), wukt_ref[hd])
        q_ref[hd, :, KV_LORA:] = q[:, NOPE:NOPE + ROPE]


def _mla_proj(x, pos, g_mix, q_norm, kv_norm, mw, extra, body, outs, name):
    r, d = x.shape
    tm = _row_tile(pos.shape[0], 512)
    cos32, sin32, cosq, sinq = _rope_tables(pos)
    n_per = pos.shape[0] // tm
    periods = {k: n_per for k in (1, 2, 3, 4)}
    consts = [g_mix.reshape(1, d), q_norm.reshape(1, -1), kv_norm.reshape(1, -1), mw["wcq"],
              mw["wckv"], mw["wkpe"], mw["wkpe_rot"], mw["wq"], mw["wq_rot"]] + extra
    return _rowwise(body, [x, cos32, sin32, cosq, sinq], consts, outs, tm, name, periods=periods)


def _mla_flash_t_kernel(qi_ref, ki_ref, q_ref, k_ref, v_ref, o_ref, m_ref, acc_ref, *, tq, tk):
    n = pl.program_id(2)
    qi = qi_ref[n]
    ki = ki_ref[n]
    last = (qi * tq + tq - 1) // tk

    @pl.when(ki == 0)
    def _():
        m_ref[...] = jnp.full(m_ref.shape, NEG, F32)
        acc_ref[...] = jnp.zeros(acc_ref.shape, F32)

    def step(masked):
        mask = None
        if masked:
            mask = (ki * tk + lax.broadcasted_iota(I32, (tk, tq), 0)
                    <= qi * tq + lax.broadcasted_iota(I32, (tk, tq), 1))
        for j in range(2):
            qk = slice(j * MLA_QK, (j + 1) * MLA_QK)
            _flash_update_t(q_ref[:, qk], k_ref[:, qk], v_ref[:, j * V_DIM:(j + 1) * V_DIM], mask,
                            m_ref.at[j], acc_ref.at[j])

    needs_mask = ki * tk + tk - 1 > qi * tq

    @pl.when(needs_mask)
    def _():
        step(True)

    @pl.when(jnp.logical_not(needs_mask))
    def _():
        step(False)

    @pl.when(ki == last)
    def _():
        for j in range(2):
            o_ref[:, j * V_DIM:(j + 1) * V_DIM] = _flash_result_t(acc_ref[j]).T


def _mla_flash_t(q, k, v, b, t, tq=1024, tk=1024):
    tq, tk = min(tq, t), min(tk, t)
    nq, nk = t // tq, t // tk
    qi, ki = _causal_steps(nq, tq, tk)
    pairs = MLA_HEADS // 2
    grid_spec = pltpu.PrefetchScalarGridSpec(
        num_scalar_prefetch=2, grid=(b, pairs, qi.shape[0]),
        in_specs=[pl.BlockSpec((tq, 2 * MLA_QK), lambda i, h, n, qt, kt: (i * nq + qt[n], h)),
                  pl.BlockSpec((tk, 2 * MLA_QK), lambda i, h, n, qt, kt: (i * nk + kt[n], h)),
                  pl.BlockSpec((tk, 2 * V_DIM), lambda i, h, n, qt, kt: (i * nk + kt[n], h))],
        out_specs=pl.BlockSpec((tq, 2 * V_DIM), lambda i, h, n, qt, kt: (i * nq + qt[n], h)),
        scratch_shapes=[pltpu.VMEM((2, 1, tq), F32), pltpu.VMEM((2, 2 * V_DIM, tq), F32)])
    return pl.pallas_call(
        functools.partial(_mla_flash_t_kernel, tq=tq, tk=tk), grid_spec=grid_spec,
        out_shape=_sds((b * t, MLA_HEADS * V_DIM)),
        compiler_params=_params(("arbitrary", "arbitrary", "arbitrary"), 48),
        name="mla_flash_prompt")(qi, ki, q, k, v)


def _odd_layer_prompt(x, b, t, g_mix, w_in, q_norm, w_uq, kv_norm, w_uk, w_uv, w_out):
    mw = _mla_weights(w_in, w_uq, w_uk)
    r = x.shape[0]
    outs = [_sds((r, LAT)), _sds((r, MLA_HEADS * MLA_QK), BF16), _sds((r, MLA_HEADS * MLA_QK), BF16),
            _sds((r, MLA_HEADS * V_DIM), BF16)]
    extra = [mw["wk_lat"], mw["wk_pe"], w_uv.reshape(KV_LORA, MLA_HEADS * V_DIM).astype(BF16)]
    lat, q, k, v = _mla_proj(x, jnp.arange(t, dtype=I32), g_mix, q_norm, kv_norm, mw, extra,
                             _mla_proj_prompt_body, outs, "mla_proj_prompt")
    o = _mla_flash_t(q, k, v, b, t)
    y = _matmul_res([o], [w_out], x, "mla_out_proj")
    return y, lat.reshape(b, t, LAT)


def _mla_sample_kernel(pt_ref, pool_ref, q_ref, new_ref, o_ref, buf_ref, sem_ref, *,
                       layer, n_pages, s, chunk):
    slot = _paged_fetch(pool_ref, layer, pt_ref, n_pages, buf_ref, sem_ref)
    past = n_pages * PAGE_ROWS
    m = MLA_HEADS * s
    q = q_ref[:, 0].reshape(m, LAT).astype(BF16)
    state = _online_init(m, KV_LORA)
    for c0 in range(0, past, chunk):
        kv = buf_ref[slot, :, c0:c0 + chunk].astype(BF16)
        state = _online(state, _dot(q, kv), None, kv[:KV_LORA], v_t=True)
    new = new_ref[0].astype(BF16)
    causal = (lax.broadcasted_iota(I32, (s, s), 1) <= lax.broadcasted_iota(I32, (s, s), 0))
    state = _online(state, _dot_t(q, new), jnp.tile(causal, (MLA_HEADS, 1)), new[:, :KV_LORA])
    o = state[2] / jnp.maximum(state[1], TINY)
    o_ref[:, 0] = o.reshape(MLA_HEADS, s, KV_LORA)


def _mla_sample_attend(q, pool, layer, page_table, lat_new, sb, s):
    n_pages = page_table.shape[1]
    past = n_pages * PAGE_ROWS
    q4 = q.reshape(MLA_HEADS, sb, s, LAT)
    kern = functools.partial(_mla_sample_kernel, layer=layer, n_pages=n_pages, s=s,
                             chunk=min(2048, past))
    grid_spec = pltpu.PrefetchScalarGridSpec(
        num_scalar_prefetch=1, grid=(sb,),
        in_specs=[pl.BlockSpec(memory_space=pl.ANY),
                  pl.BlockSpec((MLA_HEADS, 1, s, LAT), lambda i, pt: (0, i, 0, 0)),
                  pl.BlockSpec((1, s, LAT), lambda i, pt: (i, 0, 0))],
        out_specs=pl.BlockSpec((MLA_HEADS, 1, s, KV_LORA), lambda i, pt: (0, i, 0, 0)),
        scratch_shapes=[pltpu.VMEM((2, LAT, past), F32), pltpu.SemaphoreType.DMA((2,))])
    o = pl.pallas_call(
        kern, grid_spec=grid_spec, out_shape=_sds((MLA_HEADS, sb, s, KV_LORA)),
        compiler_params=_params(("arbitrary",), 56), name="mla_paged_sample")(
            page_table.reshape(-1), _channel_major(pool), q4, lat_new.reshape(sb, s, LAT))
    return o.reshape(MLA_HEADS, sb * s, KV_LORA)


def _mla_out_sample_body(rows, consts, outs):
    o_ref, res_ref = rows
    wuv_ref, wout_ref = consts
    acc = res_ref[...]
    for hd in range(MLA_HEADS):
        o_h = _dot(o_ref[hd].astype(BF16), wuv_ref[hd])
        acc = acc + _dot(o_h.astype(BF16), wout_ref[hd])
    outs[0][...] = acc


def _odd_layer_sample(x, sb, s, layer, pool, page_table, g_mix, w_in, q_norm, w_uq, kv_norm, w_uk,
                      w_uv, w_out):
    mw = _mla_weights(w_in, w_uq, w_uk)
    r = x.shape[0]
    past = page_table.shape[1] * PAGE_ROWS
    pos = past + jnp.arange(_row_tile(r, 512), dtype=I32) % s
    outs = [_sds((r, LAT)), _sds((MLA_HEADS, r, LAT))]
    wukt = w_uk.transpose(1, 2, 0).astype(BF16)
    lat, q = _mla_proj(x, pos, g_mix, q_norm, kv_norm, mw, [wukt],
                       _mla_proj_sample_body, outs, "mla_proj_sample")
    o_lat = _mla_sample_attend(q, pool, layer, page_table, lat, sb, s)
    wuv = w_uv.transpose(1, 0, 2).astype(BF16)
    wout = w_out.reshape(MLA_HEADS, V_DIM, -1).astype(BF16)
    y = _rowwise(_mla_out_sample_body, [o_lat, x], [wuv, wout], [_sds(x.shape)],
                 _row_tile(r, 512), "mla_out_sample")[0]
    return y, lat.reshape(sb, s, LAT)


def _plain_matmul_body(rows, consts, outs):
    outs[0][...] = _dot(rows[0][...].astype(BF16), consts[0][...])


def _mem_q_body(rows, consts, outs):
    g_ref, w_ref = consts
    h = _rms(rows[0][...], g_ref[...]).astype(BF16)
    dh = w_ref.shape[1] // MEM_HEADS
    outs[0][...] = (_dot(h, w_ref[...]) * (dh ** -0.5 * LOG2E)).astype(BF16)


def _mem_attn_kernel(q_ref, kv_ref, o_ref, *, split):
    d = q_ref.shape[2]
    dh = d // MEM_HEADS
    for hd in range(MEM_HEADS):
        if split:
            parts = dh // kv_ref.shape[5]
            rows = lambda kv: jnp.concatenate(
                [kv_ref[0, 0, :, kv, part * MEM_HEADS + hd, :] for part in range(parts)],
                axis=1).astype(BF16)
            k, v = rows(0), rows(1)
        else:
            k = kv_ref[0, :, hd * dh:(hd + 1) * dh].astype(BF16)
            v = kv_ref[0, :, d + hd * dh:d + (hd + 1) * dh].astype(BF16)
        s = _dot_t(q_ref[0, :, hd * dh:(hd + 1) * dh], k)
        p = jnp.exp2(s - jnp.max(s, axis=-1, keepdims=True))
        p = p / jnp.sum(p, axis=-1, keepdims=True)
        o_ref[0, :, hd * dh:(hd + 1) * dh] = _dot(p.astype(BF16), v).astype(BF16)


def _mem_layer(y, nb, kv, g_mem, w_q, w_o, layer=None):
    r, d = y.shape
    rows = r // nb
    q = _rowwise(_mem_q_body, [y], [g_mem.reshape(1, d), w_q.astype(BF16)], [_sds((r, d), BF16)],
                 _row_tile(r, 512), "mem_q_proj")[0]
    tq = _row_tile(rows, 512)
    nt = rows // tq
    if layer is None:
        kv_spec = pl.BlockSpec((1,) + kv.shape[1:], lambda i, j: (i, 0, 0))
    else:
        nl, _, mem, two, heads, dh = kv.shape
        parts = dh // V7X_LANES
        kv = kv.reshape(nl, nb, mem, two, heads, parts, V7X_LANES).transpose(0, 1, 2, 3, 5, 4, 6)
        kv = kv.reshape(nl, nb, mem, two, parts * heads, V7X_LANES)
        kv_spec = pl.BlockSpec((1, 1) + kv.shape[2:], lambda i, j: (layer, i, 0, 0, 0, 0))
    o = pl.pallas_call(
        functools.partial(_mem_attn_kernel, split=layer is not None), grid=(nb, nt),
        in_specs=[pl.BlockSpec((1, tq, d), lambda i, j: (i, j, 0)), kv_spec],
        out_specs=pl.BlockSpec((1, tq, d), lambda i, j: (i, j, 0)),
        out_shape=_sds((nb, rows, d), BF16),
        compiler_params=_params(("arbitrary", "arbitrary"), 48), name="mem_attention")(
            q.reshape(nb, rows, d), kv)
    return _matmul_res([o.reshape(r, d)], [w_o], y, "mem_out_proj")


def _mem_project(mem, w_kv):
    nb, m, d = mem.shape
    kv = _rowwise(_plain_matmul_body, [mem.reshape(nb * m, d)], [w_kv.astype(BF16)],
                  [_sds((nb * m, w_kv.shape[1]))], _row_tile(nb * m, 512), "mem_kv_proj")[0]
    return kv.reshape(nb, m, w_kv.shape[1])


def _route(x, wr_ref, br_ref):
    xh, xm, _ = _split3(x)
    w0, w1 = wr_ref[0], wr_ref[1]
    logit = _dot(xh, w0) + (_dot(xh, w1) + _dot(xm, w0)) + br_ref[...]
    lane = lax.broadcasted_iota(I32, logit.shape, 1).astype(F32)
    big = float(ROUTE_LANES)
    is_g = lane < N_GROUPS
    gl = jnp.where(is_g, logit, -jnp.inf)
    gmax = jnp.max(gl, axis=-1, keepdims=True)
    gidx = jnp.min(jnp.where(gl == gmax, lane, big), axis=-1, keepdims=True)
    g_gate = 1.0 / jnp.sum(jnp.where(is_g, jnp.exp(gl - gmax), 0.0), axis=-1, keepdims=True)
    lo = N_GROUPS + gidx * EXP_PER_GROUP
    in_grp = (lane >= lo) & (lane < lo + EXP_PER_GROUP)
    el = jnp.where(in_grp, logit, -jnp.inf)
    ee = jnp.where(in_grp, jnp.exp(el - jnp.max(el, axis=-1, keepdims=True)), 0.0)
    prob = ee / jnp.sum(ee, axis=-1, keepdims=True)
    cand = jnp.where(in_grp, prob, -1.0)
    p1 = jnp.max(cand, axis=-1, keepdims=True)
    i1 = jnp.min(jnp.where(cand == p1, lane, big), axis=-1, keepdims=True)
    cand = jnp.where(lane == i1, -1.0, cand)
    p2 = jnp.max(cand, axis=-1, keepdims=True)
    i2 = jnp.min(jnp.where(cand == p2, lane, big), axis=-1, keepdims=True)
    tot = p1 + p2
    return (jnp.where(lane == i1, p1 / tot * g_gate, 0.0)
            + jnp.where(lane == i2, p2 / tot * g_gate, 0.0))


def _moe_kernel(y_ref, g_ref, wr_ref, br_ref, w13_ref, w2_ref, gf_ref, o_ref, h_ref, comb_ref,
                acc_ref, *, final_norm):
    grp = pl.program_id(1)

    @pl.when(grp == 0)
    def _():
        x = _rms(y_ref[...], g_ref[...])
        h_ref[...] = x.astype(BF16)
        comb_ref[...] = _route(x, wr_ref, br_ref)
        acc_ref[...] = jnp.zeros(acc_ref.shape, F32)

    hc = _dot(h_ref[...], w13_ref[0])
    half = hc.shape[1] // 2
    ff = half // EXP_PER_GROUP
    comb = comb_ref[...]
    lane = lax.broadcasted_iota(I32, comb.shape, 1)
    acts = []
    for j in range(EXP_PER_GROUP):
        e_lane = N_GROUPS + grp * EXP_PER_GROUP + j
        ce = jnp.sum(jnp.where(lane == e_lane, comb, 0.0), axis=-1, keepdims=True)
        a = hc[:, j * ff:(j + 1) * ff]
        bb = hc[:, half + j * ff:half + (j + 1) * ff]
        acts.append((a * (1.0 / (1.0 + jnp.exp(-a))) * bb * ce).astype(BF16))
    acc_ref[...] += _dot(jnp.concatenate(acts, axis=1), w2_ref[0])

    @pl.when(grp == pl.num_programs(1) - 1)
    def _():
        out = y_ref[...] + acc_ref[...]
        if final_norm:
            out = _rms(out, gf_ref[...])
        o_ref[...] = out


def _moe_layer(y, g_ffn, w_group, b_group, w_expert, b_expert, w13, w2, g_final=None):
    r, d = y.shape
    tm = _row_tile(r, 1024)
    n_route = N_GROUPS + N_EXPERTS
    wr = jnp.concatenate([w_group, w_expert, jnp.zeros((d, ROUTE_LANES - n_route), F32)], axis=1)
    wr = jnp.stack(_split3(wr)[:2])
    br = jnp.concatenate([b_group, b_expert, jnp.zeros((ROUTE_LANES - n_route,), F32)]).reshape(1, -1)
    gf = (g_ffn if g_final is None else g_final).reshape(1, d)
    ff = w13.shape[2] // 2
    per = EXP_PER_GROUP
    w13g = w13.reshape(N_GROUPS, per, d, 2, ff).transpose(0, 2, 3, 1, 4).reshape(
        N_GROUPS, d, 2 * per * ff).astype(BF16)
    w2g = w2.reshape(N_GROUPS, per * ff, d).astype(BF16)
    return pl.pallas_call(
        functools.partial(_moe_kernel, final_norm=g_final is not None), grid=(r // tm, N_GROUPS),
        in_specs=[pl.BlockSpec((tm, d), lambda i, e: (i, 0)),
                  pl.BlockSpec((1, d), lambda i, e: (0, 0)),
                  pl.BlockSpec(wr.shape, lambda i, e: (0, 0, 0)),
                  pl.BlockSpec(br.shape, lambda i, e: (0, 0)),
                  pl.BlockSpec((1, d, 2 * per * ff), lambda i, e: (e, 0, 0)),
                  pl.BlockSpec((1, per * ff, d), lambda i, e: (e, 0, 0)),
                  pl.BlockSpec((1, d), lambda i, e: (0, 0))],
        out_specs=pl.BlockSpec((tm, d), lambda i, e: (i, 0)),
        out_shape=_sds((r, d)),
        scratch_shapes=[pltpu.VMEM((tm, d), BF16), pltpu.VMEM((tm, ROUTE_LANES), F32),
                        pltpu.VMEM((tm, d), F32)],
        compiler_params=_params(("arbitrary", "arbitrary"), 56), name="hier_moe")(
            y, g_ffn.reshape(1, d), wr, br, w13g, w2g, gf)


def kernel(x_prompt, x_sample, mem_prompt, cache_nsa_cmp_kv, cache_nsa_slc_kv, state_nsa_win_kv, state_conv, cache_mla_latent, cache_mem_kv, page_table, norm_mix, norm_mem, norm_ffn, norm_final, e_w_in, e_w_out, e_cmp_w, e_cmp_pe, e_conv_w, o_w_in, o_q_norm, o_w_uq, o_kv_norm, o_w_uk, o_w_uv, o_w_out, mem_w_q, mem_w_kv, mem_w_o, moe_w_group, moe_b_group, moe_w_expert, moe_b_expert, moe_w13, moe_w2):
    b, t, d = x_prompt.shape
    sb, s, _ = x_sample.shape
    depth = norm_mix.shape[0]
    yp = x_prompt.reshape(b * t, d)
    ys = x_sample.reshape(sb * s, d)
    p_even, s_even, p_mla, s_mla, p_mem = [], [], [], [], []
    for l in range(depth):
        if l % 2 == 0:
            e = l // 2
            w = (norm_mix[l], e_w_in[e], e_w_out[e], e_cmp_w[e], e_cmp_pe[e], e_conv_w[e])
            yp, cp = _even_layer_prompt(yp, b, t, *w)
            ys, cs = _even_layer_sample(ys, sb, s, e, cache_nsa_cmp_kv, cache_nsa_slc_kv,
                                        state_nsa_win_kv[e], state_conv[e], page_table, *w)
            p_even.append(cp)
            s_even.append(cs)
        else:
            o = l // 2
            w = (norm_mix[l], o_w_in[o], o_q_norm[o], o_w_uq[o], o_kv_norm[o], o_w_uk[o], o_w_uv[o],
                 o_w_out[o])
            yp, lp = _odd_layer_prompt(yp, b, t, *w)
            ys, ls = _odd_layer_sample(ys, sb, s, o, cache_mla_latent, page_table, *w)
            p_mla.append(lp)
            s_mla.append(ls)
        kv_p = _mem_project(mem_prompt, mem_w_kv[l])
        p_mem.append(kv_p.reshape(b, kv_p.shape[1], 2, MEM_HEADS, d // MEM_HEADS))
        yp = _mem_layer(yp, b, kv_p, norm_mem[l], mem_w_q[l], mem_w_o[l])
        ys = _mem_layer(ys, sb, cache_mem_kv, norm_mem[l], mem_w_q[l], mem_w_o[l], layer=l)
        g_final = norm_final if l == depth - 1 else None
        moe_w = (norm_ffn[l], moe_w_group[l], moe_b_group[l], moe_w_expert[l], moe_b_expert[l],
                 moe_w13[l], moe_w2[l])
        yp = _moe_layer(yp, *moe_w, g_final=g_final)
        ys = _moe_layer(ys, *moe_w, g_final=g_final)
    stack = lambda caches, k: jnp.stack([c[k] for c in caches])
    return (yp.reshape(b, t, d), ys.reshape(sb, s, d),
            stack(p_even, 0), stack(p_even, 1), stack(p_even, 2), stack(p_even, 3),
            jnp.stack(p_mla), jnp.stack(p_mem),
            stack(s_even, 0), stack(s_even, 1), stack(s_even, 2), stack(s_even, 3),
            jnp.stack(s_mla))
```

```python
import functools
import math

import jax
import jax.numpy as jnp
from jax import lax
from jax.experimental import pallas as pl
from jax.experimental.pallas import tpu as pltpu

F32 = jnp.float32
BF16 = jnp.bfloat16
I32 = jnp.int32

EPS = 1e-6
LOG2E = math.log2(math.e)
NEG = -1e30
TINY = 1e-30
FORCE_BONUS = 1e4

NSA_HEADS = 8
NSA_GROUPS = 2
NSA_REP = NSA_HEADS // NSA_GROUPS
NSA_DH = 64
NSA_KV_ROW = 2 * NSA_GROUPS * NSA_DH
CMP_STRIDE = 16
CMP_BLOCK = 32
SEL_BLOCK = 64
SEL_TOPK = 16
WINDOW = 512
CONV_CH = 512
CONV_W = 3
MLA_HEADS = 16
Q_LORA = 384
KV_LORA = 256
NOPE = 64
ROPE = 32
V_DIM = 64
ROPE_THETA = 10000.0
LAT = KV_LORA + ROPE
MLA_QK = 128
MEM_HEADS = 4
N_GROUPS = 4
EXP_PER_GROUP = 4
N_EXPERTS = 16
EXPERT_FF = 256
ROUTE_LANES = 128

V7X_VMEM_BYTES = 64 * 2**20
V7X_LANES = 128


def _params(sem, vmem_mb):
    assert vmem_mb * 2**20 < V7X_VMEM_BYTES
    return pltpu.CompilerParams(dimension_semantics=sem, vmem_limit_bytes=vmem_mb * 2**20)


def _dot(a, b):
    return jnp.dot(a, b, preferred_element_type=F32)


def _dot_t(a, b):
    return lax.dot_general(a, b, (((1,), (1,)), ((), ())), preferred_element_type=F32)


def _tdot(a, b):
    return lax.dot_general(a, b, (((0,), (0,)), ((), ())), preferred_element_type=F32)


def _split3(x):
    hi = x.astype(BF16)
    r1 = x - hi.astype(F32)
    mid = r1.astype(BF16)
    lo = (r1 - mid.astype(F32)).astype(BF16)
    return hi, mid, lo


def _rms(x, g):
    return x * lax.rsqrt(jnp.mean(x * x, axis=-1, keepdims=True) + EPS) * g


def _row_spec(shape, tm, period=None):
    lead = len(shape) - 2
    blk = tuple(shape[:lead]) + (tm, shape[-1])
    if period is None:
        return pl.BlockSpec(blk, lambda i, lead=lead: (0,) * lead + (i, 0))
    return pl.BlockSpec(blk, lambda i, lead=lead, p=period: (0,) * lead + (i % p, 0))


def _const_spec(shape):
    return pl.BlockSpec(tuple(shape), lambda i, n=len(shape): (0,) * n)


def _seq_col_spec(shape, tm):
    lead = len(shape) - 3
    nt = shape[-1] // tm
    blk = tuple(shape[:lead]) + (1, shape[-2], tm)
    return pl.BlockSpec(blk, lambda i, lead=lead, nt=nt: (0,) * lead + (i // nt, 0, i % nt))


def _rowwise(body, rows, consts, outs, tm, name, vmem_mb=48, periods=None, seq_cols=()):
    n_rows = rows[0].shape[-2]
    assert n_rows % tm == 0
    periods = periods or {}
    in_specs = [_row_spec(a.shape, tm, periods.get(k)) for k, a in enumerate(rows)]
    in_specs += [_const_spec(c.shape) for c in consts]
    out_specs = [_seq_col_spec(o.shape, tm) if k in seq_cols else _row_spec(o.shape, tm)
                 for k, o in enumerate(outs)]
    nr, nc = len(rows), len(consts)

    def kern(*refs):
        body(refs[:nr], refs[nr:nr + nc], refs[nr + nc:])

    return pl.pallas_call(
        kern, grid=(n_rows // tm,), in_specs=in_specs, out_specs=out_specs, out_shape=outs,
        compiler_params=_params(("arbitrary",), vmem_mb), name=name)(*rows, *consts)


def _sds(shape, dtype=F32):
    return jax.ShapeDtypeStruct(tuple(shape), dtype)


def _row_tile(n_rows, want):
    tm = min(want, n_rows)
    assert n_rows % tm == 0
    return tm


def _even_proj_body(channel_major, rows, consts, outs):
    (x_ref,) = rows
    g_ref, wq_ref, wkv_ref, wg_ref, wc_ref = consts[:5]
    if channel_major:
        q_ref, kvc_ref, kvt_ref, kvb_ref, gate_ref, gb_ref, u_ref = outs
    else:
        q_ref, kvc_ref, kvs_ref, kvw_ref, kvb_ref, gate_ref, gb_ref, u_ref = outs
    h = _rms(x_ref[...], g_ref[...]).astype(BF16)
    q = _dot(h, wq_ref[...]) * (NSA_DH ** -0.5 * LOG2E)
    for hd in range(NSA_HEADS):
        q_ref[hd // NSA_REP, hd % NSA_REP] = q[:, hd * NSA_DH:(hd + 1) * NSA_DH].astype(BF16)
    kv = _dot(h, wkv_ref[...])
    kvc_ref[...] = kv[:, :NSA_KV_ROW]
    if channel_major:
        kvt = _dot_t(consts[5][...], h)
        for j in range(3):
            kvt_ref[j, 0] = kvt[j * NSA_KV_ROW:(j + 1) * NSA_KV_ROW]
    else:
        kvs_ref[...] = kv[:, NSA_KV_ROW:2 * NSA_KV_ROW]
        kvw_ref[...] = kv[:, 2 * NSA_KV_ROW:]
    for j in range(3 * 2 * NSA_GROUPS):
        kvb_ref[j] = kv[:, j * NSA_DH:(j + 1) * NSA_DH].astype(BF16)
    gate_ref[...] = 1.0 / (1.0 + jnp.exp(-_dot(h, wg_ref[...])))
    c = _dot(h, wc_ref[...])
    gb_ref[...] = c[:, :CONV_CH]
    u_ref[...] = c[:, CONV_CH:2 * CONV_CH] * c[:, 2 * CONV_CH:]


def _even_proj(x, g, w_in, seq_len=None):
    r, d = x.shape
    qc = NSA_HEADS * NSA_DH
    kc = 3 * NSA_KV_ROW
    gc = NSA_HEADS * 3
    wq = w_in[:, :qc].astype(BF16)
    wkv = w_in[:, qc:qc + kc].astype(BF16)
    wg = w_in[:, qc + kc:qc + kc + gc].astype(BF16)
    wc = w_in[:, qc + kc + gc:].astype(BF16)
    consts = [g.reshape(1, d), wq, wkv, wg, wc]
    tail = [_sds((12, r, NSA_DH), BF16), _sds((r, gc)), _sds((r, CONV_CH)), _sds((r, CONV_CH))]
    head = [_sds((NSA_GROUPS, NSA_REP, r, NSA_DH), BF16), _sds((r, NSA_KV_ROW))]
    if seq_len is None:
        tm = _row_tile(r, 512)
        outs = head + [_sds((r, NSA_KV_ROW)), _sds((r, NSA_KV_ROW))] + tail
        seq_cols = ()
    else:
        tm = _row_tile(seq_len, 512)
        outs = head + [_sds((3, r // seq_len, NSA_KV_ROW, seq_len))] + tail
        seq_cols = (2,)
        consts.append(wkv.T)
    return _rowwise(functools.partial(_even_proj_body, seq_len is not None), [x], consts, outs, tm,
                    "even_proj", seq_cols=seq_cols)


def _compress_weights(cmp_w, cmp_pe):
    eye_g = jnp.eye(NSA_GROUPS, dtype=F32)
    eye_k = jnp.eye(2, dtype=F32)

    def big(wh):
        return jnp.einsum("kjde,kl,gh->jkgdlhe", wh, eye_k, eye_g).reshape(
            CMP_STRIDE * NSA_KV_ROW, NSA_KV_ROW)

    w = jnp.concatenate([big(cmp_w[:, :CMP_STRIDE]), big(cmp_w[:, CMP_STRIDE:])], axis=1)

    def pe_row(peh):
        return jnp.broadcast_to(peh.transpose(1, 0, 2)[:, :, None, :],
                                (CMP_STRIDE, 2, NSA_GROUPS, NSA_DH)).reshape(1, -1)

    pe = jnp.concatenate([pe_row(cmp_pe[:, :CMP_STRIDE]), pe_row(cmp_pe[:, CMP_STRIDE:]),
                          jnp.zeros((6, CMP_STRIDE * NSA_KV_ROW), F32)], axis=0)
    return w.astype(BF16), pe


def _compress_pair_weights(cmp_w, cmp_pe):
    pairs = CMP_STRIDE // 2
    w = cmp_w.reshape(2, 2, pairs, 2, NSA_DH, NSA_DH)
    w = jnp.einsum("khpjde,gG->kpjgdhGe", w, jnp.eye(NSA_GROUPS, dtype=F32))
    w = w.reshape(2, pairs, 2 * NSA_GROUPS * NSA_DH, 2 * NSA_GROUPS * NSA_DH)
    pe = cmp_pe.reshape(2, 2, pairs, 2, 1, NSA_DH)
    pe = jnp.broadcast_to(pe, (2, 2, pairs, 2, NSA_GROUPS, NSA_DH)).transpose(0, 2, 1, 3, 4, 5)
    pe = pe.reshape(2, pairs, 2, 2 * NSA_GROUPS * NSA_DH)
    pe = jnp.concatenate([pe, jnp.zeros((2, pairs, 6, pe.shape[-1]), F32)], axis=2)
    return w.astype(BF16), pe


def _compress_core(x, w_ref, pe_ref):
    return _compress_finish(_dot(x, w_ref[...]), w_ref, pe_ref)


def _compress_finish(p, w_ref, pe_ref):
    n = p.shape[0]
    w = w_ref[...]
    hi, mid, lo = _split3(pe_ref[...])
    pb = _dot(hi, w) + _dot(mid, w) + _dot(lo, w)
    bias = pb[0:1, :NSA_KV_ROW] + pb[1:2, NSA_KV_ROW:]
    return p[:, :NSA_KV_ROW] + pltpu.roll(p[:, NSA_KV_ROW:], n - 1, 0) + bias


def _compress_prompt_kernel(x_ref, w_ref, pe_ref, o_ref):
    c = _compress_core(x_ref[0].astype(BF16), w_ref, pe_ref)
    for j in range(2 * NSA_GROUPS):
        o_ref[0, j] = c[:, j * NSA_DH:(j + 1) * NSA_DH].astype(BF16)


def _compress_prompt(kvc, b, t, w, pe):
    n = t // CMP_STRIDE
    x = kvc.reshape(b, n, CMP_STRIDE * NSA_KV_ROW)
    return pl.pallas_call(
        _compress_prompt_kernel, grid=(b,),
        in_specs=[pl.BlockSpec((1, n, CMP_STRIDE * NSA_KV_ROW), lambda i: (i, 0, 0)),
                  pl.BlockSpec(w.shape, lambda i: (0, 0)), pl.BlockSpec(pe.shape, lambda i: (0, 0))],
        out_specs=pl.BlockSpec((1, 4, n, NSA_DH), lambda i: (i, 0, 0, 0)),
        out_shape=_sds((b, 4, n, NSA_DH), BF16),
        compiler_params=_params(("arbitrary",), 48), name="nsa_compress_prompt")(x, w, pe)


def _overlap_matrix(n_cmp, n_sel):
    i = jnp.arange(n_cmp)[:, None]
    j = jnp.arange(n_sel)[None, :]
    per = SEL_BLOCK // CMP_STRIDE
    return ((i // per == j).astype(F32) + ((i + 1) // per == j).astype(F32)).astype(BF16)


def _topk_mask(score, k):
    n = score.shape[-1]
    lane = lax.broadcasted_iota(I32, score.shape, score.ndim - 1).astype(F32)
    sel = jnp.zeros_like(score)
    for _ in range(k):
        m = jnp.max(score, axis=-1, keepdims=True)
        first = jnp.min(jnp.where(score == m, lane, float(n)), axis=-1, keepdims=True)
        pick = lane == first
        sel = jnp.where(pick, 1.0, sel)
        score = jnp.where(pick, -jnp.inf, score)
    return sel


def _masked_softmax(s, mask):
    s = jnp.where(mask, s, NEG)
    m = jnp.max(s, axis=-1, keepdims=True)
    p = jnp.where(mask, jnp.exp2(s - m), 0.0)
    return p / jnp.maximum(jnp.sum(p, axis=-1, keepdims=True), TINY)


def _cmp_branch(q, ck, cv, qpos, ovl):
    tq = qpos.shape[0]
    n = ck.shape[0]
    n_sel = ovl.shape[1]
    cmp_end = CMP_STRIDE * lax.broadcasted_iota(I32, (tq, n), 1) + (CMP_BLOCK - 1)
    cmask = (cmp_end <= qpos)[None]
    s = _dot_t(q, ck).reshape(NSA_REP, tq, n)
    p = _masked_softmax(s, cmask)
    o = _dot(p.reshape(NSA_REP * tq, n).astype(BF16), cv)
    imp = jnp.sum(p, axis=0)
    hi, mid, lo = _split3(imp)
    sel_imp = _dot(hi, ovl) + _dot(mid, ovl) + _dot(lo, ovl)
    blk = lax.broadcasted_iota(I32, (tq, n_sel), 1)
    qblk = qpos // SEL_BLOCK
    forced = (blk == 0) | (blk == qblk) | (blk == qblk - 1)
    return o, jnp.where(forced, sel_imp + FORCE_BONUS, jnp.where(blk <= qblk, sel_imp, NEG))


def _select_blocks(scores, topk, sel_ref, lead):
    tq, n_sel = scores[0].shape
    sel = _topk_mask(jnp.concatenate(scores, axis=0), topk).astype(BF16)
    for g in range(NSA_GROUPS):
        sel_ref[lead + (slice(None), slice(g * n_sel, (g + 1) * n_sel))] = sel[g * tq:(g + 1) * tq]


def _nsa_cmp_prompt_kernel(q_ref, c_ref, ovl_ref, o_ref, sel_ref, *, tq):
    qpos = pl.program_id(1) * tq + lax.broadcasted_iota(I32, (tq, 1), 0)
    scores = []
    for g in range(NSA_GROUPS):
        q = q_ref[g, :, 0].reshape(NSA_REP * tq, NSA_DH)
        o, score = _cmp_branch(q, c_ref[0, g], c_ref[0, NSA_GROUPS + g], qpos, ovl_ref[...])
        scores.append(score)
        for r in range(NSA_REP):
            hd = g * NSA_REP + r
            o_ref[:, hd * NSA_DH:(hd + 1) * NSA_DH] = o[r * tq:(r + 1) * tq]
    _select_blocks(scores, min(SEL_TOPK, ovl_ref.shape[1]), sel_ref, ())


def _nsa_cmp_prompt(q, cmp, b, t, tq=256):
    n = t // CMP_STRIDE
    n_sel = t // SEL_BLOCK
    tq = min(tq, t)
    nq = t // tq
    ovl = _overlap_matrix(n, n_sel)
    q5 = q.reshape(NSA_GROUPS, NSA_REP, b, t, NSA_DH)
    return pl.pallas_call(
        functools.partial(_nsa_cmp_prompt_kernel, tq=tq), grid=(b, nq),
        in_specs=[pl.BlockSpec((NSA_GROUPS, NSA_REP, 1, tq, NSA_DH), lambda i, j: (0, 0, i, j, 0)),
                  pl.BlockSpec((1, 4, n, NSA_DH), lambda i, j: (i, 0, 0, 0)),
                  pl.BlockSpec(ovl.shape, lambda i, j: (0, 0))],
        out_specs=[pl.BlockSpec((tq, NSA_HEADS * NSA_DH), lambda i, j, nq=nq: (i * nq + j, 0)),
                   pl.BlockSpec((tq, NSA_GROUPS * n_sel), lambda i, j, nq=nq: (i * nq + j, 0))],
        out_shape=[_sds((b * t, NSA_HEADS * NSA_DH)), _sds((b * t, NSA_GROUPS * n_sel), BF16)],
        compiler_params=_params(("arbitrary", "arbitrary"), 48), name="nsa_cmp_prompt")(q5, cmp, ovl)


def _flash_update_t(q, k, v, mask_t, m_ref, acc_ref):
    s = _dot_t(k, q)
    if mask_t is not None:
        s = jnp.where(mask_t, s, NEG)
    m_prev = m_ref[...]
    m_new = jnp.maximum(m_prev, jnp.max(s, axis=0, keepdims=True))
    p = jnp.exp2(s - jnp.maximum(m_new, 0.1 * NEG))
    alpha = jnp.exp2(m_prev - m_new)
    v1 = jnp.concatenate([v, jnp.ones_like(v)], axis=1)
    acc_ref[...] = alpha * acc_ref[...] + _tdot(v1, p.astype(BF16))
    m_ref[...] = m_new


def _flash_result_t(acc):
    dh = acc.shape[0] // 2
    return acc[:dh] / jnp.maximum(acc[dh:dh + 1], TINY)


def _causal_steps(nq, tq, tk):
    qi, ki = [], []
    for j in range(nq):
        for k in range((j * tq + tq - 1) // tk + 1):
            qi.append(j)
            ki.append(k)
    return jnp.asarray(qi, I32), jnp.asarray(ki, I32)


def _nsa_swt_prompt_kernel(qi_ref, ki_ref, q_ref, ks_ref, vs_ref, kw_ref, vw_ref, sel_ref, ocmp_ref,
                           gate_ref, o_ref, m_ref, acc_ref, *, tq, tk):
    n = pl.program_id(1)
    qi = qi_ref[n]
    ki = ki_ref[n]
    q0 = qi * tq
    last = (q0 + tq - 1) // tk
    first_w = jnp.maximum(q0 - (WINDOW - 1), 0) // tk
    n_sel = sel_ref.shape[1] // NSA_GROUPS
    n_q = NSA_REP * tq

    @pl.when(ki == 0)
    def _():
        m_ref[...] = jnp.full(m_ref.shape, NEG, F32)
        acc_ref[...] = jnp.zeros(acc_ref.shape, F32)

    kpos = ki * tk + lax.broadcasted_iota(I32, (tk, tq), 0)
    qpos = q0 + lax.broadcasted_iota(I32, (tk, tq), 1)
    causal = kpos <= qpos
    heads = lambda mask: jnp.concatenate([mask] * NSA_REP, axis=1)

    expand = (lax.broadcasted_iota(I32, (tk, n_sel), 1)
              == (ki * tk + lax.broadcasted_iota(I32, (tk, n_sel), 0)) // SEL_BLOCK)
    expand = jnp.where(expand, 1.0, 0.0).astype(BF16)
    for g in range(NSA_GROUPS):
        q = q_ref[g, :, 0].reshape(n_q, NSA_DH)
        chosen = _dot_t(expand, sel_ref[:, g * n_sel:(g + 1) * n_sel])
        mask = jnp.where(causal, chosen, 0.0) > 0.5
        _flash_update_t(q, ks_ref[g, 0], vs_ref[g, 0], heads(mask), m_ref.at[g], acc_ref.at[g])

    @pl.when(ki >= first_w)
    def _():
        mask = heads(jnp.where(causal, qpos - kpos, WINDOW) < WINDOW)
        for g in range(NSA_GROUPS):
            q = q_ref[g, :, 0].reshape(n_q, NSA_DH)
            _flash_update_t(q, kw_ref[g, 0], vw_ref[g, 0], mask, m_ref.at[NSA_GROUPS + g],
                            acc_ref.at[NSA_GROUPS + g])

    @pl.when(ki == last)
    def _():
        gates = gate_ref[...]
        ocmp = ocmp_ref[...]
        for g in range(NSA_GROUPS):
            o_s = _flash_result_t(acc_ref[g])
            o_w = _flash_result_t(acc_ref[NSA_GROUPS + g])
            for r in range(NSA_REP):
                hd = g * NSA_REP + r
                lanes = slice(r * tq, (r + 1) * tq)
                cols = slice(hd * NSA_DH, (hd + 1) * NSA_DH)
                o_ref[:, cols] = (gates[:, 3 * hd:3 * hd + 1] * ocmp[:, cols]
                                  + gates[:, 3 * hd + 1:3 * hd + 2] * o_s[:, lanes].T
                                  + gates[:, 3 * hd + 2:3 * hd + 3] * o_w[:, lanes].T)


def _nsa_swt_prompt(q, kvb, sel, o_cmp, gates, b, t, tq=256, tk=1024):
    tq = min(tq, t)
    tk = min(tk, t)
    nq = t // tq
    qi, ki = _causal_steps(nq, tq, tk)
    q5 = q.reshape(NSA_GROUPS, NSA_REP, b, t, NSA_DH)
    kv4 = kvb.reshape(12, b, t, NSA_DH)

    def slc_idx(blk):
        return lambda i, n, qt, kt: (blk, i, kt[n], 0)

    def win_idx(blk):
        def f(i, n, qt, kt):
            first = jnp.maximum(qt[n] * tq - (WINDOW - 1), 0) // tk
            return (blk, i, jnp.maximum(kt[n], first), 0)
        return f

    kv_blk = (NSA_GROUPS, 1, tk, NSA_DH)
    row = lambda i, n, qt, kt: (i * nq + qt[n], 0)
    n_q = NSA_REP * tq
    grid_spec = pltpu.PrefetchScalarGridSpec(
        num_scalar_prefetch=2, grid=(b, qi.shape[0]),
        in_specs=[pl.BlockSpec((NSA_GROUPS, NSA_REP, 1, tq, NSA_DH),
                               lambda i, n, qt, kt: (0, 0, i, qt[n], 0)),
                  pl.BlockSpec(kv_blk, slc_idx(2)), pl.BlockSpec(kv_blk, slc_idx(3)),
                  pl.BlockSpec(kv_blk, win_idx(4)), pl.BlockSpec(kv_blk, win_idx(5)),
                  pl.BlockSpec((tq, sel.shape[1]), row),
                  pl.BlockSpec((tq, o_cmp.shape[1]), row),
                  pl.BlockSpec((tq, gates.shape[1]), row)],
        out_specs=pl.BlockSpec((tq, NSA_HEADS * NSA_DH), row),
        scratch_shapes=[pltpu.VMEM((2 * NSA_GROUPS, 1, n_q), F32),
                        pltpu.VMEM((2 * NSA_GROUPS, 2 * NSA_DH, n_q), F32)])
    return pl.pallas_call(
        functools.partial(_nsa_swt_prompt_kernel, tq=tq, tk=tk), grid_spec=grid_spec,
        out_shape=_sds((b * t, NSA_HEADS * NSA_DH)),
        compiler_params=_params(("arbitrary", "arbitrary"), 48),
        name="nsa_slc_win_prompt")(qi, ki, q5, kv4, kv4, kv4, kv4, sel, o_cmp, gates)


def _conv_prompt_kernel(u_ref, gb_ref, w_ref, o_ref, carry_ref):
    tm = u_ref.shape[0]

    @pl.when(pl.program_id(1) == 0)
    def _():
        carry_ref[...] = jnp.zeros(carry_ref.shape, F32)

    u = u_ref[...]
    prev = carry_ref[...]
    row = lax.broadcasted_iota(I32, u.shape, 0)
    u1 = jnp.where(row == 0, prev[7:8], pltpu.roll(u, 1, 0))
    u2 = jnp.where(row == 0, prev[6:7], jnp.where(row == 1, prev[7:8], pltpu.roll(u, 2, 0)))
    w = w_ref[...]
    o_ref[...] = gb_ref[...] * (w[0:1] * u2 + w[1:2] * u1 + w[2:3] * u)
    carry_ref[...] = u[tm - 8:]


def _conv_prompt(u, gb, conv_w, b, t, tm=1024):
    tm = min(tm, t)
    nt = t // tm
    c = u.shape[1]
    row = lambda i, j: (i * nt + j, 0)
    return pl.pallas_call(
        _conv_prompt_kernel, grid=(b, nt),
        in_specs=[pl.BlockSpec((tm, c), row), pl.BlockSpec((tm, c), row),
                  pl.BlockSpec(conv_w.shape, lambda i, j: (0, 0))],
        out_specs=pl.BlockSpec((tm, c), row), out_shape=_sds(u.shape),
        scratch_shapes=[pltpu.VMEM((8, c), F32)],
        compiler_params=_params(("arbitrary", "arbitrary"), 32), name="conv_prompt")(u, gb, conv_w)


def _matmul_res_body(n_a, rows, consts, outs):
    acc = rows[n_a][...]
    for k in range(n_a):
        acc = acc + _dot(rows[k][...].astype(BF16), consts[k][...])
    outs[0][...] = acc


def _matmul_res(acts, weights, res, name):
    r = res.shape[0]
    tm = _row_tile(r, 512)
    ws = [w.astype(BF16) for w in weights]
    return _rowwise(functools.partial(_matmul_res_body, len(acts)), list(acts) + [res], ws,
                    [_sds(res.shape)], tm, name)[0]


def _even_layer_prompt(x, b, t, g_mix, w_in, w_out, cmp_w, cmp_pe, conv_w):
    q, kvc, kvt, kvb, gates, gb, u = _even_proj(x, g_mix, w_in, seq_len=t)
    cw, cpe = _compress_weights(cmp_w, cmp_pe)
    cmp = _compress_prompt(kvc, b, t, cw, cpe)
    o_cmp, sel = _nsa_cmp_prompt(q, cmp, b, t)
    o_nsa = _nsa_swt_prompt(q, kvb, sel, o_cmp, gates, b, t)
    o_conv = _conv_prompt(u, gb, conv_w, b, t)
    qc = NSA_HEADS * NSA_DH
    y = _matmul_res([o_nsa, o_conv], [w_out[:qc], w_out[qc:]], x, "even_out_proj")
    wb = min(WINDOW, t)

    def rows(a):
        return a.reshape(b, 2, NSA_GROUPS, NSA_DH, a.shape[-1]).transpose(0, 4, 1, 2, 3)

    caches = (rows(kvt[0]), rows(kvt[1]), rows(kvt[2][:, :, t - wb:]),
              u.reshape(b, t, CONV_CH)[:, t - (CONV_W - 1):])
    return y, caches


def _channel_major(pool):
    nd = pool.ndim
    t = pool.transpose((0, 1) + tuple(range(3, nd)) + (2,))
    return t.reshape(pool.shape[0], pool.shape[1], -1, pool.shape[2])


def _page_copies(pool_ref, layer, pt_ref, batch, n_pages, buf_ref, slot, sem_ref):
    return [pltpu.make_async_copy(pool_ref.at[layer, pt_ref[batch * n_pages + p]],
                                  buf_ref.at[slot, :, pl.ds(p * PAGE_ROWS, PAGE_ROWS)],
                                  sem_ref.at[slot])
            for p in range(n_pages)]


def _paged_fetch(pool_ref, layer, pt_ref, n_pages, buf_ref, sem_ref):
    b = pl.program_id(0)
    slot = b % 2

    @pl.when(b == 0)
    def _():
        for c in _page_copies(pool_ref, layer, pt_ref, b, n_pages, buf_ref, slot, sem_ref):
            c.start()

    @pl.when(b + 1 < pl.num_programs(0))
    def _():
        for c in _page_copies(pool_ref, layer, pt_ref, b + 1, n_pages, buf_ref, 1 - slot, sem_ref):
            c.start()

    for c in _page_copies(pool_ref, layer, pt_ref, b, n_pages, buf_ref, slot, sem_ref):
        c.wait()
    return slot


def _online(state, s, mask, v, v_t=False):
    m_prev, l_prev, acc = state
    if mask is not None:
        s = jnp.where(mask, s, NEG)
    m_new = jnp.maximum(m_prev, jnp.max(s, axis=-1, keepdims=True))
    p = jnp.exp2(s - m_new)
    if mask is not None:
        p = jnp.where(mask, p, 0.0)
    alpha = jnp.exp2(m_prev - m_new)
    return (m_new, alpha * l_prev + jnp.sum(p, axis=-1, keepdims=True),
            alpha * acc + (_dot_t(p.astype(BF16), v) if v_t else _dot(p.astype(BF16), v)))


def _online_init(m, w):
    return (jnp.full((m, 1), NEG, F32), jnp.zeros((m, 1), F32), jnp.zeros((m, w), F32))


def _group_query_rows(q_ref, s):
    blocks = []
    for g in range(NSA_GROUPS):
        q = q_ref[g, :, 0].astype(F32).reshape(NSA_REP * s, NSA_DH)
        z = jnp.zeros_like(q)
        blocks.append(jnp.concatenate([q if j == g else z for j in range(2 * NSA_GROUPS)], axis=1))
    return jnp.concatenate(blocks, axis=0).astype(BF16)


def _group_rows(x):
    return jnp.concatenate([jnp.tile(xg, (NSA_REP, 1)) for xg in x], axis=0)


def _nsa_cmp_sample_kernel(pt_ref, pool_ref, q_ref, w_ref, pe_ref, ovl_ref, o_ref, sel_ref,
                           buf_ref, sem_ref, x_ref, bias_ref, *, layer, n_pages, s, topk):
    half = NSA_GROUPS * NSA_DH

    @pl.when(pl.program_id(0) == 0)
    def _():
        for kv in range(2):
            pb = None
            for jp in range(w_ref.shape[1]):
                for term in _split3(pe_ref[kv, jp]):
                    t = _dot(term, w_ref[kv, jp])
                    pb = t if pb is None else pb + t
            bias_ref[kv] = jnp.broadcast_to(pb[0:1, :half] + pb[1:2, half:], bias_ref.shape[1:])

    slot = _paged_fetch(pool_ref, layer, pt_ref, n_pages, buf_ref, sem_ref)
    past = n_pages * PAGE_ROWS
    n = past // CMP_STRIDE
    step = min(1024, past)
    for c0 in range(0, past, step):
        for kv in range(2):
            x_ref[kv, c0:c0 + step, :] = buf_ref[slot, kv * half:(kv + 1) * half, c0:c0 + step].T
    c = []
    for kv in range(2):
        p = None
        for jp in range(w_ref.shape[1]):
            x = jnp.concatenate(
                [x_ref.at[kv][pl.ds(2 * jp + jj, n, stride=CMP_STRIDE), :] for jj in range(2)],
                axis=1).astype(BF16)
            t = _dot(x, w_ref[kv, jp])
            p = t if p is None else p + t
        c.append(p[:, :half] + pltpu.roll(p[:, half:], n - 1, 0) + bias_ref[kv, 0:1])
    qpos = n_pages * PAGE_ROWS + lax.broadcasted_iota(I32, (s, 1), 0)
    scores = []
    for g in range(NSA_GROUPS):
        q = q_ref[g, :, 0].astype(F32).reshape(NSA_REP * s, NSA_DH).astype(BF16)
        ck = c[0][:, g * NSA_DH:(g + 1) * NSA_DH].astype(BF16)
        cv = c[1][:, g * NSA_DH:(g + 1) * NSA_DH].astype(BF16)
        o, score = _cmp_branch(q, ck, cv, qpos, ovl_ref[...])
        scores.append(score)
        for r in range(NSA_REP):
            hd = g * NSA_REP + r
            o_ref[0, :, hd * NSA_DH:(hd + 1) * NSA_DH] = o[r * s:(r + 1) * s]
    _select_blocks(scores, topk, sel_ref, (0,))


PAGE_ROWS = 128
PAGE_CHUNKS = PAGE_ROWS // CMP_STRIDE


def _nsa_cmp_sample(q, pool, layer, page_table, w, pe, sb, s):
    n_pages = page_table.shape[1]
    past = n_pages * PAGE_ROWS
    assert past % SEL_BLOCK == 0 and s <= CMP_STRIDE and pool.shape[2] == PAGE_ROWS
    n = past // CMP_STRIDE
    n_sel = past // SEL_BLOCK
    ovl = _overlap_matrix(n, n_sel)
    pool4 = _channel_major(pool)
    q5 = q.reshape(NSA_GROUPS, NSA_REP, sb, s, NSA_DH)
    kern = functools.partial(_nsa_cmp_sample_kernel, layer=layer, n_pages=n_pages, s=s,
                             topk=SEL_TOPK - 1)
    grid_spec = pltpu.PrefetchScalarGridSpec(
        num_scalar_prefetch=1, grid=(sb,),
        in_specs=[pl.BlockSpec(memory_space=pl.ANY),
                  pl.BlockSpec((NSA_GROUPS, NSA_REP, 1, s, NSA_DH), lambda i, pt: (0, 0, i, 0, 0)),
                  pl.BlockSpec(w.shape, lambda i, pt: (0, 0, 0, 0)),
                  pl.BlockSpec(pe.shape, lambda i, pt: (0, 0, 0, 0)),
                  pl.BlockSpec(ovl.shape, lambda i, pt: (0, 0))],
        out_specs=[pl.BlockSpec((1, s, NSA_HEADS * NSA_DH), lambda i, pt: (i, 0, 0)),
                   pl.BlockSpec((1, s, NSA_GROUPS * n_sel), lambda i, pt: (i, 0, 0))],
        scratch_shapes=[pltpu.VMEM((2, NSA_KV_ROW, past), F32), pltpu.SemaphoreType.DMA((2,)),
                        pltpu.VMEM((2, past, NSA_GROUPS * NSA_DH), F32),
                        pltpu.VMEM((2, 8, NSA_GROUPS * NSA_DH), F32)])
    return pl.pallas_call(
        kern, grid_spec=grid_spec,
        out_shape=[_sds((sb, s, NSA_HEADS * NSA_DH)), _sds((sb, s, NSA_GROUPS * n_sel), BF16)],
        compiler_params=_params(("arbitrary",), 56), name="nsa_cmp_sample")(
            page_table.reshape(-1), pool4, q5, w, pe, ovl)


def _nsa_sw_sample_kernel(pt_ref, pool_ref, q_ref, knew_ref, wnew_ref, win_ref, sel_ref, ocmp_ref,
                          gate_ref, o_ref, buf_ref, sem_ref, *, layer, n_pages, s, chunk):
    slot = _paged_fetch(pool_ref, layer, pt_ref, n_pages, buf_ref, sem_ref)
    past = n_pages * PAGE_ROWS
    n_sel = sel_ref.shape[2] // NSA_GROUPS
    m = NSA_HEADS * s
    vrow = NSA_KV_ROW // 2
    qm = _group_query_rows(q_ref, s)
    sq = lax.broadcasted_iota(I32, (s, s), 0)
    sk = lax.broadcasted_iota(I32, (s, s), 1)
    new_mask = _group_rows([sk <= sq] * NSA_GROUPS)

    state = _online_init(m, vrow)
    for c0 in range(0, past, chunk):
        kv = buf_ref[slot, :, c0:c0 + chunk].astype(BF16)
        expand = (lax.broadcasted_iota(I32, (n_sel, chunk), 0)
                  == (c0 + lax.broadcasted_iota(I32, (n_sel, chunk), 1)) // SEL_BLOCK)
        expand = jnp.where(expand, 1.0, 0.0).astype(BF16)
        chosen = [_dot(sel_ref[0, :, g * n_sel:(g + 1) * n_sel], expand) > 0.5
                  for g in range(NSA_GROUPS)]
        state = _online(state, _dot(qm, kv), _group_rows(chosen), kv[vrow:], v_t=True)
    knew = knew_ref[0].astype(BF16)
    state = _online(state, _dot_t(qm, knew), new_mask, knew[:, vrow:])
    o_slc = state[2] / jnp.maximum(state[1], TINY)

    wb = win_ref.shape[2]
    win = win_ref[0].astype(BF16)
    qrow = lax.broadcasted_iota(I32, (s, wb), 0)
    dist = qrow + wb - lax.broadcasted_iota(I32, (s, wb), 1)
    old_mask = _group_rows([(dist >= 0) & (dist < WINDOW)] * NSA_GROUPS)
    state = _online(_online_init(m, vrow), _dot(qm, win), old_mask, win[vrow:], v_t=True)
    wnew = wnew_ref[0].astype(BF16)
    state = _online(state, _dot_t(qm, wnew), new_mask, wnew[:, vrow:])
    o_win = state[2] / jnp.maximum(state[1], TINY)

    gates = gate_ref[0]
    ocmp = ocmp_ref[0]
    for g in range(NSA_GROUPS):
        vcols = slice(g * NSA_DH, (g + 1) * NSA_DH)
        for r in range(NSA_REP):
            hd = g * NSA_REP + r
            rows = slice(hd * s, (hd + 1) * s)
            cols = slice(hd * NSA_DH, (hd + 1) * NSA_DH)
            o_ref[0, :, cols] = (gates[:, 3 * hd:3 * hd + 1] * ocmp[:, cols]
                                 + gates[:, 3 * hd + 1:3 * hd + 2] * o_slc[rows, vcols]
                                 + gates[:, 3 * hd + 2:3 * hd + 3] * o_win[rows, vcols])


def _nsa_sw_sample(q, pool, layer, page_table, kvs_new, kvw_new, win_buf, sel, o_cmp, gates, sb, s):
    n_pages = page_table.shape[1]
    past = n_pages * PAGE_ROWS
    wb = win_buf.shape[1]
    assert wb == WINDOW and s <= SEL_BLOCK and past % SEL_BLOCK == 0
    pool4 = _channel_major(pool)
    q5 = q.reshape(NSA_GROUPS, NSA_REP, sb, s, NSA_DH)
    per_b = lambda a: a.reshape(sb, s, a.shape[-1])
    blk = lambda a: pl.BlockSpec((1,) + a.shape[1:], lambda i, pt: (i, 0, 0))
    win_t = _channel_major(win_buf[None])[0]
    rows = [per_b(kvs_new), per_b(kvw_new), win_t, sel, o_cmp, per_b(gates)]
    kern = functools.partial(_nsa_sw_sample_kernel, layer=layer, n_pages=n_pages, s=s,
                             chunk=min(2048, past))
    grid_spec = pltpu.PrefetchScalarGridSpec(
        num_scalar_prefetch=1, grid=(sb,),
        in_specs=[pl.BlockSpec(memory_space=pl.ANY),
                  pl.BlockSpec((NSA_GROUPS, NSA_REP, 1, s, NSA_DH), lambda i, pt: (0, 0, i, 0, 0))]
        + [blk(a) for a in rows],
        out_specs=pl.BlockSpec((1, s, NSA_HEADS * NSA_DH), lambda i, pt: (i, 0, 0)),
        scratch_shapes=[pltpu.VMEM((2, NSA_KV_ROW, past), F32), pltpu.SemaphoreType.DMA((2,))])
    return pl.pallas_call(
        kern, grid_spec=grid_spec, out_shape=_sds((sb, s, NSA_HEADS * NSA_DH)),
        compiler_params=_params(("arbitrary",), 56), name="nsa_slc_win_sample")(
            page_table.reshape(-1), pool4, q5, *rows)


def _conv_sample_kernel(u_ref, gb_ref, cb_ref, w_ref, o_ref):
    u = u_ref[...]
    cb = cb_ref[...]
    row = lax.broadcasted_iota(I32, u.shape, 1)
    u1 = jnp.where(row == 0, cb[:, 1:2], pltpu.roll(u, 1, 1))
    u2 = jnp.where(row == 0, cb[:, 0:1], jnp.where(row == 1, cb[:, 1:2], pltpu.roll(u, 2, 1)))
    w = w_ref[...]
    o_ref[...] = gb_ref[...] * (w[0:1][None] * u2 + w[1:2][None] * u1 + w[2:3][None] * u)


def _conv_sample(u, gb, conv_buf, conv_w, sb, s):
    assert s >= CONV_W - 1
    c = u.shape[1]
    full = lambda shape: pl.BlockSpec(shape, lambda i, n=len(shape): (0,) * n)
    return pl.pallas_call(
        _conv_sample_kernel, grid=(1,),
        in_specs=[full((sb, s, c)), full((sb, s, c)), full(conv_buf.shape), full(conv_w.shape)],
        out_specs=full((sb, s, c)), out_shape=_sds((sb, s, c)),
        compiler_params=_params(("arbitrary",), 32), name="conv_sample")(
            u.reshape(sb, s, c), gb.reshape(sb, s, c), conv_buf, conv_w).reshape(sb * s, c)


def _even_layer_sample(x, sb, s, layer, pool_cmp, pool_slc, win_buf, conv_buf, page_table,
                       g_mix, w_in, w_out, cmp_w, cmp_pe, conv_w):
    q, kvc, kvs, kvw, _, gates, gb, u = _even_proj(x, g_mix, w_in)
    cw, cpe = _compress_pair_weights(cmp_w, cmp_pe)
    o_cmp, sel = _nsa_cmp_sample(q, pool_cmp, layer, page_table, cw, cpe, sb, s)
    o_nsa = _nsa_sw_sample(q, pool_slc, layer, page_table, kvs, kvw, win_buf, sel, o_cmp, gates,
                           sb, s)
    o_conv = _conv_sample(u, gb, conv_buf, conv_w, sb, s)
    qc = NSA_HEADS * NSA_DH
    y = _matmul_res([o_nsa.reshape(sb * s, qc), o_conv], [w_out[:qc], w_out[qc:]], x,
                    "even_out_proj")
    five = lambda a: a.reshape(sb, s, 2, NSA_GROUPS, NSA_DH)
    win_all = jnp.concatenate([win_buf, five(kvw)], axis=1)[:, s:]
    conv_all = jnp.concatenate([conv_buf, u.reshape(sb, s, CONV_CH)], axis=1)[:, s:]
    return y, (five(kvc), five(kvs), win_all, conv_all)


def _rot_half_cols(w):
    half = w.shape[-1] // 2
    return jnp.concatenate([-w[..., half:], w[..., :half]], axis=-1)


def _rope_tables(pos):
    half = ROPE // 2
    inv = ROPE_THETA ** (-jnp.arange(half, dtype=F32) / half)
    ang = pos.astype(F32)[:, None] * inv[None, :]
    cos, sin = jnp.cos(ang), jnp.sin(ang)
    cos32 = jnp.concatenate([cos, cos], axis=1)
    sin32 = jnp.concatenate([sin, sin], axis=1)
    n = pos.shape[0]
    pad = MLA_QK - NOPE - ROPE
    cosq = jnp.concatenate([jnp.ones((n, NOPE), F32), cos32, jnp.zeros((n, pad), F32)], axis=1)
    sinq = jnp.concatenate([jnp.zeros((n, NOPE), F32), sin32, jnp.zeros((n, pad), F32)], axis=1)
    return cos32, sin32, cosq, sinq


def _mla_weights(w_in, w_uq, w_uk):
    wcq = w_in[:, :Q_LORA]
    wckv = w_in[:, Q_LORA:Q_LORA + KV_LORA]
    wkpe = w_in[:, Q_LORA + KV_LORA:]
    uq = w_uq.reshape(Q_LORA, MLA_HEADS, NOPE + ROPE)
    pad = jnp.zeros((Q_LORA, MLA_HEADS, MLA_QK - NOPE - ROPE), F32)
    wq = jnp.concatenate([uq, pad], axis=-1).reshape(Q_LORA, MLA_HEADS * MLA_QK)
    wq_rot = jnp.concatenate([jnp.zeros((Q_LORA, MLA_HEADS, NOPE), F32),
                              _rot_half_cols(uq[..., NOPE:]), pad], axis=-1).reshape(wq.shape)
    kpad = jnp.zeros((KV_LORA, MLA_HEADS, MLA_QK - NOPE), F32)
    wk_lat = jnp.concatenate([w_uk, kpad], axis=-1).reshape(KV_LORA, MLA_HEADS * MLA_QK)
    eye = jnp.broadcast_to(jnp.eye(ROPE, dtype=F32)[:, None, :], (ROPE, MLA_HEADS, ROPE))
    wk_pe = jnp.concatenate([jnp.zeros((ROPE, MLA_HEADS, NOPE), F32), eye,
                             jnp.zeros((ROPE, MLA_HEADS, MLA_QK - NOPE - ROPE), F32)],
                            axis=-1).reshape(ROPE, MLA_HEADS * MLA_QK)
    bf = lambda a: a.astype(BF16)
    return dict(wcq=bf(wcq), wckv=bf(wckv), wkpe=bf(wkpe), wkpe_rot=bf(_rot_half_cols(wkpe)),
                wq=bf(wq), wq_rot=bf(wq_rot), wk_lat=bf(wk_lat), wk_pe=bf(wk_pe))


def _mla_latent(x_ref, cos_ref, sin_ref, g_ref, qn_ref, kvn_ref, wcq_ref, wckv_ref, wkpe_ref,
                wkrot_ref, lat_ref):
    h = _rms(x_ref[...], g_ref[...]).astype(BF16)
    cq = _rms(_dot(h, wcq_ref[...]), qn_ref[...]).astype(BF16)
    ckv = _rms(_dot(h, wckv_ref[...]), kvn_ref[...])
    kpe = _dot(h, wkpe_ref[...]) * cos_ref[...] + _dot(h, wkrot_ref[...]) * sin_ref[...]
    lat_ref[:, :KV_LORA] = ckv
    lat_ref[:, KV_LORA:] = kpe
    return cq, ckv.astype(BF16), kpe.astype(BF16)


def _mla_queries(cq, wq_ref, wqrot_ref, cosq, sinq, head):
    cols = slice(head * MLA_QK, (head + 1) * MLA_QK)
    q = _dot(cq, wq_ref[:, cols]) * cosq + _dot(cq, wqrot_ref[:, cols]) * sinq
    return q * ((NOPE + ROPE) ** -0.5 * LOG2E)


def _mla_proj_prompt_body(rows, consts, outs):
    x_ref, cos_ref, sin_ref, cosq_ref, sinq_ref = rows
    (g_ref, qn_ref, kvn_ref, wcq_ref, wckv_ref, wkpe_ref, wkrot_ref, wq_ref, wqrot_ref,
     wklat_ref, wkpe2_ref, wuv_ref) = consts
    lat_ref, q_ref, k_ref, v_ref = outs
    cq, ckv, kpe = _mla_latent(x_ref, cos_ref, sin_ref, g_ref, qn_ref, kvn_ref, wcq_ref,
                               wckv_ref, wkpe_ref, wkrot_ref, lat_ref)
    cosq, sinq = cosq_ref[...], sinq_ref[...]
    for hd in range(MLA_HEADS):
        cols = slice(hd * MLA_QK, (hd + 1) * MLA_QK)
        q_ref[:, cols] = _mla_queries(cq, wq_ref, wqrot_ref, cosq, sinq, hd).astype(BF16)
    k_ref[...] = (_dot(ckv, wklat_ref[...]) + _dot(kpe, wkpe2_ref[...])).astype(BF16)
    v_ref[...] = _dot(ckv, wuv_ref[...]).astype(BF16)


def _mla_proj_sample_body(rows, consts, outs):
    x_ref, cos_ref, sin_ref, cosq_ref, sinq_ref = rows
    (g_ref, qn_ref, kvn_ref, wcq_ref, wckv_ref, wkpe_ref, wkrot_ref, wq_ref, wqrot_ref,
     wukt_ref) = consts
    lat_ref, q_ref = outs
    cq, _, _ = _mla_latent(x_ref, cos_ref, sin_ref, g_ref, qn_ref, kvn_ref, wcq_ref, wckv_ref,
                           wkpe_ref, wkrot_ref, lat_ref)
    cosq, sinq = cosq_ref[...], sinq_ref[...]
    for hd in range(MLA_HEADS):
        q = _mla_queries(cq, wq_ref, wqrot_ref, cosq, sinq, hd)
        q_ref[hd, :, :KV_LORA] = _dot(q[:, :NOPE].astype(BF16), wukt_ref[hd])
        q_ref[hd, :, KV_LORA:] = q[:, NOPE:NOPE + ROPE]


def _mla_proj(x, pos, g_mix, q_norm, kv_norm, mw, extra, body, outs, name):
    r, d = x.shape
    tm = _row_tile(pos.shape[0], 512)
    cos32, sin32, cosq, sinq = _rope_tables(pos)
    n_per = pos.shape[0] // tm
    periods = {k: n_per for k in (1, 2, 3, 4)}
    consts = [g_mix.reshape(1, d), q_norm.reshape(1, -1), kv_norm.reshape(1, -1), mw["wcq"],
              mw["wckv"], mw["wkpe"], mw["wkpe_rot"], mw["wq"], mw["wq_rot"]] + extra
    return _rowwise(body, [x, cos32, sin32, cosq, sinq], consts, outs, tm, name, periods=periods)


def _mla_flash_t_kernel(qi_ref, ki_ref, q_ref, k_ref, v_ref, o_ref, m_ref, acc_ref, *, tq, tk):
    n = pl.program_id(2)
    qi = qi_ref[n]
    ki = ki_ref[n]
    last = (qi * tq + tq - 1) // tk

    @pl.when(ki == 0)
    def _():
        m_ref[...] = jnp.full(m_ref.shape, NEG, F32)
        acc_ref[...] = jnp.zeros(acc_ref.shape, F32)

    def step(masked):
        mask = None
        if masked:
            mask = (ki * tk + lax.broadcasted_iota(I32, (tk, tq), 0)
                    <= qi * tq + lax.broadcasted_iota(I32, (tk, tq), 1))
        for j in range(2):
            qk = slice(j * MLA_QK, (j + 1) * MLA_QK)
            _flash_update_t(q_ref[:, qk], k_ref[:, qk], v_ref[:, j * V_DIM:(j + 1) * V_DIM], mask,
                            m_ref.at[j], acc_ref.at[j])

    needs_mask = ki * tk + tk - 1 > qi * tq

    @pl.when(needs_mask)
    def _():
        step(True)

    @pl.when(jnp.logical_not(needs_mask))
    def _():
        step(False)

    @pl.when(ki == last)
    def _():
        for j in range(2):
            o_ref[:, j * V_DIM:(j + 1) * V_DIM] = _flash_result_t(acc_ref[j]).T


def _mla_flash_t(q, k, v, b, t, tq=1024, tk=1024):
    tq, tk = min(tq, t), min(tk, t)
    nq, nk = t // tq, t // tk
    qi, ki = _causal_steps(nq, tq, tk)
    pairs = MLA_HEADS // 2
    grid_spec = pltpu.PrefetchScalarGridSpec(
        num_scalar_prefetch=2, grid=(b, pairs, qi.shape[0]),
        in_specs=[pl.BlockSpec((tq, 2 * MLA_QK), lambda i, h, n, qt, kt: (i * nq + qt[n], h)),
                  pl.BlockSpec((tk, 2 * MLA_QK), lambda i, h, n, qt, kt: (i * nk + kt[n], h)),
                  pl.BlockSpec((tk, 2 * V_DIM), lambda i, h, n, qt, kt: (i * nk + kt[n], h))],
        out_specs=pl.BlockSpec((tq, 2 * V_DIM), lambda i, h, n, qt, kt: (i * nq + qt[n], h)),
        scratch_shapes=[pltpu.VMEM((2, 1, tq), F32), pltpu.VMEM((2, 2 * V_DIM, tq), F32)])
    return pl.pallas_call(
        functools.partial(_mla_flash_t_kernel, tq=tq, tk=tk), grid_spec=grid_spec,
        out_shape=_sds((b * t, MLA_HEADS * V_DIM)),
        compiler_params=_params(("arbitrary", "arbitrary", "arbitrary"), 48),
        name="mla_flash_prompt")(qi, ki, q, k, v)


def _odd_layer_prompt(x, b, t, g_mix, w_in, q_norm, w_uq, kv_norm, w_uk, w_uv, w_out):
    mw = _mla_weights(w_in, w_uq, w_uk)
    r = x.shape[0]
    outs = [_sds((r, LAT)), _sds((r, MLA_HEADS * MLA_QK), BF16), _sds((r, MLA_HEADS * MLA_QK), BF16),
            _sds((r, MLA_HEADS * V_DIM), BF16)]
    extra = [mw["wk_lat"], mw["wk_pe"], w_uv.reshape(KV_LORA, MLA_HEADS * V_DIM).astype(BF16)]
    lat, q, k, v = _mla_proj(x, jnp.arange(t, dtype=I32), g_mix, q_norm, kv_norm, mw, extra,
                             _mla_proj_prompt_body, outs, "mla_proj_prompt")
    o = _mla_flash_t(q, k, v, b, t)
    y = _matmul_res([o], [w_out], x, "mla_out_proj")
    return y, lat.reshape(b, t, LAT)


def _mla_sample_kernel(pt_ref, pool_ref, q_ref, new_ref, o_ref, buf_ref, sem_ref, *,
                       layer, n_pages, s, chunk):
    slot = _paged_fetch(pool_ref, layer, pt_ref, n_pages, buf_ref, sem_ref)
    past = n_pages * PAGE_ROWS
    m = MLA_HEADS * s
    q = q_ref[:, 0].reshape(m, LAT).astype(BF16)
    state = _online_init(m, KV_LORA)
    for c0 in range(0, past, chunk):
        kv = buf_ref[slot, :, c0:c0 + chunk].astype(BF16)
        state = _online(state, _dot(q, kv), None, kv[:KV_LORA], v_t=True)
    new = new_ref[0].astype(BF16)
    causal = (lax.broadcasted_iota(I32, (s, s), 1) <= lax.broadcasted_iota(I32, (s, s), 0))
    state = _online(state, _dot_t(q, new), jnp.tile(causal, (MLA_HEADS, 1)), new[:, :KV_LORA])
    o = state[2] / jnp.maximum(state[1], TINY)
    o_ref[:, 0] = o.reshape(MLA_HEADS, s, KV_LORA)


def _mla_sample_attend(q, pool, layer, page_table, lat_new, sb, s):
    n_pages = page_table.shape[1]
    past = n_pages * PAGE_ROWS
    q4 = q.reshape(MLA_HEADS, sb, s, LAT)
    kern = functools.partial(_mla_sample_kernel, layer=layer, n_pages=n_pages, s=s,
                             chunk=min(2048, past))
    grid_spec = pltpu.PrefetchScalarGridSpec(
        num_scalar_prefetch=1, grid=(sb,),
        in_specs=[pl.BlockSpec(memory_space=pl.ANY),
                  pl.BlockSpec((MLA_HEADS, 1, s, LAT), lambda i, pt: (0, i, 0, 0)),
                  pl.BlockSpec((1, s, LAT), lambda i, pt: (i, 0, 0))],
        out_specs=pl.BlockSpec((MLA_HEADS, 1, s, KV_LORA), lambda i, pt: (0, i, 0, 0)),
        scratch_shapes=[pltpu.VMEM((2, LAT, past), F32), pltpu.SemaphoreType.DMA((2,))])
    o = pl.pallas_call(
        kern, grid_spec=grid_spec, out_shape=_sds((MLA_HEADS, sb, s, KV_LORA)),
        compiler_params=_params(("arbitrary",), 56), name="mla_paged_sample")(
            page_table.reshape(-1), _channel_major(pool), q4, lat_new.reshape(sb, s, LAT))
    return o.reshape(MLA_HEADS, sb * s, KV_LORA)


def _mla_out_sample_body(rows, consts, outs):
    o_ref, res_ref = rows
    wuv_ref, wout_ref = consts
    acc = res_ref[...]
    for hd in range(MLA_HEADS):
        o_h = _dot(o_ref[hd].astype(BF16), wuv_ref[hd])
        acc = acc + _dot(o_h.astype(BF16), wout_ref[hd])
    outs[0][...] = acc


def _odd_layer_sample(x, sb, s, layer, pool, page_table, g_mix, w_in, q_norm, w_uq, kv_norm, w_uk,
                      w_uv, w_out):
    mw = _mla_weights(w_in, w_uq, w_uk)
    r = x.shape[0]
    past = page_table.shape[1] * PAGE_ROWS
    pos = past + jnp.arange(_row_tile(r, 512), dtype=I32) % s
    outs = [_sds((r, LAT)), _sds((MLA_HEADS, r, LAT))]
    wukt = w_uk.transpose(1, 2, 0).astype(BF16)
    lat, q = _mla_proj(x, pos, g_mix, q_norm, kv_norm, mw, [wukt],
                       _mla_proj_sample_body, outs, "mla_proj_sample")
    o_lat = _mla_sample_attend(q, pool, layer, page_table, lat, sb, s)
    wuv = w_uv.transpose(1, 0, 2).astype(BF16)
    wout = w_out.reshape(MLA_HEADS, V_DIM, -1).astype(BF16)
    y = _rowwise(_mla_out_sample_body, [o_lat, x], [wuv, wout], [_sds(x.shape)],
                 _row_tile(r, 512), "mla_out_sample")[0]
    return y, lat.reshape(sb, s, LAT)


def _plain_matmul_body(rows, consts, outs):
    outs[0][...] = _dot(rows[0][...].astype(BF16), consts[0][...])


def _mem_q_body(rows, consts, outs):
    g_ref, w_ref = consts
    h = _rms(rows[0][...], g_ref[...]).astype(BF16)
    dh = w_ref.shape[1] // MEM_HEADS
    outs[0][...] = (_dot(h, w_ref[...]) * (dh ** -0.5 * LOG2E)).astype(BF16)


def _softmax_rows(s):
    p = jnp.exp2(s - jnp.max(s, axis=-1, keepdims=True))
    return p / jnp.sum(p, axis=-1, keepdims=True)


def _mem_attn_kernel(q_ref, kv_ref, o_ref):
    d = q_ref.shape[2]
    dh = d // MEM_HEADS
    parts = dh // kv_ref.shape[5]
    for bi in range(q_ref.shape[0]):
        for hd in range(MEM_HEADS):
            rows = lambda kv: jnp.concatenate(
                [kv_ref[0, bi, :, kv, part * MEM_HEADS + hd, :] for part in range(parts)],
                axis=1).astype(BF16)
            p = _softmax_rows(_dot_t(q_ref[bi, :, hd * dh:(hd + 1) * dh], rows(0)))
            o_ref[bi, :, hd * dh:(hd + 1) * dh] = _dot(p.astype(BF16), rows(1)).astype(BF16)


def _mem_layer_cached(y, nb, cache, layer, g_mem, w_q, w_o):
    r, d = y.shape
    rows = r // nb
    q = _rowwise(_mem_q_body, [y], [g_mem.reshape(1, d), w_q.astype(BF16)], [_sds((r, d), BF16)],
                 _row_tile(r, 512), "mem_q_proj")[0]
    nl, _, mem, two, heads, dh = cache.shape
    parts = dh // V7X_LANES
    kv = cache.reshape(nl, nb, mem, two, heads, parts, V7X_LANES).transpose(0, 1, 2, 3, 5, 4, 6)
    kv = kv.reshape(nl, nb, mem, two, parts * heads, V7X_LANES)
    nbk = 2 if nb % 2 == 0 else 1
    o = pl.pallas_call(
        _mem_attn_kernel, grid=(nb // nbk,),
        in_specs=[pl.BlockSpec((nbk, rows, d), lambda i: (i, 0, 0)),
                  pl.BlockSpec((1, nbk) + kv.shape[2:], lambda i: (layer, i, 0, 0, 0, 0))],
        out_specs=pl.BlockSpec((nbk, rows, d), lambda i: (i, 0, 0)),
        out_shape=_sds((nb, rows, d), BF16),
        compiler_params=_params(("arbitrary",), 48), name="mem_attention")(
            q.reshape(nb, rows, d), kv)
    return _matmul_res([o.reshape(r, d)], [w_o], y, "mem_out_proj")


def _mem_fused_kernel(y_ref, kv_ref, g_ref, wq_ref, wo_ref, o_ref):
    d = y_ref.shape[1]
    dh = d // MEM_HEADS
    y = y_ref[...]
    h = _rms(y, g_ref[...]).astype(BF16)
    q = (_dot(h, wq_ref[...]) * (dh ** -0.5 * LOG2E)).astype(BF16)
    acc = y
    for hd in range(MEM_HEADS):
        cols = slice(hd * dh, (hd + 1) * dh)
        k = kv_ref[0, :, cols].astype(BF16)
        v = kv_ref[0, :, d + hd * dh:d + (hd + 1) * dh].astype(BF16)
        p = _softmax_rows(_dot_t(q[:, cols], k))
        o = _dot(p.astype(BF16), v).astype(BF16)
        acc = acc + _dot(o, wo_ref[cols, :])
    o_ref[...] = acc


def _mem_layer_fused(y, nb, kv, g_mem, w_q, w_o):
    r, d = y.shape
    rows = r // nb
    tq = _row_tile(rows, 512)
    nt = rows // tq
    full = lambda a: pl.BlockSpec(a.shape, lambda i, j, n=a.ndim: (0,) * n)
    consts = [g_mem.reshape(1, d), w_q.astype(BF16), w_o.astype(BF16)]
    return pl.pallas_call(
        _mem_fused_kernel, grid=(nb, nt),
        in_specs=[pl.BlockSpec((tq, d), lambda i, j: (i * nt + j, 0)),
                  pl.BlockSpec((1,) + kv.shape[1:], lambda i, j: (i, 0, 0))]
        + [full(c) for c in consts],
        out_specs=pl.BlockSpec((tq, d), lambda i, j: (i * nt + j, 0)),
        out_shape=_sds((r, d)),
        compiler_params=_params(("arbitrary", "arbitrary"), 48), name="mem_layer_prompt")(
            y, kv, *consts)


def _mem_project(mem, w_kv):
    nb, m, d = mem.shape
    kv = _rowwise(_plain_matmul_body, [mem.reshape(nb * m, d)], [w_kv.astype(BF16)],
                  [_sds((nb * m, w_kv.shape[1]))], _row_tile(nb * m, 512), "mem_kv_proj")[0]
    return kv.reshape(nb, m, w_kv.shape[1])


def _route(x, wr_ref, br_ref):
    xh, xm, _ = _split3(x)
    w0, w1 = wr_ref[0], wr_ref[1]
    logit = _dot(xh, w0) + (_dot(xh, w1) + _dot(xm, w0)) + br_ref[...]
    lane = lax.broadcasted_iota(I32, logit.shape, 1).astype(F32)
    big = float(ROUTE_LANES)
    is_g = lane < N_GROUPS
    gl = jnp.where(is_g, logit, -jnp.inf)
    gmax = jnp.max(gl, axis=-1, keepdims=True)
    gidx = jnp.min(jnp.where(gl == gmax, lane, big), axis=-1, keepdims=True)
    g_gate = 1.0 / jnp.sum(jnp.where(is_g, jnp.exp(gl - gmax), 0.0), axis=-1, keepdims=True)
    lo = N_GROUPS + gidx * EXP_PER_GROUP
    in_grp = (lane >= lo) & (lane < lo + EXP_PER_GROUP)
    el = jnp.where(in_grp, logit, -jnp.inf)
    ee = jnp.where(in_grp, jnp.exp(el - jnp.max(el, axis=-1, keepdims=True)), 0.0)
    prob = ee / jnp.sum(ee, axis=-1, keepdims=True)
    cand = jnp.where(in_grp, prob, -1.0)
    p1 = jnp.max(cand, axis=-1, keepdims=True)
    i1 = jnp.min(jnp.where(cand == p1, lane, big), axis=-1, keepdims=True)
    cand = jnp.where(lane == i1, -1.0, cand)
    p2 = jnp.max(cand, axis=-1, keepdims=True)
    i2 = jnp.min(jnp.where(cand == p2, lane, big), axis=-1, keepdims=True)
    tot = p1 + p2
    return (jnp.where(lane == i1, p1 / tot * g_gate, 0.0)
            + jnp.where(lane == i2, p2 / tot * g_gate, 0.0))


def _moe_kernel(y_ref, g_ref, wr_ref, br_ref, w13_ref, w2_ref, gf_ref, o_ref, h_ref, comb_ref,
                acc_ref, *, final_norm):
    grp = pl.program_id(1)

    @pl.when(grp == 0)
    def _():
        x = _rms(y_ref[...], g_ref[...])
        h_ref[...] = x.astype(BF16)
        comb_ref[...] = _route(x, wr_ref, br_ref)
        acc_ref[...] = jnp.zeros(acc_ref.shape, F32)

    hc = _dot(h_ref[...], w13_ref[0])
    half = hc.shape[1] // 2
    ff = half // EXP_PER_GROUP
    comb = comb_ref[...]
    lane = lax.broadcasted_iota(I32, comb.shape, 1)
    acts = []
    for j in range(EXP_PER_GROUP):
        e_lane = N_GROUPS + grp * EXP_PER_GROUP + j
        ce = jnp.sum(jnp.where(lane == e_lane, comb, 0.0), axis=-1, keepdims=True)
        a = hc[:, j * ff:(j + 1) * ff]
        bb = hc[:, half + j * ff:half + (j + 1) * ff]
        acts.append((a * (1.0 / (1.0 + jnp.exp(-a))) * bb * ce).astype(BF16))
    acc_ref[...] += _dot(jnp.concatenate(acts, axis=1), w2_ref[0])

    @pl.when(grp == pl.num_programs(1) - 1)
    def _():
        out = y_ref[...] + acc_ref[...]
        if final_norm:
            out = _rms(out, gf_ref[...])
        o_ref[...] = out


def _moe_layer(y, g_ffn, w_group, b_group, w_expert, b_expert, w13, w2, g_final=None):
    r, d = y.shape
    tm = _row_tile(r, 1024)
    n_route = N_GROUPS + N_EXPERTS
    wr = jnp.concatenate([w_group, w_expert, jnp.zeros((d, ROUTE_LANES - n_route), F32)], axis=1)
    wr = jnp.stack(_split3(wr)[:2])
    br = jnp.concatenate([b_group, b_expert, jnp.zeros((ROUTE_LANES - n_route,), F32)]).reshape(1, -1)
    gf = (g_ffn if g_final is None else g_final).reshape(1, d)
    ff = w13.shape[2] // 2
    per = EXP_PER_GROUP
    w13g = w13.reshape(N_GROUPS, per, d, 2, ff).transpose(0, 2, 3, 1, 4).reshape(
        N_GROUPS, d, 2 * per * ff).astype(BF16)
    w2g = w2.reshape(N_GROUPS, per * ff, d).astype(BF16)
    return pl.pallas_call(
        functools.partial(_moe_kernel, final_norm=g_final is not None), grid=(r // tm, N_GROUPS),
        in_specs=[pl.BlockSpec((tm, d), lambda i, e: (i, 0)),
                  pl.BlockSpec((1, d), lambda i, e: (0, 0)),
                  pl.BlockSpec(wr.shape, lambda i, e: (0, 0, 0)),
                  pl.BlockSpec(br.shape, lambda i, e: (0, 0)),
                  pl.BlockSpec((1, d, 2 * per * ff), lambda i, e: (e, 0, 0)),
                  pl.BlockSpec((1, per * ff, d), lambda i, e: (e, 0, 0)),
                  pl.BlockSpec((1, d), lambda i, e: (0, 0))],
        out_specs=pl.BlockSpec((tm, d), lambda i, e: (i, 0)),
        out_shape=_sds((r, d)),
        scratch_shapes=[pltpu.VMEM((tm, d), BF16), pltpu.VMEM((tm, ROUTE_LANES), F32),
                        pltpu.VMEM((tm, d), F32)],
        compiler_params=_params(("arbitrary", "arbitrary"), 56), name="hier_moe")(
            y, g_ffn.reshape(1, d), wr, br, w13g, w2g, gf)


def kernel(x_prompt, x_sample, mem_prompt, cache_nsa_cmp_kv, cache_nsa_slc_kv, state_nsa_win_kv, state_conv, cache_mla_latent, cache_mem_kv, page_table, norm_mix, norm_mem, norm_ffn, norm_final, e_w_in, e_w_out, e_cmp_w, e_cmp_pe, e_conv_w, o_w_in, o_q_norm, o_w_uq, o_kv_norm, o_w_uk, o_w_uv, o_w_out, mem_w_q, mem_w_kv, mem_w_o, moe_w_group, moe_b_group, moe_w_expert, moe_b_expert, moe_w13, moe_w2):
    b, t, d = x_prompt.shape
    sb, s, _ = x_sample.shape
    depth = norm_mix.shape[0]
    yp = x_prompt.reshape(b * t, d)
    ys = x_sample.reshape(sb * s, d)
    p_even, s_even, p_mla, s_mla, p_mem = [], [], [], [], []
    for l in range(depth):
        if l % 2 == 0:
            e = l // 2
            w = (norm_mix[l], e_w_in[e], e_w_out[e], e_cmp_w[e], e_cmp_pe[e], e_conv_w[e])
            yp, cp = _even_layer_prompt(yp, b, t, *w)
            ys, cs = _even_layer_sample(ys, sb, s, e, cache_nsa_cmp_kv, cache_nsa_slc_kv,
                                        state_nsa_win_kv[e], state_conv[e], page_table, *w)
            p_even.append(cp)
            s_even.append(cs)
        else:
            o = l // 2
            w = (norm_mix[l], o_w_in[o], o_q_norm[o], o_w_uq[o], o_kv_norm[o], o_w_uk[o], o_w_uv[o],
                 o_w_out[o])
            yp, lp = _odd_layer_prompt(yp, b, t, *w)
            ys, ls = _odd_layer_sample(ys, sb, s, o, cache_mla_latent, page_table, *w)
            p_mla.append(lp)
            s_mla.append(ls)
        kv_p = _mem_project(mem_prompt, mem_w_kv[l])
        p_mem.append(kv_p.reshape(b, kv_p.shape[1], 2, MEM_HEADS, d // MEM_HEADS))
        yp = _mem_layer_fused(yp, b, kv_p, norm_mem[l], mem_w_q[l], mem_w_o[l])
        ys = _mem_layer_cached(ys, sb, cache_mem_kv, l, norm_mem[l], mem_w_q[l], mem_w_o[l])
        g_final = norm_final if l == depth - 1 else None
        moe_w = (norm_ffn[l], moe_w_group[l], moe_b_group[l], moe_w_expert[l], moe_b_expert[l],
                 moe_w13[l], moe_w2[l])
        yp = _moe_layer(yp, *moe_w, g_final=g_final)
        ys = _moe_layer(ys, *moe_w, g_final=g_final)
    stack = lambda caches, k: jnp.stack([c[k] for c in caches])
    return (yp.reshape(b, t, d), ys.reshape(sb, s, d),
            stack(p_even, 0), stack(p_even, 1), stack(p_even, 2), stack(p_even, 3),
            jnp.stack(p_mla), jnp.stack(p_mem),
            stack(s_even, 0), stack(s_even, 1), stack(s_even, 2), stack(s_even, 3),
            jnp.stack(s_mla))
```

```python
import functools
import math

import jax
import jax.numpy as jnp
from jax import lax
from jax.experimental import pallas as pl
from jax.experimental.pallas import tpu as pltpu

F32 = jnp.float32
BF16 = jnp.bfloat16
I32 = jnp.int32

EPS = 1e-6
LOG2E = math.log2(math.e)
NEG = -1e30
TINY = 1e-30
FORCE_BONUS = 1e4

NSA_HEADS = 8
NSA_GROUPS = 2
NSA_REP = NSA_HEADS // NSA_GROUPS
NSA_DH = 64
NSA_KV_ROW = 2 * NSA_GROUPS * NSA_DH
CMP_STRIDE = 16
CMP_BLOCK = 32
SEL_BLOCK = 64
SEL_TOPK = 16
WINDOW = 512
CONV_CH = 512
CONV_W = 3
MLA_HEADS = 16
Q_LORA = 384
KV_LORA = 256
NOPE = 64
ROPE = 32
V_DIM = 64
ROPE_THETA = 10000.0
LAT = KV_LORA + ROPE
MLA_QK = 128
MEM_HEADS = 4
N_GROUPS = 4
EXP_PER_GROUP = 4
N_EXPERTS = 16
EXPERT_FF = 256
ROUTE_LANES = 128

V7X_VMEM_BYTES = 64 * 2**20
V7X_LANES = 128


def _params(sem, vmem_mb):
    assert vmem_mb * 2**20 < V7X_VMEM_BYTES
    return pltpu.CompilerParams(dimension_semantics=sem, vmem_limit_bytes=vmem_mb * 2**20)


def _dot(a, b):
    return jnp.dot(a, b, preferred_element_type=F32)


def _dot_t(a, b):
    return lax.dot_general(a, b, (((1,), (1,)), ((), ())), preferred_element_type=F32)


def _tdot(a, b):
    return lax.dot_general(a, b, (((0,), (0,)), ((), ())), preferred_element_type=F32)


def _split3(x):
    hi = x.astype(BF16)
    r1 = x - hi.astype(F32)
    mid = r1.astype(BF16)
    lo = (r1 - mid.astype(F32)).astype(BF16)
    return hi, mid, lo


def _rms(x, g):
    return x * lax.rsqrt(jnp.mean(x * x, axis=-1, keepdims=True) + EPS) * g


def _row_spec(shape, tm, period=None):
    lead = len(shape) - 2
    blk = tuple(shape[:lead]) + (tm, shape[-1])
    if period is None:
        return pl.BlockSpec(blk, lambda i, lead=lead: (0,) * lead + (i, 0))
    return pl.BlockSpec(blk, lambda i, lead=lead, p=period: (0,) * lead + (i % p, 0))


def _const_spec(shape):
    return pl.BlockSpec(tuple(shape), lambda i, n=len(shape): (0,) * n)


def _seq_col_spec(shape, tm):
    lead = len(shape) - 3
    nt = shape[-1] // tm
    blk = tuple(shape[:lead]) + (1, shape[-2], tm)
    return pl.BlockSpec(blk, lambda i, lead=lead, nt=nt: (0,) * lead + (i // nt, 0, i % nt))


def _rowwise(body, rows, consts, outs, tm, name, vmem_mb=48, periods=None, seq_cols=()):
    n_rows = rows[0].shape[-2]
    assert n_rows % tm == 0
    periods = periods or {}
    in_specs = [_row_spec(a.shape, tm, periods.get(k)) for k, a in enumerate(rows)]
    in_specs += [_const_spec(c.shape) for c in consts]
    out_specs = [_seq_col_spec(o.shape, tm) if k in seq_cols else _row_spec(o.shape, tm)
                 for k, o in enumerate(outs)]
    nr, nc = len(rows), len(consts)

    def kern(*refs):
        body(refs[:nr], refs[nr:nr + nc], refs[nr + nc:])

    return pl.pallas_call(
        kern, grid=(n_rows // tm,), in_specs=in_specs, out_specs=out_specs, out_shape=outs,
        compiler_params=_params(("arbitrary",), vmem_mb), name=name)(*rows, *consts)


def _sds(shape, dtype=F32):
    return jax.ShapeDtypeStruct(tuple(shape), dtype)


def _row_tile(n_rows, want):
    tm = min(want, n_rows)
    assert n_rows % tm == 0
    return tm


def _even_proj_body(channel_major, rows, consts, outs):
    (x_ref,) = rows
    g_ref, wq_ref, wkv_ref, wg_ref, wc_ref = consts[:5]
    if channel_major:
        q_ref, kvc_ref, kvt_ref, kvb_ref, gate_ref, gb_ref, u_ref = outs
    else:
        q_ref, kvc_ref, kvs_ref, kvw_ref, kvb_ref, gate_ref, gb_ref, u_ref = outs
    h = _rms(x_ref[...], g_ref[...]).astype(BF16)
    q = _dot(h, wq_ref[...]) * (NSA_DH ** -0.5 * LOG2E)
    for hd in range(NSA_HEADS):
        q_ref[hd // NSA_REP, hd % NSA_REP] = q[:, hd * NSA_DH:(hd + 1) * NSA_DH].astype(BF16)
    kv = _dot(h, wkv_ref[...])
    kvc_ref[...] = kv[:, :NSA_KV_ROW]
    if channel_major:
        kvt = _dot_t(consts[5][...], h)
        for j in range(3):
            kvt_ref[j, 0] = kvt[j * NSA_KV_ROW:(j + 1) * NSA_KV_ROW]
    else:
        kvs_ref[...] = kv[:, NSA_KV_ROW:2 * NSA_KV_ROW]
        kvw_ref[...] = kv[:, 2 * NSA_KV_ROW:]
    for j in range(3 * 2 * NSA_GROUPS):
        kvb_ref[j] = kv[:, j * NSA_DH:(j + 1) * NSA_DH].astype(BF16)
    gate_ref[...] = 1.0 / (1.0 + jnp.exp(-_dot(h, wg_ref[...])))
    c = _dot(h, wc_ref[...])
    gb_ref[...] = c[:, :CONV_CH]
    u_ref[...] = c[:, CONV_CH:2 * CONV_CH] * c[:, 2 * CONV_CH:]


def _even_proj(x, g, w_in, seq_len=None):
    r, d = x.shape
    qc = NSA_HEADS * NSA_DH
    kc = 3 * NSA_KV_ROW
    gc = NSA_HEADS * 3
    wq = w_in[:, :qc].astype(BF16)
    wkv = w_in[:, qc:qc + kc].astype(BF16)
    wg = w_in[:, qc + kc:qc + kc + gc].astype(BF16)
    wc = w_in[:, qc + kc + gc:].astype(BF16)
    consts = [g.reshape(1, d), wq, wkv, wg, wc]
    tail = [_sds((12, r, NSA_DH), BF16), _sds((r, gc)), _sds((r, CONV_CH)), _sds((r, CONV_CH))]
    head = [_sds((NSA_GROUPS, NSA_REP, r, NSA_DH), BF16), _sds((r, NSA_KV_ROW))]
    if seq_len is None:
        tm = _row_tile(r, 512)
        outs = head + [_sds((r, NSA_KV_ROW)), _sds((r, NSA_KV_ROW))] + tail
        seq_cols = ()
    else:
        tm = _row_tile(seq_len, 512)
        outs = head + [_sds((3, r // seq_len, NSA_KV_ROW, seq_len))] + tail
        seq_cols = (2,)
        consts.append(wkv.T)
    return _rowwise(functools.partial(_even_proj_body, seq_len is not None), [x], consts, outs, tm,
                    "even_proj", seq_cols=seq_cols)


def _compress_weights(cmp_w, cmp_pe):
    eye_g = jnp.eye(NSA_GROUPS, dtype=F32)
    eye_k = jnp.eye(2, dtype=F32)

    def big(wh):
        return jnp.einsum("kjde,kl,gh->jkgdlhe", wh, eye_k, eye_g).reshape(
            CMP_STRIDE * NSA_KV_ROW, NSA_KV_ROW)

    w = jnp.concatenate([big(cmp_w[:, :CMP_STRIDE]), big(cmp_w[:, CMP_STRIDE:])], axis=1)

    def pe_row(peh):
        return jnp.broadcast_to(peh.transpose(1, 0, 2)[:, :, None, :],
                                (CMP_STRIDE, 2, NSA_GROUPS, NSA_DH)).reshape(1, -1)

    pe = jnp.concatenate([pe_row(cmp_pe[:, :CMP_STRIDE]), pe_row(cmp_pe[:, CMP_STRIDE:]),
                          jnp.zeros((6, CMP_STRIDE * NSA_KV_ROW), F32)], axis=0)
    return w.astype(BF16), pe


def _compress_pair_weights(cmp_w, cmp_pe):
    pairs = CMP_STRIDE // 2
    w = cmp_w.reshape(2, 2, pairs, 2, NSA_DH, NSA_DH)
    w = jnp.einsum("khpjde,gG->kpjgdhGe", w, jnp.eye(NSA_GROUPS, dtype=F32))
    w = w.reshape(2, pairs, 2 * NSA_GROUPS * NSA_DH, 2 * NSA_GROUPS * NSA_DH)
    pe = cmp_pe.reshape(2, 2, pairs, 2, 1, NSA_DH)
    pe = jnp.broadcast_to(pe, (2, 2, pairs, 2, NSA_GROUPS, NSA_DH)).transpose(0, 2, 1, 3, 4, 5)
    pe = pe.reshape(2, pairs, 2, 2 * NSA_GROUPS * NSA_DH)
    pe = jnp.concatenate([pe, jnp.zeros((2, pairs, 6, pe.shape[-1]), F32)], axis=2)
    return w.astype(BF16), pe


def _compress_core(x, w_ref, pe_ref):
    return _compress_finish(_dot(x, w_ref[...]), w_ref, pe_ref)


def _compress_finish(p, w_ref, pe_ref):
    n = p.shape[0]
    w = w_ref[...]
    hi, mid, lo = _split3(pe_ref[...])
    pb = _dot(hi, w) + _dot(mid, w) + _dot(lo, w)
    bias = pb[0:1, :NSA_KV_ROW] + pb[1:2, NSA_KV_ROW:]
    return p[:, :NSA_KV_ROW] + pltpu.roll(p[:, NSA_KV_ROW:], n - 1, 0) + bias


def _compress_prompt_kernel(x_ref, w_ref, pe_ref, o_ref):
    c = _compress_core(x_ref[0].astype(BF16), w_ref, pe_ref)
    for j in range(2 * NSA_GROUPS):
        o_ref[0, j] = c[:, j * NSA_DH:(j + 1) * NSA_DH].astype(BF16)


def _compress_prompt(kvc, b, t, w, pe):
    n = t // CMP_STRIDE
    x = kvc.reshape(b, n, CMP_STRIDE * NSA_KV_ROW)
    return pl.pallas_call(
        _compress_prompt_kernel, grid=(b,),
        in_specs=[pl.BlockSpec((1, n, CMP_STRIDE * NSA_KV_ROW), lambda i: (i, 0, 0)),
                  pl.BlockSpec(w.shape, lambda i: (0, 0)), pl.BlockSpec(pe.shape, lambda i: (0, 0))],
        out_specs=pl.BlockSpec((1, 4, n, NSA_DH), lambda i: (i, 0, 0, 0)),
        out_shape=_sds((b, 4, n, NSA_DH), BF16),
        compiler_params=_params(("arbitrary",), 48), name="nsa_compress_prompt")(x, w, pe)


def _overlap_matrix(n_cmp, n_sel):
    i = jnp.arange(n_cmp)[:, None]
    j = jnp.arange(n_sel)[None, :]
    per = SEL_BLOCK // CMP_STRIDE
    return ((i // per == j).astype(F32) + ((i + 1) // per == j).astype(F32)).astype(BF16)


def _topk_mask(score, k):
    n = score.shape[-1]
    lane = lax.broadcasted_iota(I32, score.shape, score.ndim - 1).astype(F32)
    sel = jnp.zeros_like(score)
    for _ in range(k):
        m = jnp.max(score, axis=-1, keepdims=True)
        first = jnp.min(jnp.where(score == m, lane, float(n)), axis=-1, keepdims=True)
        pick = lane == first
        sel = jnp.where(pick, 1.0, sel)
        score = jnp.where(pick, -jnp.inf, score)
    return sel


def _masked_softmax(s, mask):
    s = jnp.where(mask, s, NEG)
    m = jnp.max(s, axis=-1, keepdims=True)
    p = jnp.where(mask, jnp.exp2(s - m), 0.0)
    return p / jnp.maximum(jnp.sum(p, axis=-1, keepdims=True), TINY)


def _cmp_branch(q, ck, cv, qpos, ovl):
    tq = qpos.shape[0]
    n = ck.shape[0]
    n_sel = ovl.shape[1]
    cmp_end = CMP_STRIDE * lax.broadcasted_iota(I32, (tq, n), 1) + (CMP_BLOCK - 1)
    cmask = (cmp_end <= qpos)[None]
    s = _dot_t(q, ck).reshape(NSA_REP, tq, n)
    p = _masked_softmax(s, cmask)
    o = _dot(p.reshape(NSA_REP * tq, n).astype(BF16), cv)
    imp = jnp.sum(p, axis=0)
    hi, mid, lo = _split3(imp)
    sel_imp = _dot(hi, ovl) + _dot(mid, ovl) + _dot(lo, ovl)
    blk = lax.broadcasted_iota(I32, (tq, n_sel), 1)
    qblk = qpos // SEL_BLOCK
    forced = (blk == 0) | (blk == qblk) | (blk == qblk - 1)
    return o, jnp.where(forced, sel_imp + FORCE_BONUS, jnp.where(blk <= qblk, sel_imp, NEG))


def _select_blocks(scores, topk, sel_ref, lead):
    tq, n_sel = scores[0].shape
    sel = _topk_mask(jnp.concatenate(scores, axis=0), topk).astype(BF16)
    for g in range(NSA_GROUPS):
        sel_ref[lead + (slice(None), slice(g * n_sel, (g + 1) * n_sel))] = sel[g * tq:(g + 1) * tq]


def _nsa_cmp_prompt_kernel(q_ref, c_ref, ovl_ref, o_ref, sel_ref, *, tq):
    qpos = pl.program_id(1) * tq + lax.broadcasted_iota(I32, (tq, 1), 0)
    scores = []
    for g in range(NSA_GROUPS):
        q = q_ref[g, :, 0].reshape(NSA_REP * tq, NSA_DH)
        o, score = _cmp_branch(q, c_ref[0, g], c_ref[0, NSA_GROUPS + g], qpos, ovl_ref[...])
        scores.append(score)
        for r in range(NSA_REP):
            hd = g * NSA_REP + r
            o_ref[:, hd * NSA_DH:(hd + 1) * NSA_DH] = o[r * tq:(r + 1) * tq]
    _select_blocks(scores, min(SEL_TOPK, ovl_ref.shape[1]), sel_ref, ())


def _nsa_cmp_prompt(q, cmp, b, t, tq=256):
    n = t // CMP_STRIDE
    n_sel = t // SEL_BLOCK
    tq = min(tq, t)
    nq = t // tq
    ovl = _overlap_matrix(n, n_sel)
    q5 = q.reshape(NSA_GROUPS, NSA_REP, b, t, NSA_DH)
    return pl.pallas_call(
        functools.partial(_nsa_cmp_prompt_kernel, tq=tq), grid=(b, nq),
        in_specs=[pl.BlockSpec((NSA_GROUPS, NSA_REP, 1, tq, NSA_DH), lambda i, j: (0, 0, i, j, 0)),
                  pl.BlockSpec((1, 4, n, NSA_DH), lambda i, j: (i, 0, 0, 0)),
                  pl.BlockSpec(ovl.shape, lambda i, j: (0, 0))],
        out_specs=[pl.BlockSpec((tq, NSA_HEADS * NSA_DH), lambda i, j, nq=nq: (i * nq + j, 0)),
                   pl.BlockSpec((tq, NSA_GROUPS * n_sel), lambda i, j, nq=nq: (i * nq + j, 0))],
        out_shape=[_sds((b * t, NSA_HEADS * NSA_DH)), _sds((b * t, NSA_GROUPS * n_sel), BF16)],
        compiler_params=_params(("arbitrary", "arbitrary"), 48), name="nsa_cmp_prompt")(q5, cmp, ovl)


def _flash_update_t(q, k, v, mask_t, m_ref, acc_ref):
    s = _dot_t(k, q)
    if mask_t is not None:
        s = jnp.where(mask_t, s, NEG)
    m_prev = m_ref[...]
    m_new = jnp.maximum(m_prev, jnp.max(s, axis=0, keepdims=True))
    p = jnp.exp2(s - jnp.maximum(m_new, 0.1 * NEG))
    alpha = jnp.exp2(m_prev - m_new)
    v1 = jnp.concatenate([v, jnp.ones_like(v)], axis=1)
    acc_ref[...] = alpha * acc_ref[...] + _tdot(v1, p.astype(BF16))
    m_ref[...] = m_new


def _flash_result_t(acc):
    dh = acc.shape[0] // 2
    return acc[:dh] / jnp.maximum(acc[dh:dh + 1], TINY)


def _causal_steps(nq, tq, tk):
    qi, ki = [], []
    for j in range(nq):
        for k in range((j * tq + tq - 1) // tk + 1):
            qi.append(j)
            ki.append(k)
    return jnp.asarray(qi, I32), jnp.asarray(ki, I32)


def _nsa_swt_prompt_kernel(qi_ref, ki_ref, q_ref, ks_ref, vs_ref, kw_ref, vw_ref, sel_ref, ocmp_ref,
                           gate_ref, o_ref, m_ref, acc_ref, *, tq, tk):
    n = pl.program_id(1)
    qi = qi_ref[n]
    ki = ki_ref[n]
    q0 = qi * tq
    last = (q0 + tq - 1) // tk
    first_w = jnp.maximum(q0 - (WINDOW - 1), 0) // tk
    n_sel = sel_ref.shape[1] // NSA_GROUPS
    n_q = NSA_REP * tq

    @pl.when(ki == 0)
    def _():
        m_ref[...] = jnp.full(m_ref.shape, NEG, F32)
        acc_ref[...] = jnp.zeros(acc_ref.shape, F32)

    kpos = ki * tk + lax.broadcasted_iota(I32, (tk, tq), 0)
    qpos = q0 + lax.broadcasted_iota(I32, (tk, tq), 1)
    causal = kpos <= qpos
    heads = lambda mask: jnp.concatenate([mask] * NSA_REP, axis=1)

    expand = (lax.broadcasted_iota(I32, (tk, n_sel), 1)
              == (ki * tk + lax.broadcasted_iota(I32, (tk, n_sel), 0)) // SEL_BLOCK)
    expand = jnp.where(expand, 1.0, 0.0).astype(BF16)
    for g in range(NSA_GROUPS):
        q = q_ref[g, :, 0].reshape(n_q, NSA_DH)
        chosen = _dot_t(expand, sel_ref[:, g * n_sel:(g + 1) * n_sel])
        mask = jnp.where(causal, chosen, 0.0) > 0.5
        _flash_update_t(q, ks_ref[g, 0], vs_ref[g, 0], heads(mask), m_ref.at[g], acc_ref.at[g])

    @pl.when(ki >= first_w)
    def _():
        mask = heads(jnp.where(causal, qpos - kpos, WINDOW) < WINDOW)
        for g in range(NSA_GROUPS):
            q = q_ref[g, :, 0].reshape(n_q, NSA_DH)
            _flash_update_t(q, kw_ref[g, 0], vw_ref[g, 0], mask, m_ref.at[NSA_GROUPS + g],
                            acc_ref.at[NSA_GROUPS + g])

    @pl.when(ki == last)
    def _():
        gates = gate_ref[...]
        ocmp = ocmp_ref[...]
        for g in range(NSA_GROUPS):
            o_s = _flash_result_t(acc_ref[g])
            o_w = _flash_result_t(acc_ref[NSA_GROUPS + g])
            for r in range(NSA_REP):
                hd = g * NSA_REP + r
                lanes = slice(r * tq, (r + 1) * tq)
                cols = slice(hd * NSA_DH, (hd + 1) * NSA_DH)
                o_ref[:, cols] = (gates[:, 3 * hd:3 * hd + 1] * ocmp[:, cols]
                                  + gates[:, 3 * hd + 1:3 * hd + 2] * o_s[:, lanes].T
                                  + gates[:, 3 * hd + 2:3 * hd + 3] * o_w[:, lanes].T)


def _nsa_swt_prompt(q, kvb, sel, o_cmp, gates, b, t, tq=256, tk=1024):
    tq = min(tq, t)
    tk = min(tk, t)
    nq = t // tq
    qi, ki = _causal_steps(nq, tq, tk)
    q5 = q.reshape(NSA_GROUPS, NSA_REP, b, t, NSA_DH)
    kv4 = kvb.reshape(12, b, t, NSA_DH)

    def slc_idx(blk):
        return lambda i, n, qt, kt: (blk, i, kt[n], 0)

    def win_idx(blk):
        def f(i, n, qt, kt):
            first = jnp.maximum(qt[n] * tq - (WINDOW - 1), 0) // tk
            return (blk, i, jnp.maximum(kt[n], first), 0)
        return f

    kv_blk = (NSA_GROUPS, 1, tk, NSA_DH)
    row = lambda i, n, qt, kt: (i * nq + qt[n], 0)
    n_q = NSA_REP * tq
    grid_spec = pltpu.PrefetchScalarGridSpec(
        num_scalar_prefetch=2, grid=(b, qi.shape[0]),
        in_specs=[pl.BlockSpec((NSA_GROUPS, NSA_REP, 1, tq, NSA_DH),
                               lambda i, n, qt, kt: (0, 0, i, qt[n], 0)),
                  pl.BlockSpec(kv_blk, slc_idx(2)), pl.BlockSpec(kv_blk, slc_idx(3)),
                  pl.BlockSpec(kv_blk, win_idx(4)), pl.BlockSpec(kv_blk, win_idx(5)),
                  pl.BlockSpec((tq, sel.shape[1]), row),
                  pl.BlockSpec((tq, o_cmp.shape[1]), row),
                  pl.BlockSpec((tq, gates.shape[1]), row)],
        out_specs=pl.BlockSpec((tq, NSA_HEADS * NSA_DH), row),
        scratch_shapes=[pltpu.VMEM((2 * NSA_GROUPS, 1, n_q), F32),
                        pltpu.VMEM((2 * NSA_GROUPS, 2 * NSA_DH, n_q), F32)])
    return pl.pallas_call(
        functools.partial(_nsa_swt_prompt_kernel, tq=tq, tk=tk), grid_spec=grid_spec,
        out_shape=_sds((b * t, NSA_HEADS * NSA_DH)),
        compiler_params=_params(("arbitrary", "arbitrary"), 48),
        name="nsa_slc_win_prompt")(qi, ki, q5, kv4, kv4, kv4, kv4, sel, o_cmp, gates)


def _conv_prompt_kernel(u_ref, gb_ref, w_ref, o_ref, carry_ref):
    tm = u_ref.shape[0]

    @pl.when(pl.program_id(1) == 0)
    def _():
        carry_ref[...] = jnp.zeros(carry_ref.shape, F32)

    u = u_ref[...]
    prev = carry_ref[...]
    row = lax.broadcasted_iota(I32, u.shape, 0)
    u1 = jnp.where(row == 0, prev[7:8], pltpu.roll(u, 1, 0))
    u2 = jnp.where(row == 0, prev[6:7], jnp.where(row == 1, prev[7:8], pltpu.roll(u, 2, 0)))
    w = w_ref[...]
    o_ref[...] = gb_ref[...] * (w[0:1] * u2 + w[1:2] * u1 + w[2:3] * u)
    carry_ref[...] = u[tm - 8:]


def _conv_prompt(u, gb, conv_w, b, t, tm=1024):
    tm = min(tm, t)
    nt = t // tm
    c = u.shape[1]
    row = lambda i, j: (i * nt + j, 0)
    return pl.pallas_call(
        _conv_prompt_kernel, grid=(b, nt),
        in_specs=[pl.BlockSpec((tm, c), row), pl.BlockSpec((tm, c), row),
                  pl.BlockSpec(conv_w.shape, lambda i, j: (0, 0))],
        out_specs=pl.BlockSpec((tm, c), row), out_shape=_sds(u.shape),
        scratch_shapes=[pltpu.VMEM((8, c), F32)],
        compiler_params=_params(("arbitrary", "arbitrary"), 32), name="conv_prompt")(u, gb, conv_w)


def _matmul_res_body(n_a, rows, consts, outs):
    acc = rows[n_a][...]
    for k in range(n_a):
        acc = acc + _dot(rows[k][...].astype(BF16), consts[k][...])
    outs[0][...] = acc


def _matmul_res(acts, weights, res, name):
    r = res.shape[0]
    tm = _row_tile(r, 512)
    ws = [w.astype(BF16) for w in weights]
    return _rowwise(functools.partial(_matmul_res_body, len(acts)), list(acts) + [res], ws,
                    [_sds(res.shape)], tm, name)[0]


def _even_layer_prompt(x, b, t, g_mix, w_in, w_out, cmp_w, cmp_pe, conv_w):
    q, kvc, kvt, kvb, gates, gb, u = _even_proj(x, g_mix, w_in, seq_len=t)
    cw, cpe = _compress_weights(cmp_w, cmp_pe)
    cmp = _compress_prompt(kvc, b, t, cw, cpe)
    o_cmp, sel = _nsa_cmp_prompt(q, cmp, b, t)
    o_nsa = _nsa_swt_prompt(q, kvb, sel, o_cmp, gates, b, t)
    o_conv = _conv_prompt(u, gb, conv_w, b, t)
    qc = NSA_HEADS * NSA_DH
    y = _matmul_res([o_nsa, o_conv], [w_out[:qc], w_out[qc:]], x, "even_out_proj")
    wb = min(WINDOW, t)

    def rows(a):
        return a.reshape(b, 2, NSA_GROUPS, NSA_DH, a.shape[-1]).transpose(0, 4, 1, 2, 3)

    caches = (rows(kvt[0]), rows(kvt[1]), rows(kvt[2][:, :, t - wb:]),
              u.reshape(b, t, CONV_CH)[:, t - (CONV_W - 1):])
    return y, caches


def _channel_major(pool):
    nd = pool.ndim
    t = pool.transpose((0, 1) + tuple(range(3, nd)) + (2,))
    return t.reshape(pool.shape[0], pool.shape[1], -1, pool.shape[2])


def _page_copies(pool_ref, layer, pt_ref, batch, n_pages, buf_ref, slot, sem_ref):
    return [pltpu.make_async_copy(pool_ref.at[layer, pt_ref[batch * n_pages + p]],
                                  buf_ref.at[slot, :, pl.ds(p * PAGE_ROWS, PAGE_ROWS)],
                                  sem_ref.at[slot])
            for p in range(n_pages)]


def _paged_fetch(pool_ref, layer, pt_ref, n_pages, buf_ref, sem_ref):
    b = pl.program_id(0)
    slot = b % 2

    @pl.when(b == 0)
    def _():
        for p, c in enumerate(_page_copies(pool_ref, layer, pt_ref, b, n_pages, buf_ref, slot, sem_ref)):
            c.start(priority=p % 2)

    @pl.when(b + 1 < pl.num_programs(0))
    def _():
        for p, c in enumerate(_page_copies(pool_ref, layer, pt_ref, b + 1, n_pages, buf_ref, 1 - slot,
                                           sem_ref)):
            c.start(priority=p % 2)

    for c in _page_copies(pool_ref, layer, pt_ref, b, n_pages, buf_ref, slot, sem_ref):
        c.wait()
    return slot


def _online(state, s, mask, v, v_t=False):
    m_prev, l_prev, acc = state
    if mask is not None:
        s = jnp.where(mask, s, NEG)
    m_new = jnp.maximum(m_prev, jnp.max(s, axis=-1, keepdims=True))
    p = jnp.exp2(s - m_new)
    if mask is not None:
        p = jnp.where(mask, p, 0.0)
    alpha = jnp.exp2(m_prev - m_new)
    return (m_new, alpha * l_prev + jnp.sum(p, axis=-1, keepdims=True),
            alpha * acc + (_dot_t(p.astype(BF16), v) if v_t else _dot(p.astype(BF16), v)))


def _online_init(m, w):
    return (jnp.full((m, 1), NEG, F32), jnp.zeros((m, 1), F32), jnp.zeros((m, w), F32))


def _group_query_rows(q_ref, s):
    blocks = []
    for g in range(NSA_GROUPS):
        q = q_ref[g, :, 0].astype(F32).reshape(NSA_REP * s, NSA_DH)
        z = jnp.zeros_like(q)
        blocks.append(jnp.concatenate([q if j == g else z for j in range(2 * NSA_GROUPS)], axis=1))
    return jnp.concatenate(blocks, axis=0).astype(BF16)


def _group_rows(x):
    return jnp.concatenate([jnp.tile(xg, (NSA_REP, 1)) for xg in x], axis=0)


def _nsa_cmp_sample_kernel(pt_ref, pool_ref, q_ref, w_ref, pe_ref, ovl_ref, o_ref, sel_ref,
                           buf_ref, sem_ref, x_ref, bias_ref, *, layer, n_pages, s, topk):
    half = NSA_GROUPS * NSA_DH

    @pl.when(pl.program_id(0) == 0)
    def _():
        for kv in range(2):
            pb = None
            for jp in range(w_ref.shape[1]):
                for term in _split3(pe_ref[kv, jp]):
                    t = _dot(term, w_ref[kv, jp])
                    pb = t if pb is None else pb + t
            bias_ref[kv] = jnp.broadcast_to(pb[0:1, :half] + pb[1:2, half:], bias_ref.shape[1:])

    slot = _paged_fetch(pool_ref, layer, pt_ref, n_pages, buf_ref, sem_ref)
    past = n_pages * PAGE_ROWS
    n = past // CMP_STRIDE
    step = min(1024, past)
    for c0 in range(0, past, step):
        for kv in range(2):
            x_ref[kv, c0:c0 + step, :] = buf_ref[slot, kv * half:(kv + 1) * half, c0:c0 + step].T
    c = []
    for kv in range(2):
        p = None
        for jp in range(w_ref.shape[1]):
            x = jnp.concatenate(
                [x_ref.at[kv][pl.ds(2 * jp + jj, n, stride=CMP_STRIDE), :] for jj in range(2)],
                axis=1).astype(BF16)
            t = _dot(x, w_ref[kv, jp])
            p = t if p is None else p + t
        c.append(p[:, :half] + pltpu.roll(p[:, half:], n - 1, 0) + bias_ref[kv, 0:1])
    qpos = n_pages * PAGE_ROWS + lax.broadcasted_iota(I32, (s, 1), 0)
    scores = []
    for g in range(NSA_GROUPS):
        q = q_ref[g, :, 0].astype(F32).reshape(NSA_REP * s, NSA_DH).astype(BF16)
        ck = c[0][:, g * NSA_DH:(g + 1) * NSA_DH].astype(BF16)
        cv = c[1][:, g * NSA_DH:(g + 1) * NSA_DH].astype(BF16)
        o, score = _cmp_branch(q, ck, cv, qpos, ovl_ref[...])
        scores.append(score)
        for r in range(NSA_REP):
            hd = g * NSA_REP + r
            o_ref[0, :, hd * NSA_DH:(hd + 1) * NSA_DH] = o[r * s:(r + 1) * s]
    _select_blocks(scores, topk, sel_ref, (0,))


PAGE_ROWS = 128
PAGE_CHUNKS = PAGE_ROWS // CMP_STRIDE


def _nsa_cmp_sample(q, pool, layer, page_table, w, pe, sb, s):
    n_pages = page_table.shape[1]
    past = n_pages * PAGE_ROWS
    assert past % SEL_BLOCK == 0 and s <= CMP_STRIDE and pool.shape[2] == PAGE_ROWS
    n = past // CMP_STRIDE
    n_sel = past // SEL_BLOCK
    ovl = _overlap_matrix(n, n_sel)
    pool4 = _channel_major(pool)
    q5 = q.reshape(NSA_GROUPS, NSA_REP, sb, s, NSA_DH)
    kern = functools.partial(_nsa_cmp_sample_kernel, layer=layer, n_pages=n_pages, s=s,
                             topk=SEL_TOPK - 1)
    grid_spec = pltpu.PrefetchScalarGridSpec(
        num_scalar_prefetch=1, grid=(sb,),
        in_specs=[pl.BlockSpec(memory_space=pl.ANY),
                  pl.BlockSpec((NSA_GROUPS, NSA_REP, 1, s, NSA_DH), lambda i, pt: (0, 0, i, 0, 0)),
                  pl.BlockSpec(w.shape, lambda i, pt: (0, 0, 0, 0)),
                  pl.BlockSpec(pe.shape, lambda i, pt: (0, 0, 0, 0)),
                  pl.BlockSpec(ovl.shape, lambda i, pt: (0, 0))],
        out_specs=[pl.BlockSpec((1, s, NSA_HEADS * NSA_DH), lambda i, pt: (i, 0, 0)),
                   pl.BlockSpec((1, s, NSA_GROUPS * n_sel), lambda i, pt: (i, 0, 0))],
        scratch_shapes=[pltpu.VMEM((2, NSA_KV_ROW, past), F32), pltpu.SemaphoreType.DMA((2,)),
                        pltpu.VMEM((2, past, NSA_GROUPS * NSA_DH), F32),
                        pltpu.VMEM((2, 8, NSA_GROUPS * NSA_DH), F32)])
    return pl.pallas_call(
        kern, grid_spec=grid_spec,
        out_shape=[_sds((sb, s, NSA_HEADS * NSA_DH)), _sds((sb, s, NSA_GROUPS * n_sel), BF16)],
        compiler_params=_params(("arbitrary",), 56), name="nsa_cmp_sample")(
            page_table.reshape(-1), pool4, q5, w, pe, ovl)


def _nsa_sw_sample_kernel(pt_ref, pool_ref, q_ref, knew_ref, wnew_ref, win_ref, sel_ref, ocmp_ref,
                          gate_ref, o_ref, buf_ref, sem_ref, *, layer, n_pages, s, chunk):
    slot = _paged_fetch(pool_ref, layer, pt_ref, n_pages, buf_ref, sem_ref)
    past = n_pages * PAGE_ROWS
    n_sel = sel_ref.shape[2] // NSA_GROUPS
    m = NSA_HEADS * s
    vrow = NSA_KV_ROW // 2
    qm = _group_query_rows(q_ref, s)
    sq = lax.broadcasted_iota(I32, (s, s), 0)
    sk = lax.broadcasted_iota(I32, (s, s), 1)
    new_mask = _group_rows([sk <= sq] * NSA_GROUPS)

    state = _online_init(m, vrow)
    for c0 in range(0, past, chunk):
        kv = buf_ref[slot, :, c0:c0 + chunk].astype(BF16)
        expand = (lax.broadcasted_iota(I32, (n_sel, chunk), 0)
                  == (c0 + lax.broadcasted_iota(I32, (n_sel, chunk), 1)) // SEL_BLOCK)
        expand = jnp.where(expand, 1.0, 0.0).astype(BF16)
        chosen = [_dot(sel_ref[0, :, g * n_sel:(g + 1) * n_sel], expand) > 0.5
                  for g in range(NSA_GROUPS)]
        state = _online(state, _dot(qm, kv), _group_rows(chosen), kv[vrow:], v_t=True)
    knew = knew_ref[0].astype(BF16)
    state = _online(state, _dot_t(qm, knew), new_mask, knew[:, vrow:])
    o_slc = state[2] / jnp.maximum(state[1], TINY)

    wb = win_ref.shape[2]
    win = win_ref[0].astype(BF16)
    qrow = lax.broadcasted_iota(I32, (s, wb), 0)
    dist = qrow + wb - lax.broadcasted_iota(I32, (s, wb), 1)
    old_mask = _group_rows([(dist >= 0) & (dist < WINDOW)] * NSA_GROUPS)
    state = _online(_online_init(m, vrow), _dot(qm, win), old_mask, win[vrow:], v_t=True)
    wnew = wnew_ref[0].astype(BF16)
    state = _online(state, _dot_t(qm, wnew), new_mask, wnew[:, vrow:])
    o_win = state[2] / jnp.maximum(state[1], TINY)

    gates = gate_ref[0]
    ocmp = ocmp_ref[0]
    for g in range(NSA_GROUPS):
        vcols = slice(g * NSA_DH, (g + 1) * NSA_DH)
        for r in range(NSA_REP):
            hd = g * NSA_REP + r
            rows = slice(hd * s, (hd + 1) * s)
            cols = slice(hd * NSA_DH, (hd + 1) * NSA_DH)
            o_ref[0, :, cols] = (gates[:, 3 * hd:3 * hd + 1] * ocmp[:, cols]
                                 + gates[:, 3 * hd + 1:3 * hd + 2] * o_slc[rows, vcols]
                                 + gates[:, 3 * hd + 2:3 * hd + 3] * o_win[rows, vcols])


def _nsa_sw_sample(q, pool, layer, page_table, kvs_new, kvw_new, win_buf, sel, o_cmp, gates, sb, s):
    n_pages = page_table.shape[1]
    past = n_pages * PAGE_ROWS
    wb = win_buf.shape[1]
    assert wb == WINDOW and s <= SEL_BLOCK and past % SEL_BLOCK == 0
    pool4 = _channel_major(pool)
    q5 = q.reshape(NSA_GROUPS, NSA_REP, sb, s, NSA_DH)
    per_b = lambda a: a.reshape(sb, s, a.shape[-1])
    blk = lambda a: pl.BlockSpec((1,) + a.shape[1:], lambda i, pt: (i, 0, 0))
    win_t = _channel_major(win_buf[None])[0]
    rows = [per_b(kvs_new), per_b(kvw_new), win_t, sel, o_cmp, per_b(gates)]
    kern = functools.partial(_nsa_sw_sample_kernel, layer=layer, n_pages=n_pages, s=s,
                             chunk=min(2048, past))
    grid_spec = pltpu.PrefetchScalarGridSpec(
        num_scalar_prefetch=1, grid=(sb,),
        in_specs=[pl.BlockSpec(memory_space=pl.ANY),
                  pl.BlockSpec((NSA_GROUPS, NSA_REP, 1, s, NSA_DH), lambda i, pt: (0, 0, i, 0, 0))]
        + [blk(a) for a in rows],
        out_specs=pl.BlockSpec((1, s, NSA_HEADS * NSA_DH), lambda i, pt: (i, 0, 0)),
        scratch_shapes=[pltpu.VMEM((2, NSA_KV_ROW, past), F32), pltpu.SemaphoreType.DMA((2,))])
    return pl.pallas_call(
        kern, grid_spec=grid_spec, out_shape=_sds((sb, s, NSA_HEADS * NSA_DH)),
        compiler_params=_params(("arbitrary",), 56), name="nsa_slc_win_sample")(
            page_table.reshape(-1), pool4, q5, *rows)


def _conv_sample_kernel(u_ref, gb_ref, cb_ref, w_ref, o_ref):
    u = u_ref[...]
    cb = cb_ref[...]
    row = lax.broadcasted_iota(I32, u.shape, 1)
    u1 = jnp.where(row == 0, cb[:, 1:2], pltpu.roll(u, 1, 1))
    u2 = jnp.where(row == 0, cb[:, 0:1], jnp.where(row == 1, cb[:, 1:2], pltpu.roll(u, 2, 1)))
    w = w_ref[...]
    o_ref[...] = gb_ref[...] * (w[0:1][None] * u2 + w[1:2][None] * u1 + w[2:3][None] * u)


def _conv_sample(u, gb, conv_buf, conv_w, sb, s):
    assert s >= CONV_W - 1
    c = u.shape[1]
    full = lambda shape: pl.BlockSpec(shape, lambda i, n=len(shape): (0,) * n)
    return pl.pallas_call(
        _conv_sample_kernel, grid=(1,),
        in_specs=[full((sb, s, c)), full((sb, s, c)), full(conv_buf.shape), full(conv_w.shape)],
        out_specs=full((sb, s, c)), out_shape=_sds((sb, s, c)),
        compiler_params=_params(("arbitrary",), 32), name="conv_sample")(
            u.reshape(sb, s, c), gb.reshape(sb, s, c), conv_buf, conv_w).reshape(sb * s, c)


def _even_layer_sample(x, sb, s, layer, pool_cmp, pool_slc, win_buf, conv_buf, page_table,
                       g_mix, w_in, w_out, cmp_w, cmp_pe, conv_w):
    q, kvc, kvs, kvw, _, gates, gb, u = _even_proj(x, g_mix, w_in)
    cw, cpe = _compress_pair_weights(cmp_w, cmp_pe)
    o_cmp, sel = _nsa_cmp_sample(q, pool_cmp, layer, page_table, cw, cpe, sb, s)
    o_nsa = _nsa_sw_sample(q, pool_slc, layer, page_table, kvs, kvw, win_buf, sel, o_cmp, gates,
                           sb, s)
    o_conv = _conv_sample(u, gb, conv_buf, conv_w, sb, s)
    qc = NSA_HEADS * NSA_DH
    y = _matmul_res([o_nsa.reshape(sb * s, qc), o_conv], [w_out[:qc], w_out[qc:]], x,
                    "even_out_proj")
    five = lambda a: a.reshape(sb, s, 2, NSA_GROUPS, NSA_DH)
    win_all = jnp.concatenate([win_buf, five(kvw)], axis=1)[:, s:]
    conv_all = jnp.concatenate([conv_buf, u.reshape(sb, s, CONV_CH)], axis=1)[:, s:]
    return y, (five(kvc), five(kvs), win_all, conv_all)


def _rot_half_cols(w):
    half = w.shape[-1] // 2
    return jnp.concatenate([-w[..., half:], w[..., :half]], axis=-1)


def _rope_tables(pos):
    half = ROPE // 2
    inv = ROPE_THETA ** (-jnp.arange(half, dtype=F32) / half)
    ang = pos.astype(F32)[:, None] * inv[None, :]
    cos, sin = jnp.cos(ang), jnp.sin(ang)
    cos32 = jnp.concatenate([cos, cos], axis=1)
    sin32 = jnp.concatenate([sin, sin], axis=1)
    n = pos.shape[0]
    pad = MLA_QK - NOPE - ROPE
    cosq = jnp.concatenate([jnp.ones((n, NOPE), F32), cos32, jnp.zeros((n, pad), F32)], axis=1)
    sinq = jnp.concatenate([jnp.zeros((n, NOPE), F32), sin32, jnp.zeros((n, pad), F32)], axis=1)
    return cos32, sin32, cosq, sinq


def _mla_weights(w_in, w_uq, w_uk):
    wcq = w_in[:, :Q_LORA]
    wckv = w_in[:, Q_LORA:Q_LORA + KV_LORA]
    wkpe = w_in[:, Q_LORA + KV_LORA:]
    uq = w_uq.reshape(Q_LORA, MLA_HEADS, NOPE + ROPE)
    pad = jnp.zeros((Q_LORA, MLA_HEADS, MLA_QK - NOPE - ROPE), F32)
    wq = jnp.concatenate([uq, pad], axis=-1).reshape(Q_LORA, MLA_HEADS * MLA_QK)
    wq_rot = jnp.concatenate([jnp.zeros((Q_LORA, MLA_HEADS, NOPE), F32),
                              _rot_half_cols(uq[..., NOPE:]), pad], axis=-1).reshape(wq.shape)
    kpad = jnp.zeros((KV_LORA, MLA_HEADS, MLA_QK - NOPE), F32)
    wk_lat = jnp.concatenate([w_uk, kpad], axis=-1).reshape(KV_LORA, MLA_HEADS * MLA_QK)
    eye = jnp.broadcast_to(jnp.eye(ROPE, dtype=F32)[:, None, :], (ROPE, MLA_HEADS, ROPE))
    wk_pe = jnp.concatenate([jnp.zeros((ROPE, MLA_HEADS, NOPE), F32), eye,
                             jnp.zeros((ROPE, MLA_HEADS, MLA_QK - NOPE - ROPE), F32)],
                            axis=-1).reshape(ROPE, MLA_HEADS * MLA_QK)
    bf = lambda a: a.astype(BF16)
    return dict(wcq=bf(wcq), wckv=bf(wckv), wkpe=bf(wkpe), wkpe_rot=bf(_rot_half_cols(wkpe)),
                wq=bf(wq), wq_rot=bf(wq_rot), wk_lat=bf(wk_lat), wk_pe=bf(wk_pe))


def _mla_latent(x_ref, cos_ref, sin_ref, g_ref, qn_ref, kvn_ref, wcq_ref, wckv_ref, wkpe_ref,
                wkrot_ref, lat_ref):
    h = _rms(x_ref[...], g_ref[...]).astype(BF16)
    cq = _rms(_dot(h, wcq_ref[...]), qn_ref[...]).astype(BF16)
    ckv = _rms(_dot(h, wckv_ref[...]), kvn_ref[...])
    kpe = _dot(h, wkpe_ref[...]) * cos_ref[...] + _dot(h, wkrot_ref[...]) * sin_ref[...]
    lat_ref[:, :KV_LORA] = ckv
    lat_ref[:, KV_LORA:] = kpe
    return cq, ckv.astype(BF16), kpe.astype(BF16)


def _mla_queries(cq, wq_ref, wqrot_ref, cosq, sinq, head):
    cols = slice(head * MLA_QK, (head + 1) * MLA_QK)
    q = _dot(cq, wq_ref[:, cols]) * cosq + _dot(cq, wqrot_ref[:, cols]) * sinq
    return q * ((NOPE + ROPE) ** -0.5 * LOG2E)


def _mla_proj_prompt_body(rows, consts, outs):
    x_ref, cos_ref, sin_ref, cosq_ref, sinq_ref = rows
    (g_ref, qn_ref, kvn_ref, wcq_ref, wckv_ref, wkpe_ref, wkrot_ref, wq_ref, wqrot_ref,
     wklat_ref, wkpe2_ref, wuv_ref) = consts
    lat_ref, q_ref, k_ref, v_ref = outs
    cq, ckv, kpe = _mla_latent(x_ref, cos_ref, sin_ref, g_ref, qn_ref, kvn_ref, wcq_ref,
                               wckv_ref, wkpe_ref, wkrot_ref, lat_ref)
    cosq, sinq = cosq_ref[...], sinq_ref[...]
    for hd in range(MLA_HEADS):
        cols = slice(hd * MLA_QK, (hd + 1) * MLA_QK)
        q_ref[:, cols] = _mla_queries(cq, wq_ref, wqrot_ref, cosq, sinq, hd).astype(BF16)
    k_ref[...] = (_dot(ckv, wklat_ref[...]) + _dot(kpe, wkpe2_ref[...])).astype(BF16)
    v_ref[...] = _dot(ckv, wuv_ref[...]).astype(BF16)


def _mla_proj_sample_body(rows, consts, outs):
    x_ref, cos_ref, sin_ref, cosq_ref, sinq_ref = rows
    (g_ref, qn_ref, kvn_ref, wcq_ref, wckv_ref, wkpe_ref, wkrot_ref, wq_ref, wqrot_ref,
     wukt_ref) = consts
    lat_ref, q_ref = outs
    cq, _, _ = _mla_latent(x_ref, cos_ref, sin_ref, g_ref, qn_ref, kvn_ref, wcq_ref, wckv_ref,
                           wkpe_ref, wkrot_ref, lat_ref)
    cosq, sinq = cosq_ref[...], sinq_ref[...]
    for hd in range(MLA_HEADS):
        q = _mla_queries(cq, wq_ref, wqrot_ref, cosq, sinq, hd)
        q_ref[hd, :, :KV_LORA] = _dot(q[:, :NOPE].astype(BF16), wukt_ref[hd])
        q_ref[hd, :, KV_LORA:] = q[:, NOPE:NOPE + ROPE]


def _mla_proj(x, pos, g_mix, q_norm, kv_norm, mw, extra, body, outs, name):
    r, d = x.shape
    tm = _row_tile(pos.shape[0], 512)
    cos32, sin32, cosq, sinq = _rope_tables(pos)
    n_per = pos.shape[0] // tm
    periods = {k: n_per for k in (1, 2, 3, 4)}
    consts = [g_mix.reshape(1, d), q_norm.reshape(1, -1), kv_norm.reshape(1, -1), mw["wcq"],
              mw["wckv"], mw["wkpe"], mw["wkpe_rot"], mw["wq"], mw["wq_rot"]] + extra
    return _rowwise(body, [x, cos32, sin32, cosq, sinq], consts, outs, tm, name, periods=periods)


def _mla_flash_t_kernel(qi_ref, ki_ref, q_ref, k_ref, v_ref, o_ref, m_ref, acc_ref, *, tq, tk):
    n = pl.program_id(2)
    qi = qi_ref[n]
    ki = ki_ref[n]
    last = (qi * tq + tq - 1) // tk

    @pl.when(ki == 0)
    def _():
        m_ref[...] = jnp.full(m_ref.shape, NEG, F32)
        acc_ref[...] = jnp.zeros(acc_ref.shape, F32)

    def step(masked):
        mask = None
        if masked:
            mask = (ki * tk + lax.broadcasted_iota(I32, (tk, tq), 0)
                    <= qi * tq + lax.broadcasted_iota(I32, (tk, tq), 1))
        for j in range(2):
            qk = slice(j * MLA_QK, (j + 1) * MLA_QK)
            _flash_update_t(q_ref[:, qk], k_ref[:, qk], v_ref[:, j * V_DIM:(j + 1) * V_DIM], mask,
                            m_ref.at[j], acc_ref.at[j])

    needs_mask = ki * tk + tk - 1 > qi * tq

    @pl.when(needs_mask)
    def _():
        step(True)

    @pl.when(jnp.logical_not(needs_mask))
    def _():
        step(False)

    @pl.when(ki == last)
    def _():
        for j in range(2):
            o_ref[:, j * V_DIM:(j + 1) * V_DIM] = _flash_result_t(acc_ref[j]).T


def _mla_flash_t(q, k, v, b, t, tq=1024, tk=1024):
    tq, tk = min(tq, t), min(tk, t)
    nq, nk = t // tq, t // tk
    qi, ki = _causal_steps(nq, tq, tk)
    pairs = MLA_HEADS // 2
    grid_spec = pltpu.PrefetchScalarGridSpec(
        num_scalar_prefetch=2, grid=(b, pairs, qi.shape[0]),
        in_specs=[pl.BlockSpec((tq, 2 * MLA_QK), lambda i, h, n, qt, kt: (i * nq + qt[n], h)),
                  pl.BlockSpec((tk, 2 * MLA_QK), lambda i, h, n, qt, kt: (i * nk + kt[n], h)),
                  pl.BlockSpec((tk, 2 * V_DIM), lambda i, h, n, qt, kt: (i * nk + kt[n], h))],
        out_specs=pl.BlockSpec((tq, 2 * V_DIM), lambda i, h, n, qt, kt: (i * nq + qt[n], h)),
        scratch_shapes=[pltpu.VMEM((2, 1, tq), F32), pltpu.VMEM((2, 2 * V_DIM, tq), F32)])
    return pl.pallas_call(
        functools.partial(_mla_flash_t_kernel, tq=tq, tk=tk), grid_spec=grid_spec,
        out_shape=_sds((b * t, MLA_HEADS * V_DIM)),
        compiler_params=_params(("arbitrary", "arbitrary", "arbitrary"), 48),
        name="mla_flash_prompt")(qi, ki, q, k, v)


def _odd_layer_prompt(x, b, t, g_mix, w_in, q_norm, w_uq, kv_norm, w_uk, w_uv, w_out):
    mw = _mla_weights(w_in, w_uq, w_uk)
    r = x.shape[0]
    outs = [_sds((r, LAT)), _sds((r, MLA_HEADS * MLA_QK), BF16), _sds((r, MLA_HEADS * MLA_QK), BF16),
            _sds((r, MLA_HEADS * V_DIM), BF16)]
    extra = [mw["wk_lat"], mw["wk_pe"], w_uv.reshape(KV_LORA, MLA_HEADS * V_DIM).astype(BF16)]
    lat, q, k, v = _mla_proj(x, jnp.arange(t, dtype=I32), g_mix, q_norm, kv_norm, mw, extra,
                             _mla_proj_prompt_body, outs, "mla_proj_prompt")
    o = _mla_flash_t(q, k, v, b, t)
    y = _matmul_res([o], [w_out], x, "mla_out_proj")
    return y, lat.reshape(b, t, LAT)


def _mla_sample_kernel(pt_ref, pool_ref, q_ref, new_ref, o_ref, buf_ref, sem_ref, *,
                       layer, n_pages, s, chunk):
    slot = _paged_fetch(pool_ref, layer, pt_ref, n_pages, buf_ref, sem_ref)
    past = n_pages * PAGE_ROWS
    m = MLA_HEADS * s
    q = q_ref[:, 0].reshape(m, LAT).astype(BF16)
    state = _online_init(m, KV_LORA)
    for c0 in range(0, past, chunk):
        kv = buf_ref[slot, :, c0:c0 + chunk].astype(BF16)
        state = _online(state, _dot(q, kv), None, kv[:KV_LORA], v_t=True)
    new = new_ref[0].astype(BF16)
    causal = (lax.broadcasted_iota(I32, (s, s), 1) <= lax.broadcasted_iota(I32, (s, s), 0))
    state = _online(state, _dot_t(q, new), jnp.tile(causal, (MLA_HEADS, 1)), new[:, :KV_LORA])
    o = state[2] / jnp.maximum(state[1], TINY)
    o_ref[:, 0] = o.reshape(MLA_HEADS, s, KV_LORA)


def _mla_sample_attend(q, pool, layer, page_table, lat_new, sb, s):
    n_pages = page_table.shape[1]
    past = n_pages * PAGE_ROWS
    q4 = q.reshape(MLA_HEADS, sb, s, LAT)
    kern = functools.partial(_mla_sample_kernel, layer=layer, n_pages=n_pages, s=s,
                             chunk=min(2048, past))
    grid_spec = pltpu.PrefetchScalarGridSpec(
        num_scalar_prefetch=1, grid=(sb,),
        in_specs=[pl.BlockSpec(memory_space=pl.ANY),
                  pl.BlockSpec((MLA_HEADS, 1, s, LAT), lambda i, pt: (0, i, 0, 0)),
                  pl.BlockSpec((1, s, LAT), lambda i, pt: (i, 0, 0))],
        out_specs=pl.BlockSpec((MLA_HEADS, 1, s, KV_LORA), lambda i, pt: (0, i, 0, 0)),
        scratch_shapes=[pltpu.VMEM((2, LAT, past), F32), pltpu.SemaphoreType.DMA((2,))])
    o = pl.pallas_call(
        kern, grid_spec=grid_spec, out_shape=_sds((MLA_HEADS, sb, s, KV_LORA)),
        compiler_params=_params(("arbitrary",), 56), name="mla_paged_sample")(
            page_table.reshape(-1), _channel_major(pool), q4, lat_new.reshape(sb, s, LAT))
    return o.reshape(MLA_HEADS, sb * s, KV_LORA)


def _mla_out_sample_body(rows, consts, outs):
    o_ref, res_ref = rows
    wuv_ref, wout_ref = consts
    acc = res_ref[...]
    for hd in range(MLA_HEADS):
        o_h = _dot(o_ref[hd].astype(BF16), wuv_ref[hd])
        acc = acc + _dot(o_h.astype(BF16), wout_ref[hd])
    outs[0][...] = acc


def _odd_layer_sample(x, sb, s, layer, pool, page_table, g_mix, w_in, q_norm, w_uq, kv_norm, w_uk,
                      w_uv, w_out):
    mw = _mla_weights(w_in, w_uq, w_uk)
    r = x.shape[0]
    past = page_table.shape[1] * PAGE_ROWS
    pos = past + jnp.arange(_row_tile(r, 512), dtype=I32) % s
    outs = [_sds((r, LAT)), _sds((MLA_HEADS, r, LAT))]
    wukt = w_uk.transpose(1, 2, 0).astype(BF16)
    lat, q = _mla_proj(x, pos, g_mix, q_norm, kv_norm, mw, [wukt],
                       _mla_proj_sample_body, outs, "mla_proj_sample")
    o_lat = _mla_sample_attend(q, pool, layer, page_table, lat, sb, s)
    wuv = w_uv.transpose(1, 0, 2).astype(BF16)
    wout = w_out.reshape(MLA_HEADS, V_DIM, -1).astype(BF16)
    y = _rowwise(_mla_out_sample_body, [o_lat, x], [wuv, wout], [_sds(x.shape)],
                 _row_tile(r, 512), "mla_out_sample")[0]
    return y, lat.reshape(sb, s, LAT)


def _plain_matmul_body(rows, consts, outs):
    outs[0][...] = _dot(rows[0][...].astype(BF16), consts[0][...])


def _mem_q_body(rows, consts, outs):
    g_ref, w_ref = consts
    h = _rms(rows[0][...], g_ref[...]).astype(BF16)
    dh = w_ref.shape[1] // MEM_HEADS
    outs[0][...] = (_dot(h, w_ref[...]) * (dh ** -0.5 * LOG2E)).astype(BF16)


def _softmax_rows(s):
    p = jnp.exp2(s - jnp.max(s, axis=-1, keepdims=True))
    return p / jnp.sum(p, axis=-1, keepdims=True)


def _mem_attn_kernel(q_ref, kv_ref, o_ref):
    d = q_ref.shape[2]
    dh = d // MEM_HEADS
    parts = dh // kv_ref.shape[5]
    for bi in range(q_ref.shape[0]):
        for hd in range(MEM_HEADS):
            rows = lambda kv: jnp.concatenate(
                [kv_ref[0, bi, :, kv, part * MEM_HEADS + hd, :] for part in range(parts)],
                axis=1).astype(BF16)
            p = _softmax_rows(_dot_t(q_ref[bi, :, hd * dh:(hd + 1) * dh], rows(0)))
            o_ref[bi, :, hd * dh:(hd + 1) * dh] = _dot(p.astype(BF16), rows(1)).astype(BF16)


def _mem_layer_cached(y, nb, cache, layer, g_mem, w_q, w_o):
    r, d = y.shape
    rows = r // nb
    q = _rowwise(_mem_q_body, [y], [g_mem.reshape(1, d), w_q.astype(BF16)], [_sds((r, d), BF16)],
                 _row_tile(r, 512), "mem_q_proj")[0]
    nl, _, mem, two, heads, dh = cache.shape
    parts = dh // V7X_LANES
    kv = cache.reshape(nl, nb, mem, two, heads, parts, V7X_LANES).transpose(0, 1, 2, 3, 5, 4, 6)
    kv = kv.reshape(nl, nb, mem, two, parts * heads, V7X_LANES)
    nbk = 2 if nb % 2 == 0 else 1
    o = pl.pallas_call(
        _mem_attn_kernel, grid=(nb // nbk,),
        in_specs=[pl.BlockSpec((nbk, rows, d), lambda i: (i, 0, 0)),
                  pl.BlockSpec((1, nbk) + kv.shape[2:], lambda i: (layer, i, 0, 0, 0, 0))],
        out_specs=pl.BlockSpec((nbk, rows, d), lambda i: (i, 0, 0)),
        out_shape=_sds((nb, rows, d), BF16),
        compiler_params=_params(("arbitrary",), 48), name="mem_attention")(
            q.reshape(nb, rows, d), kv)
    return _matmul_res([o.reshape(r, d)], [w_o], y, "mem_out_proj")


def _mem_fused_kernel(y_ref, kv_ref, g_ref, wq_ref, wo_ref, o_ref):
    d = y_ref.shape[1]
    dh = d // MEM_HEADS
    y = y_ref[...]
    h = _rms(y, g_ref[...]).astype(BF16)
    q = (_dot(h, wq_ref[...]) * (dh ** -0.5 * LOG2E)).astype(BF16)
    acc = y
    for hd in range(MEM_HEADS):
        cols = slice(hd * dh, (hd + 1) * dh)
        k = kv_ref[0, :, cols].astype(BF16)
        v = kv_ref[0, :, d + hd * dh:d + (hd + 1) * dh].astype(BF16)
        p = _softmax_rows(_dot_t(q[:, cols], k))
        o = _dot(p.astype(BF16), v).astype(BF16)
        acc = acc + _dot(o, wo_ref[cols, :])
    o_ref[...] = acc


def _mem_layer_fused(y, nb, kv, g_mem, w_q, w_o):
    r, d = y.shape
    rows = r // nb
    tq = _row_tile(rows, 512)
    nt = rows // tq
    full = lambda a: pl.BlockSpec(a.shape, lambda i, j, n=a.ndim: (0,) * n)
    consts = [g_mem.reshape(1, d), w_q.astype(BF16), w_o.astype(BF16)]
    return pl.pallas_call(
        _mem_fused_kernel, grid=(nb, nt),
        in_specs=[pl.BlockSpec((tq, d), lambda i, j: (i * nt + j, 0)),
                  pl.BlockSpec((1,) + kv.shape[1:], lambda i, j: (i, 0, 0))]
        + [full(c) for c in consts],
        out_specs=pl.BlockSpec((tq, d), lambda i, j: (i * nt + j, 0)),
        out_shape=_sds((r, d)),
        compiler_params=_params(("arbitrary", "arbitrary"), 48), name="mem_layer_prompt")(
            y, kv, *consts)


def _mem_project(mem, w_kv):
    nb, m, d = mem.shape
    kv = _rowwise(_plain_matmul_body, [mem.reshape(nb * m, d)], [w_kv.astype(BF16)],
                  [_sds((nb * m, w_kv.shape[1]))], _row_tile(nb * m, 512), "mem_kv_proj")[0]
    return kv.reshape(nb, m, w_kv.shape[1])


def _route(x, wr_ref, br_ref):
    xh, xm, _ = _split3(x)
    w0, w1 = wr_ref[0], wr_ref[1]
    logit = _dot(xh, w0) + (_dot(xh, w1) + _dot(xm, w0)) + br_ref[...]
    lane = lax.broadcasted_iota(I32, logit.shape, 1).astype(F32)
    big = float(ROUTE_LANES)
    is_g = lane < N_GROUPS
    gl = jnp.where(is_g, logit, -jnp.inf)
    gmax = jnp.max(gl, axis=-1, keepdims=True)
    gidx = jnp.min(jnp.where(gl == gmax, lane, big), axis=-1, keepdims=True)
    g_gate = 1.0 / jnp.sum(jnp.where(is_g, jnp.exp(gl - gmax), 0.0), axis=-1, keepdims=True)
    lo = N_GROUPS + gidx * EXP_PER_GROUP
    in_grp = (lane >= lo) & (lane < lo + EXP_PER_GROUP)
    el = jnp.where(in_grp, logit, -jnp.inf)
    ee = jnp.where(in_grp, jnp.exp(el - jnp.max(el, axis=-1, keepdims=True)), 0.0)
    prob = ee / jnp.sum(ee, axis=-1, keepdims=True)
    cand = jnp.where(in_grp, prob, -1.0)
    p1 = jnp.max(cand, axis=-1, keepdims=True)
    i1 = jnp.min(jnp.where(cand == p1, lane, big), axis=-1, keepdims=True)
    cand = jnp.where(lane == i1, -1.0, cand)
    p2 = jnp.max(cand, axis=-1, keepdims=True)
    i2 = jnp.min(jnp.where(cand == p2, lane, big), axis=-1, keepdims=True)
    tot = p1 + p2
    return (jnp.where(lane == i1, p1 / tot * g_gate, 0.0)
            + jnp.where(lane == i2, p2 / tot * g_gate, 0.0))


def _moe_kernel(y_ref, g_ref, wr_ref, br_ref, w13_ref, w2_ref, gf_ref, o_ref, h_ref, comb_ref,
                acc_ref, *, final_norm):
    grp = pl.program_id(1)

    @pl.when(grp == 0)
    def _():
        x = _rms(y_ref[...], g_ref[...])
        h_ref[...] = x.astype(BF16)
        comb_ref[...] = _route(x, wr_ref, br_ref)
        acc_ref[...] = jnp.zeros(acc_ref.shape, F32)

    hc = _dot(h_ref[...], w13_ref[0])
    half = hc.shape[1] // 2
    ff = half // EXP_PER_GROUP
    comb = comb_ref[...]
    lane = lax.broadcasted_iota(I32, comb.shape, 1)
    acts = []
    for j in range(EXP_PER_GROUP):
        e_lane = N_GROUPS + grp * EXP_PER_GROUP + j
        ce = jnp.sum(jnp.where(lane == e_lane, comb, 0.0), axis=-1, keepdims=True)
        a = hc[:, j * ff:(j + 1) * ff]
        bb = hc[:, half + j * ff:half + (j + 1) * ff]
        acts.append((a * (1.0 / (1.0 + jnp.exp(-a))) * bb * ce).astype(BF16))
    acc_ref[...] += _dot(jnp.concatenate(acts, axis=1), w2_ref[0])

    @pl.when(grp == pl.num_programs(1) - 1)
    def _():
        out = y_ref[...] + acc_ref[...]
        if final_norm:
            out = _rms(out, gf_ref[...])
        o_ref[...] = out


def _moe_layer(y, g_ffn, w_group, b_group, w_expert, b_expert, w13, w2, g_final=None):
    r, d = y.shape
    tm = _row_tile(r, 1024)
    n_route = N_GROUPS + N_EXPERTS
    wr = jnp.concatenate([w_group, w_expert, jnp.zeros((d, ROUTE_LANES - n_route), F32)], axis=1)
    wr = jnp.stack(_split3(wr)[:2])
    br = jnp.concatenate([b_group, b_expert, jnp.zeros((ROUTE_LANES - n_route,), F32)]).reshape(1, -1)
    gf = (g_ffn if g_final is None else g_final).reshape(1, d)
    ff = w13.shape[2] // 2
    per = EXP_PER_GROUP
    w13g = w13.reshape(N_GROUPS, per, d, 2, ff).transpose(0, 2, 3, 1, 4).reshape(
        N_GROUPS, d, 2 * per * ff).astype(BF16)
    w2g = w2.reshape(N_GROUPS, per * ff, d).astype(BF16)
    return pl.pallas_call(
        functools.partial(_moe_kernel, final_norm=g_final is not None), grid=(r // tm, N_GROUPS),
        in_specs=[pl.BlockSpec((tm, d), lambda i, e: (i, 0)),
                  pl.BlockSpec((1, d), lambda i, e: (0, 0)),
                  pl.BlockSpec(wr.shape, lambda i, e: (0, 0, 0)),
                  pl.BlockSpec(br.shape, lambda i, e: (0, 0)),
                  pl.BlockSpec((1, d, 2 * per * ff), lambda i, e: (e, 0, 0)),
                  pl.BlockSpec((1, per * ff, d), lambda i, e: (e, 0, 0)),
                  pl.BlockSpec((1, d), lambda i, e: (0, 0))],
        out_specs=pl.BlockSpec((tm, d), lambda i, e: (i, 0)),
        out_shape=_sds((r, d)),
        scratch_shapes=[pltpu.VMEM((tm, d), BF16), pltpu.VMEM((tm, ROUTE_LANES), F32),
                        pltpu.VMEM((tm, d), F32)],
        compiler_params=_params(("arbitrary", "arbitrary"), 56), name="hier_moe")(
            y, g_ffn.reshape(1, d), wr, br, w13g, w2g, gf)


def kernel(x_prompt, x_sample, mem_prompt, cache_nsa_cmp_kv, cache_nsa_slc_kv, state_nsa_win_kv, state_conv, cache_mla_latent, cache_mem_kv, page_table, norm_mix, norm_mem, norm_ffn, norm_final, e_w_in, e_w_out, e_cmp_w, e_cmp_pe, e_conv_w, o_w_in, o_q_norm, o_w_uq, o_kv_norm, o_w_uk, o_w_uv, o_w_out, mem_w_q, mem_w_kv, mem_w_o, moe_w_group, moe_b_group, moe_w_expert, moe_b_expert, moe_w13, moe_w2):
    b, t, d = x_prompt.shape
    sb, s, _ = x_sample.shape
    depth = norm_mix.shape[0]
    yp = x_prompt.reshape(b * t, d)
    ys = x_sample.reshape(sb * s, d)
    p_even, s_even, p_mla, s_mla, p_mem = [], [], [], [], []
    for l in range(depth):
        if l % 2 == 0:
            e = l // 2
            w = (norm_mix[l], e_w_in[e], e_w_out[e], e_cmp_w[e], e_cmp_pe[e], e_conv_w[e])
            yp, cp = _even_layer_prompt(yp, b, t, *w)
            ys, cs = _even_layer_sample(ys, sb, s, e, cache_nsa_cmp_kv, cache_nsa_slc_kv,
                                        state_nsa_win_kv[e], state_conv[e], page_table, *w)
            p_even.append(cp)
            s_even.append(cs)
        else:
            o = l // 2
            w = (norm_mix[l], o_w_in[o], o_q_norm[o], o_w_uq[o], o_kv_norm[o], o_w_uk[o], o_w_uv[o],
                 o_w_out[o])
            yp, lp = _odd_layer_prompt(yp, b, t, *w)
            ys, ls = _odd_layer_sample(ys, sb, s, o, cache_mla_latent, page_table, *w)
            p_mla.append(lp)
            s_mla.append(ls)
        kv_p = _mem_project(mem_prompt, mem_w_kv[l])
        p_mem.append(kv_p.reshape(b, kv_p.shape[1], 2, MEM_HEADS, d // MEM_HEADS))
        yp = _mem_layer_fused(yp, b, kv_p, norm_mem[l], mem_w_q[l], mem_w_o[l])
        ys = _mem_layer_cached(ys, sb, cache_mem_kv, l, norm_mem[l], mem_w_q[l], mem_w_o[l])
        g_final = norm_final if l == depth - 1 else None
        moe_w = (norm_ffn[l], moe_w_group[l], moe_b_group[l], moe_w_expert[l], moe_b_expert[l],
                 moe_w13[l], moe_w2[l])
        yp = _moe_layer(yp, *moe_w, g_final=g_final)
        ys = _moe_layer(ys, *moe_w, g_final=g_final)
    stack = lambda caches, k: jnp.stack([c[k] for c in caches])
    return (yp.reshape(b, t, d), ys.reshape(sb, s, d),
            stack(p_even, 0), stack(p_even, 1), stack(p_even, 2), stack(p_even, 3),
            jnp.stack(p_mla), jnp.stack(p_mem),
            stack(s_even, 0), stack(s_even, 1), stack(s_even, 2), stack(s_even, 3),
            jnp.stack(s_mla))
```
